```python
import jax, jax.numpy as jnp
from jax import lax
import numpy as np

D_MODEL = 1024
BATCH = 8
SEQ = 2048
DEPTH = 2
DEC_BATCH = 128
DEC_SEQ = 4
PAST_LEN = 16384
PAGE_SIZE = 128

EXPAND = 2
E_A = EXPAND * D_MODEL
HEAD_K = 128
N_HEADS_A = E_A // HEAD_K
HEAD_V = E_A // N_HEADS_A
CHUNK_A = 64
E_B = EXPAND * D_MODEL
CHUNK_B = 128
GROUP_B = 128
N_GROUPS_B = E_B // GROUP_B
N_A_LAYERS = (DEPTH + 1) // 2
N_B_LAYERS = DEPTH // 2
ALPHA = (2 * DEPTH) ** 0.25
BETA = (8 * DEPTH) ** -0.25
LN_EPS = 1e-5

kernel_name = "hgrn2_chunk_gmlp_hybrid_step"


def layer_norm(x, g, b):
    xf = x.astype(jnp.float32)
    mu = jnp.mean(xf, axis=-1, keepdims=True)
    var = jnp.mean(jnp.square(xf - mu), axis=-1, keepdims=True)
    y = (xf - mu) * lax.rsqrt(var + LN_EPS) * g.astype(jnp.float32) + b.astype(jnp.float32)
    return y.astype(x.dtype)


def hgrn2_recurrence(q, log_f, v, s0):
    bsz, seq_len, n_heads, _ = q.shape
    pad = (-seq_len) % CHUNK_A
    n_chunks = (seq_len + pad) // CHUNK_A

    def to_chunks(t):
        t = jnp.pad(t.astype(jnp.float32), ((0, 0), (0, pad), (0, 0), (0, 0)))
        return t.reshape(bsz, n_chunks, CHUNK_A, n_heads, t.shape[-1]).transpose(1, 0, 3, 2, 4)

    qc, lfc, vc = to_chunks(q), to_chunks(log_f), to_chunks(v)
    causal = jnp.tril(jnp.ones((CHUNK_A, CHUNK_A), dtype=bool))

    def step(s, inp):
        qn, lfn, vn = inp
        kn = -jnp.expm1(lfn)
        cum = jnp.cumsum(lfn, axis=2)
        cum_last = cum[:, :, -1:, :]
        q_dec = qn * jnp.exp(cum)
        k_inv = kn * jnp.exp(-cum)
        scores = jnp.where(causal, jnp.einsum('bhtk,bhsk->bhts', q_dec, k_inv), 0.0)
        o = jnp.einsum('bhts,bhsv->bhtv', scores, vn) + jnp.einsum('bhtk,bhkv->bhtv', q_dec, s)
        k_end = kn * jnp.exp(cum_last - cum)
        s_new = jnp.exp(cum_last[:, :, 0, :])[..., None] * s + jnp.einsum('bhsk,bhsv->bhkv', k_end, vn)
        return s_new, o

    s_final, o = lax.scan(step, s0.astype(jnp.float32), (qc, lfc, vc))
    o = o.transpose(1, 0, 3, 2, 4).reshape(bsz, n_chunks * CHUNK_A, n_heads, -1)[:, :seq_len]
    return o, s_final


def hgrn2_branch(x, s0, w_in, lb, gnorm, w_out):
    bsz, seq_len, _ = x.shape
    proj = x @ w_in
    q, f_logit, i_in, gate = jnp.split(proj, 4, axis=-1)
    q = jax.nn.silu(q).reshape(bsz, seq_len, N_HEADS_A, HEAD_K)
    log_f = jnp.logaddexp(jnp.log(lb), jnp.log1p(-lb) + jax.nn.log_sigmoid(f_logit.astype(jnp.float32)))
    log_f = log_f.reshape(bsz, seq_len, N_HEADS_A, HEAD_K)
    v = i_in.reshape(bsz, seq_len, N_HEADS_A, HEAD_V)
    o, s_new = hgrn2_recurrence(q, log_f, v, s0)
    o = o * lax.rsqrt(jnp.mean(jnp.square(o), axis=-1, keepdims=True) + LN_EPS)
    o = o * gnorm.astype(jnp.float32).reshape(N_HEADS_A, HEAD_V)
    o = o.reshape(bsz, seq_len, E_A).astype(x.dtype) * jax.nn.silu(gate)
    return o @ w_out, s_new.astype(s0.dtype)


def chunk_gmlp_branch(x, w_in, lnv_g, lnv_b, w_s, b_s, w_out):
    bsz, seq_len, _ = x.shape
    proj = x @ w_in
    u, v = jnp.split(jax.nn.gelu(proj[..., :2 * E_B]), 2, axis=-1)
    z = proj[..., 2 * E_B:]
    v = layer_norm(v, lnv_g, lnv_b)
    pad = (-seq_len) % CHUNK_B
    n_chunks = (seq_len + pad) // CHUNK_B
    vc = jnp.pad(v, ((0, 0), (0, pad), (0, 0))).reshape(bsz, n_chunks, CHUNK_B, N_GROUPS_B, GROUP_B)
    w_causal = jnp.where(jnp.tril(jnp.ones((CHUNK_B, CHUNK_B), dtype=bool)), w_s, 0.0)
    mixed = jnp.einsum('gts,bnsgc->bntgc', w_causal, vc) + b_s.T[None, None, :, :, None]
    mixed = mixed.reshape(bsz, n_chunks * CHUNK_B, E_B)[:, :seq_len]
    out = u * mixed * jax.nn.silu(z)
    return out @ w_out, v


def setup_inputs(seed: int = 0) -> dict:
    key = jax.random.key(seed)
    ks = jax.random.split(key, 16)
    nrm = jax.random.normal
    f32 = jnp.float32
    return {
        "x_prompt": nrm(ks[0], (BATCH, SEQ, D_MODEL), f32),
        "x_sample": nrm(ks[1], (DEC_BATCH, DEC_SEQ, D_MODEL), f32),
        "state_hgrn": 0.5 * nrm(ks[2], (N_A_LAYERS, DEC_BATCH, N_HEADS_A, HEAD_K, HEAD_V), f32),
        "w_in_a": nrm(ks[3], (N_A_LAYERS, D_MODEL, 4 * E_A), f32) * D_MODEL ** -0.5,
        "lb_logits_a": 0.1 * nrm(ks[4], (N_A_LAYERS + 1, N_HEADS_A * HEAD_K), f32),
        "gnorm_a": 1.0 + 0.02 * nrm(ks[5], (N_A_LAYERS, E_A), f32),
        "w_out_a": nrm(ks[6], (N_A_LAYERS, E_A, D_MODEL), f32) * (E_A ** -0.5 * BETA),
        "w_in_b": nrm(ks[7], (N_B_LAYERS, D_MODEL, 3 * E_B), f32) * D_MODEL ** -0.5,
        "lnv_g_b": 1.0 + 0.02 * nrm(ks[8], (N_B_LAYERS, E_B), f32),
        "lnv_b_b": 0.02 * nrm(ks[9], (N_B_LAYERS, E_B), f32),
        "w_s_b": nrm(ks[10], (N_B_LAYERS, N_GROUPS_B, CHUNK_B, CHUNK_B), f32) * CHUNK_B ** -0.5,
        "b_s_b": 1.0 + 0.02 * nrm(ks[11], (N_B_LAYERS, N_GROUPS_B, CHUNK_B), f32),
        "w_out_b": nrm(ks[12], (N_B_LAYERS, E_B, D_MODEL), f32) * (E_B ** -0.5 * BETA),
        "ln_g": 1.0 + 0.02 * nrm(ks[13], (DEPTH, D_MODEL), f32),
        "ln_b": 0.02 * nrm(ks[14], (DEPTH, D_MODEL), f32),
    }


def reference(x_prompt, x_sample, state_hgrn, w_in_a, lb_logits_a, gnorm_a, w_out_a, w_in_b, lnv_g_b,
              lnv_b_b, w_s_b, b_s_b, w_out_b, ln_g, ln_b):
    lb_all = jnp.cumsum(jax.nn.softmax(lb_logits_a.astype(jnp.float32), axis=0), axis=0)
    hp, hs = x_prompt, x_sample
    hgrn_prompt, hgrn_sample, chunk_v_sample = [], [], []
    for layer in range(DEPTH):
        j = layer // 2
        if layer % 2 == 0:
            s_zero = jnp.zeros((hp.shape[0], N_HEADS_A, HEAD_K, HEAD_V), state_hgrn.dtype)
            dp, sp = hgrn2_branch(hp, s_zero, w_in_a[j], lb_all[j], gnorm_a[j], w_out_a[j])
            ds, ss = hgrn2_branch(hs, state_hgrn[j], w_in_a[j], lb_all[j], gnorm_a[j], w_out_a[j])
            hgrn_prompt.append(sp)
            hgrn_sample.append(ss)
        else:
            dp, _ = chunk_gmlp_branch(hp, w_in_b[j], lnv_g_b[j], lnv_b_b[j], w_s_b[j], b_s_b[j], w_out_b[j])
            ds, vs = chunk_gmlp_branch(hs, w_in_b[j], lnv_g_b[j], lnv_b_b[j], w_s_b[j], b_s_b[j], w_out_b[j])
            chunk_v_sample.append(vs)
        hp = layer_norm(ALPHA * hp + dp, ln_g[layer], ln_b[layer])
        hs = layer_norm(ALPHA * hs + ds, ln_g[layer], ln_b[layer])
    return (hp, hs, jnp.stack(hgrn_prompt), jnp.stack(hgrn_sample), jnp.stack(chunk_v_sample))
```

```python
import functools
import math

import jax
import jax.numpy as jnp
from jax import lax
from jax.experimental import pallas as pl
from jax.experimental.pallas import tpu as pltpu

F32 = jnp.float32
BF16 = jnp.bfloat16

D_MODEL = 1024
E = 2048
HEAD = 128
N_HEADS = E // HEAD
LANE_GROUP = 256
N_GROUPS = E // LANE_GROUP
HEADS_PER_GROUP = LANE_GROUP // HEAD
CHUNK_A = 64
CHUNK_B = 128
ROW_TILE = 256
DEC_PAD = 8
DEPTH = 2
ALPHA = (2 * DEPTH) ** 0.25
LN_EPS = 1e-5
VMEM_LIMIT_BYTES = 56 * 1024 * 1024


def _dot(a, b):
    return jnp.dot(a, b, preferred_element_type=F32)


def _dot_nt(a, b):
    return lax.dot_general(a, b, (((1,), (1,)), ((), ())), preferred_element_type=F32)


def _dot_tn(a, b):
    return lax.dot_general(a, b, (((0,), (0,)), ((), ())), preferred_element_type=F32)


def _silu(x):
    return x / (1.0 + jnp.exp(-x))


def _gelu_tanh(x):
    cdf = 0.5 * (1.0 + jnp.tanh(math.sqrt(2.0 / math.pi) * (x + 0.044715 * (x * x * x))))
    return x * cdf


def _layer_norm(x, g, b):
    mu = jnp.mean(x, axis=-1, keepdims=True)
    d = x - mu
    var = jnp.mean(d * d, axis=-1, keepdims=True)
    return d * lax.rsqrt(var + LN_EPS) * g + b


def _block_masks(n, block):
    shift = block.bit_length() - 1
    row = lax.broadcasted_iota(jnp.int32, (n, n), 0)
    col = lax.broadcasted_iota(jnp.int32, (n, n), 1)
    same = jnp.right_shift(row, shift) == jnp.right_shift(col, shift)
    causal = jnp.logical_and(same, col <= row)
    return same, causal


def _split_dot(m, x):
    hi = x.astype(BF16)
    lo = (x - hi.astype(F32)).astype(BF16)
    return _dot(m, hi) + _dot(m, lo)


def _forget_lower_bound(logits, layer):
    m = jnp.max(logits, axis=0, keepdims=True)
    e = jnp.exp(logits - m)
    den = jnp.sum(e, axis=0, keepdims=True)
    num = jnp.sum(e[: layer + 1], axis=0, keepdims=True)
    return num / den


def _hgrn_gates(xb, w_q, w_f, lb, tri, valid):
    qp = _dot(xb, w_q)
    q = _silu(qp)
    z = _dot(xb, w_f)
    c1 = 1.0 - lb
    cs = c1 / (1.0 + jnp.exp(-z))
    f = lb + cs
    k = c1 - cs
    lf = jnp.log(f)
    if valid is not None:
        lf = jnp.where(valid, lf, 0.0)
        k = jnp.where(valid, k, 0.0)
    cum = _split_dot(tri, lf)
    return q * jnp.exp(cum), k * jnp.exp(-cum), cum, lf


def _rms_gate(o, gate):
    ms = jnp.mean(o * o, axis=-1, keepdims=True)
    return o * lax.rsqrt(ms + LN_EPS) * gate


def _hgrn_prompt_kernel(x_ref, w_ref, lbl_ref, gn_ref, wo_ref, lng_ref, lnb_ref,
                        y_ref, s_ref, st_scr, og_scr, *, layer):
    j = pl.program_id(1)
    n_chunks = ROW_TILE // CHUNK_A

    @pl.when(j == 0)
    def _():
        st_scr[...] = jnp.zeros_like(st_scr)

    x = x_ref[0]
    xb = x.astype(BF16)
    _, causal = _block_masks(ROW_TILE, CHUNK_A)
    tri = jnp.where(causal, 1.0, 0.0).astype(BF16)

    def group_body(hp, carry):
        lb = _forget_lower_bound(lbl_ref[hp], layer)
        qd, ki, cum, _ = _hgrn_gates(xb, w_ref[hp], w_ref[N_GROUPS + hp], lb, tri, None)
        v = _dot(xb, w_ref[2 * N_GROUPS + hp])
        g = _dot(xb, w_ref[3 * N_GROUPS + hp])
        gate = _silu(g) * gn_ref[hp]
        qdb = qd.astype(BF16)
        kib = ki.astype(BF16)
        vb = v.astype(BF16)
        dls, kes = [], []
        for c in range(n_chunks):
            lo, hi = c * CHUNK_A, (c + 1) * CHUNK_A
            dl = jnp.exp(cum[hi - 1:hi, :])
            dls.append(dl)
            kes.append((ki[lo:hi] * dl).astype(BF16))
        ogs = []
        for i in range(HEADS_PER_GROUP):
            ls = slice(i * HEAD, (i + 1) * HEAD)
            scores = jnp.where(causal, _dot_nt(qdb[:, ls], kib[:, ls]), 0.0).astype(BF16)
            o_intra = _dot(scores, vb[:, ls])
            st = st_scr[HEADS_PER_GROUP * hp + i]
            os = []
            for c in range(n_chunks):
                lo, hi = c * CHUNK_A, (c + 1) * CHUNK_A
                os.append(o_intra[lo:hi] + _dot_nt(qdb[lo:hi, ls], st.astype(BF16)))
                st = st * dls[c][:, ls] + _dot_tn(vb[lo:hi, ls], kes[c][:, ls])
            st_scr[HEADS_PER_GROUP * hp + i] = st
            o = jnp.concatenate(os, axis=0)
            ogs.append(_rms_gate(o, gate[:, ls]))
        og_scr[hp] = jnp.concatenate(ogs, axis=1).astype(BF16)
        return carry

    lax.fori_loop(0, N_GROUPS, group_body, 0)

    y = _dot(og_scr[0], wo_ref[0])
    for hp in range(1, N_GROUPS):
        y = y + _dot(og_scr[hp], wo_ref[hp])
    y_ref[0] = _layer_norm(ALPHA * x + y, lng_ref[...], lnb_ref[...])

    @pl.when(j == pl.num_programs(1) - 1)
    def _():
        for h in range(N_HEADS):
            s_ref[0, h] = st_scr[h].T


def _resident(shape):
    nd = len(shape)
    return pl.BlockSpec(shape, lambda *_: (0,) * nd, pipeline_mode=pl.Buffered(1))


def _hgrn_prompt(x, w3, lbl3, gn3, wo3, lng, lnb, layer):
    bsz, seq, _ = x.shape
    return pl.pallas_call(
        functools.partial(_hgrn_prompt_kernel, layer=layer),
        grid=(bsz, seq // ROW_TILE),
        in_specs=[
            pl.BlockSpec((1, ROW_TILE, D_MODEL), lambda b, j: (b, j, 0)),
            _resident(w3.shape), _resident(lbl3.shape), _resident(gn3.shape), _resident(wo3.shape),
            _resident(lng.shape), _resident(lnb.shape),
        ],
        out_specs=[
            pl.BlockSpec((1, ROW_TILE, D_MODEL), lambda b, j: (b, j, 0)),
            pl.BlockSpec((1, N_HEADS, HEAD, HEAD), lambda b, j: (b, 0, 0, 0)),
        ],
        out_shape=[
            jax.ShapeDtypeStruct(x.shape, F32),
            jax.ShapeDtypeStruct((bsz, N_HEADS, HEAD, HEAD), F32),
        ],
        scratch_shapes=[
            pltpu.VMEM((N_HEADS, HEAD, HEAD), F32),
            pltpu.VMEM((N_GROUPS, ROW_TILE, LANE_GROUP), BF16),
        ],
        compiler_params=pltpu.CompilerParams(
            dimension_semantics=("arbitrary", "arbitrary"), vmem_limit_bytes=VMEM_LIMIT_BYTES),
        name="hgrn_prompt",
    )(x, w3, lbl3, gn3, wo3, lng, lnb)


def _hgrn_sample_gates_kernel(x_ref, wq_ref, wf_ref, wi_ref, wg_ref, lbl_ref, gn_ref,
                              qd_ref, ki_ref, ke_ref, v_ref, gate_ref, dl_ref, *, layer, n_valid):
    rows = x_ref.shape[0]
    xb = x_ref[...].astype(BF16)
    same, causal = _block_masks(ROW_TILE, DEC_PAD)
    tri = jnp.where(causal, 1.0, 0.0).astype(BF16)
    blk = jnp.where(same, 1.0, 0.0).astype(BF16)
    row = lax.broadcasted_iota(jnp.int32, (rows, LANE_GROUP), 0)
    valid = jnp.bitwise_and(row, DEC_PAD - 1) < n_valid
    lb = _forget_lower_bound(lbl_ref[0], layer)

    qp = _dot(xb, wq_ref[0])
    q = _silu(qp)
    z = _dot(xb, wf_ref[0])
    c1 = 1.0 - lb
    cs = c1 / (1.0 + jnp.exp(-z))
    k = jnp.where(valid, c1 - cs, 0.0)
    lf = jnp.where(valid, jnp.log(lb + cs), 0.0)
    for r in range(rows // ROW_TILE):
        rs = slice(r * ROW_TILE, (r + 1) * ROW_TILE)
        cum = _split_dot(tri, lf[rs])
        cum_last = _split_dot(blk, lf[rs])
        ki = k[rs] * jnp.exp(-cum)
        dl = jnp.exp(cum_last)
        qd_ref[0, rs] = q[rs] * jnp.exp(cum)
        ki_ref[0, rs] = ki
        ke_ref[0, rs] = ki * dl
        dl_ref[0, rs] = dl
    v_ref[0] = _dot(xb, wi_ref[0])
    gate_ref[0] = _silu(_dot(xb, wg_ref[0])) * gn_ref[0]


def _hgrn_sample_gates(x_pad, w3, lbl3, gn3, layer, n_valid):
    rows = x_pad.shape[0]
    w_spec = lambda part: pl.BlockSpec((1, D_MODEL, LANE_GROUP), lambda g: (part * N_GROUPS + g, 0, 0))
    out_spec = pl.BlockSpec((1, rows, LANE_GROUP), lambda g: (g, 0, 0))
    out_shape = jax.ShapeDtypeStruct((N_GROUPS, rows, LANE_GROUP), F32)
    return pl.pallas_call(
        functools.partial(_hgrn_sample_gates_kernel, layer=layer, n_valid=n_valid),
        grid=(N_GROUPS,),
        in_specs=[
            pl.BlockSpec((rows, D_MODEL), lambda g: (0, 0)),
            w_spec(0), w_spec(1), w_spec(2), w_spec(3),
            pl.BlockSpec((1,) + lbl3.shape[1:], lambda g: (g, 0, 0)),
            pl.BlockSpec((1, 1, LANE_GROUP), lambda g: (g, 0, 0)),
        ],
        out_specs=[out_spec] * 6,
        out_shape=[out_shape] * 6,
        compiler_params=pltpu.CompilerParams(
            dimension_semantics=("arbitrary",), vmem_limit_bytes=VMEM_LIMIT_BYTES),
        name="hgrn_sample_gates",
    )(x_pad, w3, w3, w3, w3, lbl3, gn3)


SEQS_PER_STEP = 4


def _hgrn_sample_state_kernel(qd_ref, ki_ref, ke_ref, v_ref, gate_ref, dl_ref, s_ref,
                              og_ref, so_ref):
    row = lax.broadcasted_iota(jnp.int32, (DEC_PAD, DEC_PAD), 0)
    col = lax.broadcasted_iota(jnp.int32, (DEC_PAD, DEC_PAD), 1)
    causal = col <= row

    def seq_body(b, carry):
        rs = pl.ds(pl.multiple_of(b * DEC_PAD, DEC_PAD), DEC_PAD)

        def group_body(hp, carry2):
            qd = qd_ref[hp, rs, :].astype(BF16)
            ki = ki_ref[hp, rs, :].astype(BF16)
            ke = ke_ref[hp, rs, :].astype(BF16)
            vb = v_ref[hp, rs, :].astype(BF16)
            gate = gate_ref[hp, rs, :]
            dl = dl_ref[hp, rs, :]
            ogs = []
            for i in range(HEADS_PER_GROUP):
                ls = slice(i * HEAD, (i + 1) * HEAD)
                h = HEADS_PER_GROUP * hp + i
                st = s_ref[b, h]
                scores = jnp.where(causal, _dot_nt(qd[:, ls], ki[:, ls]), 0.0).astype(BF16)
                o = _dot(scores, vb[:, ls]) + _dot(qd[:, ls], st.astype(BF16))
                decay = jnp.broadcast_to(dl[0:1, ls], (HEAD, HEAD)).T
                so_ref[b, h] = decay * st + _dot_tn(ke[:, ls], vb[:, ls])
                ogs.append(_rms_gate(o, gate[:, ls]))
            og_ref[hp, rs, :] = jnp.concatenate(ogs, axis=1)
            return carry2

        return lax.fori_loop(0, N_GROUPS, group_body, carry)

    lax.fori_loop(0, SEQS_PER_STEP, seq_body, 0)


def _hgrn_sample_state(qd, ki, ke, v, gate, dl, state):
    n_seq = state.shape[0]
    rows_per_step = SEQS_PER_STEP * DEC_PAD
    tok_spec = pl.BlockSpec((N_GROUPS, rows_per_step, LANE_GROUP), lambda s: (0, s, 0))
    st_spec = pl.BlockSpec((SEQS_PER_STEP, N_HEADS, HEAD, HEAD), lambda s: (s, 0, 0, 0))
    return pl.pallas_call(
        _hgrn_sample_state_kernel,
        grid=(n_seq // SEQS_PER_STEP,),
        in_specs=[tok_spec] * 6 + [st_spec],
        out_specs=[tok_spec, st_spec],
        out_shape=[jax.ShapeDtypeStruct(qd.shape, F32), jax.ShapeDtypeStruct(state.shape, F32)],
        compiler_params=pltpu.CompilerParams(
            dimension_semantics=("arbitrary",), vmem_limit_bytes=VMEM_LIMIT_BYTES),
        name="hgrn_sample_state",
    )(qd, ki, ke, v, gate, dl, state)


def _out_proj_ln_kernel(og_ref, x_ref, wo_ref, lng_ref, lnb_ref, y_ref):
    y = _dot(og_ref[0].astype(BF16), wo_ref[0])
    for hp in range(1, N_GROUPS):
        y = y + _dot(og_ref[hp].astype(BF16), wo_ref[hp])
    y_ref[...] = _layer_norm(ALPHA * x_ref[...] + y, lng_ref[...], lnb_ref[...])


def _out_proj_ln(og, x, wo3, lng, lnb):
    rows = x.shape[0]
    return pl.pallas_call(
        _out_proj_ln_kernel,
        grid=(rows // ROW_TILE,),
        in_specs=[
            pl.BlockSpec((N_GROUPS, ROW_TILE, LANE_GROUP), lambda r: (0, r, 0)),
            pl.BlockSpec((ROW_TILE, D_MODEL), lambda r: (r, 0)),
            _resident(wo3.shape), _resident(lng.shape), _resident(lnb.shape),
        ],
        out_specs=pl.BlockSpec((ROW_TILE, D_MODEL), lambda r: (r, 0)),
        out_shape=jax.ShapeDtypeStruct(x.shape, F32),
        compiler_params=pltpu.CompilerParams(
            dimension_semantics=("arbitrary",), vmem_limit_bytes=VMEM_LIMIT_BYTES),
        name="out_proj_ln",
    )(og, x, wo3, lng, lnb)


def _gmlp_kernel(x_ref, w_ref, vg_ref, vb_ref, ws_ref, bs_ref, wo_ref, lng_ref, lnb_ref,
                 *out_and_scratch, emit_v):
    if emit_v:
        y_ref, vn_ref, v_scr, og_scr = out_and_scratch
    else:
        y_ref, v_scr, og_scr = out_and_scratch
        vn_ref = None
    n_chunks = ROW_TILE // CHUNK_B
    x = x_ref[...]
    xb = x.astype(BF16)

    def v_body(gp, carry):
        v_scr[gp] = _gelu_tanh(_dot(xb, w_ref[N_GROUPS + gp]))
        return carry

    lax.fori_loop(0, N_GROUPS, v_body, 0)

    s1 = jnp.sum(v_scr[0], axis=-1, keepdims=True)
    for gp in range(1, N_GROUPS):
        s1 = s1 + jnp.sum(v_scr[gp], axis=-1, keepdims=True)
    mu = s1 * (1.0 / E)
    s2 = jnp.zeros_like(mu)
    for gp in range(N_GROUPS):
        d = v_scr[gp] - mu
        s2 = s2 + jnp.sum(d * d, axis=-1, keepdims=True)
    rstd = lax.rsqrt(s2 * (1.0 / E) + LN_EPS)

    row = lax.broadcasted_iota(jnp.int32, (CHUNK_B, CHUNK_B), 0)
    col = lax.broadcasted_iota(jnp.int32, (CHUNK_B, CHUNK_B), 1)
    causal = col <= row

    def mix_body(gp, carry):
        vn = (v_scr[gp] - mu) * rstd * vg_ref[gp] + vb_ref[gp]
        if vn_ref is not None:
            vn_ref[gp] = vn
        vnb = vn.astype(BF16)
        u = _gelu_tanh(_dot(xb, w_ref[gp]))
        z = _dot(xb, w_ref[2 * N_GROUPS + gp])
        cols = []
        for i in range(HEADS_PER_GROUP):
            ls = slice(i * HEAD, (i + 1) * HEAD)
            g = HEADS_PER_GROUP * gp + i
            wc = jnp.where(causal, ws_ref[g], 0.0).astype(BF16)
            bias = bs_ref[g]
            cols.append(jnp.concatenate(
                [_dot(wc, vnb[c * CHUNK_B:(c + 1) * CHUNK_B, ls]) + bias for c in range(n_chunks)], axis=0))
        mixed = jnp.concatenate(cols, axis=1)
        og_scr[gp] = (u * mixed * _silu(z)).astype(BF16)
        return carry

    lax.fori_loop(0, N_GROUPS, mix_body, 0)

    y = _dot(og_scr[0], wo_ref[0])
    for gp in range(1, N_GROUPS):
        y = y + _dot(og_scr[gp], wo_ref[gp])
    y_ref[...] = _layer_norm(ALPHA * x + y, lng_ref[...], lnb_ref[...])


def _gmlp(x, w3, vg3, vb3, ws, bs_col, wo3, lng, lnb, emit_v):
    rows = x.shape[0]
    out_specs = [pl.BlockSpec((ROW_TILE, D_MODEL), lambda r: (r, 0))]
    out_shape = [jax.ShapeDtypeStruct(x.shape, F32)]
    if emit_v:
        out_specs.append(pl.BlockSpec((N_GROUPS, ROW_TILE, LANE_GROUP), lambda r: (0, r, 0)))
        out_shape.append(jax.ShapeDtypeStruct((N_GROUPS, rows, LANE_GROUP), F32))
    return pl.pallas_call(
        functools.partial(_gmlp_kernel, emit_v=emit_v),
        grid=(rows // ROW_TILE,),
        in_specs=[
            pl.BlockSpec((ROW_TILE, D_MODEL), lambda r: (r, 0)),
            _resident(w3.shape), _resident(vg3.shape), _resident(vb3.shape), _resident(ws.shape),
            _resident(bs_col.shape), _resident(wo3.shape), _resident(lng.shape), _resident(lnb.shape),
        ],
        out_specs=out_specs,
        out_shape=out_shape,
        scratch_shapes=[
            pltpu.VMEM((N_GROUPS, ROW_TILE, LANE_GROUP), F32),
            pltpu.VMEM((N_GROUPS, ROW_TILE, LANE_GROUP), BF16),
        ],
        compiler_params=pltpu.CompilerParams(
            dimension_semantics=("arbitrary",), vmem_limit_bytes=VMEM_LIMIT_BYTES),
        name="gmlp_emit_v" if emit_v else "gmlp",
    )(x, w3, vg3, vb3, ws, bs_col, wo3, lng, lnb)


def _lane_groups_of_columns(w):
    k, n = w.shape
    return w.reshape(k, n // LANE_GROUP, LANE_GROUP).transpose(1, 0, 2).astype(BF16)


def _lane_groups_of_rows(w):
    return w.reshape(w.shape[0] // LANE_GROUP, LANE_GROUP, w.shape[1]).astype(BF16)


def _lane_groups_of_vector(v):
    return v.reshape(v.shape[0], N_GROUPS, LANE_GROUP).transpose(1, 0, 2)


def kernel(x_prompt, x_sample, state_hgrn, w_in_a, lb_logits_a, gnorm_a, w_out_a, w_in_b, lnv_g_b,
           lnv_b_b, w_s_b, b_s_b, w_out_b, ln_g, ln_b):
    bsz, seq, _ = x_prompt.shape
    n_seq, dec_seq, _ = x_sample.shape

    w3a = _lane_groups_of_columns(w_in_a[0])
    wo3a = _lane_groups_of_rows(w_out_a[0])
    lbl3 = _lane_groups_of_vector(lb_logits_a)
    gn3 = _lane_groups_of_vector(gnorm_a[0:1])
    lng0, lnb0 = ln_g[0:1], ln_b[0:1]

    hp, sp = _hgrn_prompt(x_prompt, w3a, lbl3, gn3, wo3a, lng0, lnb0, layer=0)

    xs_pad = jnp.pad(x_sample, ((0, 0), (0, DEC_PAD - dec_seq), (0, 0))).reshape(n_seq * DEC_PAD, D_MODEL)
    qd, ki, ke, v, gate, dl = _hgrn_sample_gates(xs_pad, w3a, lbl3, gn3, layer=0, n_valid=dec_seq)
    og, ss = _hgrn_sample_state(qd, ki, ke, v, gate, dl, state_hgrn[0])
    hs_pad = _out_proj_ln(og, xs_pad, wo3a, lng0, lnb0)
    hs = hs_pad.reshape(n_seq, DEC_PAD, D_MODEL)[:, :dec_seq]

    w3b = _lane_groups_of_columns(w_in_b[0])
    wo3b = _lane_groups_of_rows(w_out_b[0])
    vg3 = _lane_groups_of_vector(lnv_g_b[0:1])
    vb3 = _lane_groups_of_vector(lnv_b_b[0:1])
    lng1, lnb1 = ln_g[1:2], ln_b[1:2]
    ws = w_s_b[0]
    bs = b_s_b[0]

    yp = _gmlp(hp.reshape(bsz * seq, D_MODEL), w3b, vg3, vb3, ws, bs[:, :, None], wo3b, lng1, lnb1,
               emit_v=False)[0].reshape(bsz, seq, D_MODEL)

    reps = CHUNK_B // dec_seq
    ws_dec = jnp.einsum('ab,gts->gatbs', jnp.eye(reps, dtype=F32), ws[:, :dec_seq, :dec_seq]).reshape(
        ws.shape[0], CHUNK_B, CHUNK_B)
    bs_dec = jnp.tile(bs[:, :dec_seq], (1, reps))
    ys, vn = _gmlp(hs.reshape(n_seq * dec_seq, D_MODEL), w3b, vg3, vb3, ws_dec, bs_dec[:, :, None], wo3b,
                   lng1, lnb1, emit_v=True)
    ys = ys.reshape(n_seq, dec_seq, D_MODEL)
    vs = vn.transpose(1, 0, 2).reshape(n_seq, dec_seq, E)

    return (yp, ys, sp[None], ss[None], vs[None])
```

```python
import functools
import math

import jax
import jax.numpy as jnp
from jax import lax
from jax.experimental import pallas as pl
from jax.experimental.pallas import tpu as pltpu

F32 = jnp.float32
BF16 = jnp.bfloat16

D_MODEL = 1024
E = 2048
HEAD = 128
N_HEADS = E // HEAD
LANE_GROUP = 256
N_GROUPS = E // LANE_GROUP
HEADS_PER_GROUP = LANE_GROUP // HEAD
CHUNK_A = 64
CHUNK_B = 128
ROW_TILE = 256
DEC_PAD = 8
DEPTH = 2
ALPHA = (2 * DEPTH) ** 0.25
LN_EPS = 1e-5
VMEM_LIMIT_BYTES = 56 * 1024 * 1024


def _dot(a, b):
    return jnp.dot(a, b, preferred_element_type=F32)


def _dot_nt(a, b):
    return lax.dot_general(a, b, (((1,), (1,)), ((), ())), preferred_element_type=F32)


def _dot_tn(a, b):
    return lax.dot_general(a, b, (((0,), (0,)), ((), ())), preferred_element_type=F32)


def _silu(x):
    return x / (1.0 + jnp.exp(-x))


def _gelu_tanh(x):
    cdf = 0.5 * (1.0 + jnp.tanh(math.sqrt(2.0 / math.pi) * (x + 0.044715 * (x * x * x))))
    return x * cdf


def _layer_norm(x, g, b):
    mu = jnp.mean(x, axis=-1, keepdims=True)
    d = x - mu
    var = jnp.mean(d * d, axis=-1, keepdims=True)
    return d * lax.rsqrt(var + LN_EPS) * g + b


def _block_masks(n, block):
    shift = block.bit_length() - 1
    row = lax.broadcasted_iota(jnp.int32, (n, n), 0)
    col = lax.broadcasted_iota(jnp.int32, (n, n), 1)
    same = jnp.right_shift(row, shift) == jnp.right_shift(col, shift)
    causal = jnp.logical_and(same, col <= row)
    return same, causal


def _split_dot(m, x):
    hi = x.astype(BF16)
    lo = (x - hi.astype(F32)).astype(BF16)
    return _dot(m, hi) + _dot(m, lo)


def _forget_lower_bound(logits, layer):
    m = jnp.max(logits, axis=0, keepdims=True)
    e = jnp.exp(logits - m)
    den = jnp.sum(e, axis=0, keepdims=True)
    num = jnp.sum(e[: layer + 1], axis=0, keepdims=True)
    return num / den


def _hgrn_gates(xb, w_q, w_f, lb, tri, valid):
    qp = _dot(xb, w_q)
    q = _silu(qp)
    z = _dot(xb, w_f)
    c1 = 1.0 - lb
    cs = c1 / (1.0 + jnp.exp(-z))
    f = lb + cs
    k = c1 - cs
    lf = jnp.log(f)
    if valid is not None:
        lf = jnp.where(valid, lf, 0.0)
        k = jnp.where(valid, k, 0.0)
    cum = _split_dot(tri, lf)
    return q * jnp.exp(cum), k * jnp.exp(-cum), cum, lf


def _rms_gate(o, gate):
    ms = jnp.mean(o * o, axis=-1, keepdims=True)
    return o * lax.rsqrt(ms + LN_EPS) * gate


def _hgrn_prompt_kernel(x_ref, w_ref, lbl_ref, gn_ref, wo_ref, lng_ref, lnb_ref,
                        y_ref, s_ref, st_scr, og_scr, *, layer):
    j = pl.program_id(1)
    n_chunks = ROW_TILE // CHUNK_A

    @pl.when(j == 0)
    def _():
        st_scr[...] = jnp.zeros_like(st_scr)

    x = x_ref[0]
    xb = x.astype(BF16)
    _, causal = _block_masks(ROW_TILE, CHUNK_A)
    tri = jnp.where(causal, 1.0, 0.0).astype(BF16)

    def stage_project(hp):
        z = _dot(xb, w_ref[N_GROUPS + hp])
        qp = _dot(xb, w_ref[hp])
        v = _dot(xb, w_ref[2 * N_GROUPS + hp])
        g = _dot(xb, w_ref[3 * N_GROUPS + hp])
        lb = _forget_lower_bound(lbl_ref[hp], layer)
        c1 = 1.0 - lb
        cs = c1 / (1.0 + jnp.exp(-z))
        k = c1 - cs
        cum = _split_dot(tri, jnp.log(lb + cs))
        return _silu(qp), k, cum, v.astype(BF16), _silu(g) * gn_ref[hp]

    def stage_scores(hp, a):
        q, k, cum, vb, gate = a
        qdb = (q * jnp.exp(cum)).astype(BF16)
        ki = k * jnp.exp(-cum)
        kib = ki.astype(BF16)
        heads = []
        for i in range(HEADS_PER_GROUP):
            ls = slice(i * HEAD, (i + 1) * HEAD)
            scores = _dot_nt(qdb[:, ls], kib[:, ls])
            dls, us = [], []
            for c in range(n_chunks):
                lo, hi = c * CHUNK_A, (c + 1) * CHUNK_A
                dl = jnp.exp(cum[hi - 1:hi, ls])
                ke = (ki[lo:hi, ls] * dl).astype(BF16)
                dls.append(dl)
                us.append(_dot_tn(vb[lo:hi, ls], ke))
            heads.append((qdb[:, ls], vb[:, ls], scores, dls, us))
        return heads, gate

    def stage_output(hp, b):
        heads, gate = b
        ogs = []
        for i, (qdb, vb, scores, dls, us) in enumerate(heads):
            o_intra = _dot(jnp.where(causal, scores, 0.0).astype(BF16), vb)
            st = st_scr[HEADS_PER_GROUP * hp + i]
            os = []
            for c in range(n_chunks):
                lo, hi = c * CHUNK_A, (c + 1) * CHUNK_A
                os.append(o_intra[lo:hi] + _dot_nt(qdb[lo:hi], st.astype(BF16)))
                st = st * dls[c] + us[c]
            st_scr[HEADS_PER_GROUP * hp + i] = st
            o = jnp.concatenate(os, axis=0)
            ogs.append(_rms_gate(o, gate[:, i * HEAD:(i + 1) * HEAD]))
        og_scr[hp] = jnp.concatenate(ogs, axis=1).astype(BF16)

    projected, scored = {}, {}
    for t in range(N_GROUPS + 2):
        if t < N_GROUPS:
            projected[t] = stage_project(t)
        if 0 <= t - 1 < N_GROUPS:
            scored[t - 1] = stage_scores(t - 1, projected.pop(t - 1))
        if 0 <= t - 2 < N_GROUPS:
            stage_output(t - 2, scored.pop(t - 2))

    y = _dot(og_scr[0], wo_ref[0])
    for hp in range(1, N_GROUPS):
        y = y + _dot(og_scr[hp], wo_ref[hp])
    y_ref[0] = _layer_norm(ALPHA * x + y, lng_ref[...], lnb_ref[...])

    @pl.when(j == pl.num_programs(1) - 1)
    def _():
        for h in range(N_HEADS):
            s_ref[0, h] = st_scr[h].T


def _resident(shape):
    nd = len(shape)
    return pl.BlockSpec(shape, lambda *_: (0,) * nd, pipeline_mode=pl.Buffered(1))


def _hgrn_prompt(x, w3, lbl3, gn3, wo3, lng, lnb, layer):
    bsz, seq, _ = x.shape
    return pl.pallas_call(
        functools.partial(_hgrn_prompt_kernel, layer=layer),
        grid=(bsz, seq // ROW_TILE),
        in_specs=[
            pl.BlockSpec((1, ROW_TILE, D_MODEL), lambda b, j: (b, j, 0)),
            _resident(w3.shape), _resident(lbl3.shape), _resident(gn3.shape), _resident(wo3.shape),
            _resident(lng.shape), _resident(lnb.shape),
        ],
        out_specs=[
            pl.BlockSpec((1, ROW_TILE, D_MODEL), lambda b, j: (b, j, 0)),
            pl.BlockSpec((1, N_HEADS, HEAD, HEAD), lambda b, j: (b, 0, 0, 0)),
        ],
        out_shape=[
            jax.ShapeDtypeStruct(x.shape, F32),
            jax.ShapeDtypeStruct((bsz, N_HEADS, HEAD, HEAD), F32),
        ],
        scratch_shapes=[
            pltpu.VMEM((N_HEADS, HEAD, HEAD), F32),
            pltpu.VMEM((N_GROUPS, ROW_TILE, LANE_GROUP), BF16),
        ],
        compiler_params=pltpu.CompilerParams(
            dimension_semantics=("arbitrary", "arbitrary"), vmem_limit_bytes=VMEM_LIMIT_BYTES),
        name="hgrn_prompt",
    )(x, w3, lbl3, gn3, wo3, lng, lnb)


def _hgrn_sample_gates_kernel(x_ref, wq_ref, wf_ref, wi_ref, wg_ref, lbl_ref, gn_ref,
                              qd_ref, ki_ref, ke_ref, v_ref, gate_ref, dl_ref, *, layer, n_valid):
    rows = x_ref.shape[0]
    xb = x_ref[...].astype(BF16)
    same, causal = _block_masks(ROW_TILE, DEC_PAD)
    tri = jnp.where(causal, 1.0, 0.0).astype(BF16)
    blk = jnp.where(same, 1.0, 0.0).astype(BF16)
    row = lax.broadcasted_iota(jnp.int32, (rows, LANE_GROUP), 0)
    valid = jnp.bitwise_and(row, DEC_PAD - 1) < n_valid
    lb = _forget_lower_bound(lbl_ref[0], layer)

    qp = _dot(xb, wq_ref[0])
    q = _silu(qp)
    z = _dot(xb, wf_ref[0])
    c1 = 1.0 - lb
    cs = c1 / (1.0 + jnp.exp(-z))
    k = jnp.where(valid, c1 - cs, 0.0)
    lf = jnp.where(valid, jnp.log(lb + cs), 0.0)
    for r in range(rows // ROW_TILE):
        rs = slice(r * ROW_TILE, (r + 1) * ROW_TILE)
        cum = _split_dot(tri, lf[rs])
        cum_last = _split_dot(blk, lf[rs])
        ki = k[rs] * jnp.exp(-cum)
        dl = jnp.exp(cum_last)
        qd_ref[0, rs] = q[rs] * jnp.exp(cum)
        ki_ref[0, rs] = ki
        ke_ref[0, rs] = ki * dl
        dl_ref[0, rs] = dl
    v_ref[0] = _dot(xb, wi_ref[0])
    gate_ref[0] = _silu(_dot(xb, wg_ref[0])) * gn_ref[0]


def _hgrn_sample_gates(x_pad, w3, lbl3, gn3, layer, n_valid):
    rows = x_pad.shape[0]
    w_spec = lambda part: pl.BlockSpec((1, D_MODEL, LANE_GROUP), lambda g: (part * N_GROUPS + g, 0, 0))
    out_spec = pl.BlockSpec((1, rows, LANE_GROUP), lambda g: (g, 0, 0))
    out_shape = jax.ShapeDtypeStruct((N_GROUPS, rows, LANE_GROUP), F32)
    return pl.pallas_call(
        functools.partial(_hgrn_sample_gates_kernel, layer=layer, n_valid=n_valid),
        grid=(N_GROUPS,),
        in_specs=[
            pl.BlockSpec((rows, D_MODEL), lambda g: (0, 0)),
            w_spec(0), w_spec(1), w_spec(2), w_spec(3),
            pl.BlockSpec((1,) + lbl3.shape[1:], lambda g: (g, 0, 0)),
            pl.BlockSpec((1, 1, LANE_GROUP), lambda g: (g, 0, 0)),
        ],
        out_specs=[out_spec] * 6,
        out_shape=[out_shape] * 6,
        compiler_params=pltpu.CompilerParams(
            dimension_semantics=("arbitrary",), vmem_limit_bytes=VMEM_LIMIT_BYTES),
        name="hgrn_sample_gates",
    )(x_pad, w3, w3, w3, w3, lbl3, gn3)


SEQS_PER_STEP = 4


def _hgrn_sample_state_kernel(qd_ref, ki_ref, ke_ref, v_ref, gate_ref, dl_ref, s_ref,
                              og_ref, so_ref, *, n_valid):
    row = lax.broadcasted_iota(jnp.int32, (DEC_PAD, DEC_PAD), 0)
    col = lax.broadcasted_iota(jnp.int32, (DEC_PAD, DEC_PAD), 1)
    causal = col <= row
    trow = lax.broadcasted_iota(jnp.int32, (DEC_PAD, HEAD), 0)
    is_token = trow < n_valid
    ones = jnp.where(jnp.logical_and(trow >= n_valid, trow < n_valid + 3), 1.0, 0.0).astype(BF16)

    def seq_body(b, carry):
        rs = pl.ds(pl.multiple_of(b * DEC_PAD, DEC_PAD), DEC_PAD)
        pending = []
        for hp in range(N_GROUPS):
            qd = qd_ref[hp, rs, :].astype(BF16)
            ki = ki_ref[hp, rs, :].astype(BF16)
            ke = ke_ref[hp, rs, :]
            v = v_ref[hp, rs, :]
            dl = dl_ref[hp, rs, :]
            d_hi = dl.astype(BF16).astype(F32)
            d_mid = (dl - d_hi).astype(BF16).astype(F32)
            d_lo = dl - d_hi - d_mid
            for i in range(HEADS_PER_GROUP):
                ls = slice(i * HEAD, (i + 1) * HEAD)
                h = HEADS_PER_GROUP * hp + i
                st = s_ref[b, h]
                scores = _dot_nt(qd[:, ls], ki[:, ls])
                o_inter = _dot(qd[:, ls], st.astype(BF16))
                split = jnp.where(trow == n_valid, d_hi[:, ls],
                                  jnp.where(trow == n_valid + 1, d_mid[:, ls], d_lo[:, ls]))
                lhs = jnp.where(is_token, ke[:, ls], split).astype(BF16)
                vb = jnp.where(is_token, v[:, ls], 0.0).astype(BF16)
                upd = _dot_tn(lhs, jnp.concatenate([vb, ones], axis=1))
                so_ref[b, h] = upd[:, HEAD:] * st + upd[:, :HEAD]
                pending.append((hp, i, scores, o_inter, vb))
        ogs = {}
        for hp, i, scores, o_inter, vb in pending:
            o = _dot(jnp.where(causal, scores, 0.0).astype(BF16), vb) + o_inter
            gate = gate_ref[hp, rs, :]
            ogs[(hp, i)] = _rms_gate(o, gate[:, i * HEAD:(i + 1) * HEAD])
        for hp in range(N_GROUPS):
            og_ref[hp, rs, :] = jnp.concatenate([ogs[(hp, i)] for i in range(HEADS_PER_GROUP)], axis=1)
        return carry

    lax.fori_loop(0, SEQS_PER_STEP, seq_body, 0)


def _hgrn_sample_state(qd, ki, ke, v, gate, dl, state, n_valid):
    assert n_valid + 3 <= DEC_PAD
    n_seq = state.shape[0]
    rows_per_step = SEQS_PER_STEP * DEC_PAD
    tok_spec = pl.BlockSpec((N_GROUPS, rows_per_step, LANE_GROUP), lambda s: (0, s, 0))
    st_spec = pl.BlockSpec((SEQS_PER_STEP, N_HEADS, HEAD, HEAD), lambda s: (s, 0, 0, 0))
    return pl.pallas_call(
        functools.partial(_hgrn_sample_state_kernel, n_valid=n_valid),
        grid=(n_seq // SEQS_PER_STEP,),
        in_specs=[tok_spec] * 6 + [st_spec],
        out_specs=[tok_spec, st_spec],
        out_shape=[jax.ShapeDtypeStruct(qd.shape, F32), jax.ShapeDtypeStruct(state.shape, F32)],
        compiler_params=pltpu.CompilerParams(
            dimension_semantics=("arbitrary",), vmem_limit_bytes=VMEM_LIMIT_BYTES),
        name="hgrn_sample_state",
    )(qd, ki, ke, v, gate, dl, state)


def _out_proj_ln_kernel(og_ref, x_ref, wo_ref, lng_ref, lnb_ref, y_ref):
    y = _dot(og_ref[0].astype(BF16), wo_ref[0])
    for hp in range(1, N_GROUPS):
        y = y + _dot(og_ref[hp].astype(BF16), wo_ref[hp])
    y_ref[...] = _layer_norm(ALPHA * x_ref[...] + y, lng_ref[...], lnb_ref[...])


def _out_proj_ln(og, x, wo3, lng, lnb):
    rows = x.shape[0]
    return pl.pallas_call(
        _out_proj_ln_kernel,
        grid=(rows // ROW_TILE,),
        in_specs=[
            pl.BlockSpec((N_GROUPS, ROW_TILE, LANE_GROUP), lambda r: (0, r, 0)),
            pl.BlockSpec((ROW_TILE, D_MODEL), lambda r: (r, 0)),
            _resident(wo3.shape), _resident(lng.shape), _resident(lnb.shape),
        ],
        out_specs=pl.BlockSpec((ROW_TILE, D_MODEL), lambda r: (r, 0)),
        out_shape=jax.ShapeDtypeStruct(x.shape, F32),
        compiler_params=pltpu.CompilerParams(
            dimension_semantics=("arbitrary",), vmem_limit_bytes=VMEM_LIMIT_BYTES),
        name="out_proj_ln",
    )(og, x, wo3, lng, lnb)


def _gmlp_kernel(x_ref, w_ref, vg_ref, vb_ref, ws_ref, bs_ref, wo_ref, lng_ref, lnb_ref,
                 *out_and_scratch, emit_v):
    if emit_v:
        y_ref, vn_ref, v_scr, og_scr = out_and_scratch
    else:
        y_ref, v_scr, og_scr = out_and_scratch
        vn_ref = None
    n_chunks = ROW_TILE // CHUNK_B
    x = x_ref[...]
    xb = x.astype(BF16)

    for gp in range(N_GROUPS):
        v_scr[gp] = _gelu_tanh(_dot(xb, w_ref[N_GROUPS + gp]))

    s1 = jnp.sum(v_scr[0], axis=-1, keepdims=True)
    for gp in range(1, N_GROUPS):
        s1 = s1 + jnp.sum(v_scr[gp], axis=-1, keepdims=True)
    mu = s1 * (1.0 / E)
    s2 = jnp.zeros_like(mu)
    for gp in range(N_GROUPS):
        d = v_scr[gp] - mu
        s2 = s2 + jnp.sum(d * d, axis=-1, keepdims=True)
    rstd = lax.rsqrt(s2 * (1.0 / E) + LN_EPS)

    row = lax.broadcasted_iota(jnp.int32, (CHUNK_B, CHUNK_B), 0)
    col = lax.broadcasted_iota(jnp.int32, (CHUNK_B, CHUNK_B), 1)
    causal = col <= row

    def stage_gate(gp):
        return _dot(xb, w_ref[gp]), _dot(xb, w_ref[2 * N_GROUPS + gp])

    def stage_mix(gp, gate):
        u_pre, z = gate
        vn = (v_scr[gp] - mu) * rstd * vg_ref[gp] + vb_ref[gp]
        if vn_ref is not None:
            vn_ref[gp] = vn
        vnb = vn.astype(BF16)
        cols = []
        for i in range(HEADS_PER_GROUP):
            ls = slice(i * HEAD, (i + 1) * HEAD)
            g = HEADS_PER_GROUP * gp + i
            wc = jnp.where(causal, ws_ref[g], 0.0).astype(BF16)
            bias = bs_ref[g]
            cols.append(jnp.concatenate(
                [_dot(wc, vnb[c * CHUNK_B:(c + 1) * CHUNK_B, ls]) + bias for c in range(n_chunks)], axis=0))
        mixed = jnp.concatenate(cols, axis=1)
        og_scr[gp] = (_gelu_tanh(u_pre) * mixed * _silu(z)).astype(BF16)

    gates = {0: stage_gate(0)}
    for gp in range(N_GROUPS):
        if gp + 1 < N_GROUPS:
            gates[gp + 1] = stage_gate(gp + 1)
        stage_mix(gp, gates.pop(gp))

    y = _dot(og_scr[0], wo_ref[0])
    for gp in range(1, N_GROUPS):
        y = y + _dot(og_scr[gp], wo_ref[gp])
    y_ref[...] = _layer_norm(ALPHA * x + y, lng_ref[...], lnb_ref[...])


def _gmlp(x, w3, vg3, vb3, ws, bs_col, wo3, lng, lnb, emit_v):
    rows = x.shape[0]
    out_specs = [pl.BlockSpec((ROW_TILE, D_MODEL), lambda r: (r, 0))]
    out_shape = [jax.ShapeDtypeStruct(x.shape, F32)]
    if emit_v:
        out_specs.append(pl.BlockSpec((N_GROUPS, ROW_TILE, LANE_GROUP), lambda r: (0, r, 0)))
        out_shape.append(jax.ShapeDtypeStruct((N_GROUPS, rows, LANE_GROUP), F32))
    return pl.pallas_call(
        functools.partial(_gmlp_kernel, emit_v=emit_v),
        grid=(rows // ROW_TILE,),
        in_specs=[
            pl.BlockSpec((ROW_TILE, D_MODEL), lambda r: (r, 0)),
            _resident(w3.shape), _resident(vg3.shape), _resident(vb3.shape), _resident(ws.shape),
            _resident(bs_col.shape), _resident(wo3.shape), _resident(lng.shape), _resident(lnb.shape),
        ],
        out_specs=out_specs,
        out_shape=out_shape,
        scratch_shapes=[
            pltpu.VMEM((N_GROUPS, ROW_TILE, LANE_GROUP), F32),
            pltpu.VMEM((N_GROUPS, ROW_TILE, LANE_GROUP), BF16),
        ],
        compiler_params=pltpu.CompilerParams(
            dimension_semantics=("arbitrary",), vmem_limit_bytes=VMEM_LIMIT_BYTES),
        name="gmlp_emit_v" if emit_v else "gmlp",
    )(x, w3, vg3, vb3, ws, bs_col, wo3, lng, lnb)


def _lane_groups_of_columns(w):
    k, n = w.shape
    return w.reshape(k, n // LANE_GROUP, LANE_GROUP).transpose(1, 0, 2).astype(BF16)


def _lane_groups_of_rows(w):
    return w.reshape(w.shape[0] // LANE_GROUP, LANE_GROUP, w.shape[1]).astype(BF16)


def _lane_groups_of_vector(v):
    return v.reshape(v.shape[0], N_GROUPS, LANE_GROUP).transpose(1, 0, 2)


def kernel(x_prompt, x_sample, state_hgrn, w_in_a, lb_logits_a, gnorm_a, w_out_a, w_in_b, lnv_g_b,
           lnv_b_b, w_s_b, b_s_b, w_out_b, ln_g, ln_b):
    bsz, seq, _ = x_prompt.shape
    n_seq, dec_seq, _ = x_sample.shape

    w3a = _lane_groups_of_columns(w_in_a[0])
    wo3a = _lane_groups_of_rows(w_out_a[0])
    lbl3 = _lane_groups_of_vector(lb_logits_a)
    gn3 = _lane_groups_of_vector(gnorm_a[0:1])
    lng0, lnb0 = ln_g[0:1], ln_b[0:1]

    hp, sp = _hgrn_prompt(x_prompt, w3a, lbl3, gn3, wo3a, lng0, lnb0, layer=0)

    xs_pad = jnp.pad(x_sample, ((0, 0), (0, DEC_PAD - dec_seq), (0, 0))).reshape(n_seq * DEC_PAD, D_MODEL)
    qd, ki, ke, v, gate, dl = _hgrn_sample_gates(xs_pad, w3a, lbl3, gn3, layer=0, n_valid=dec_seq)
    og, ss = _hgrn_sample_state(qd, ki, ke, v, gate, dl, state_hgrn[0], n_valid=dec_seq)
    hs_pad = _out_proj_ln(og, xs_pad, wo3a, lng0, lnb0)
    hs = hs_pad.reshape(n_seq, DEC_PAD, D_MODEL)[:, :dec_seq]

    w3b = _lane_groups_of_columns(w_in_b[0])
    wo3b = _lane_groups_of_rows(w_out_b[0])
    vg3 = _lane_groups_of_vector(lnv_g_b[0:1])
    vb3 = _lane_groups_of_vector(lnv_b_b[0:1])
    lng1, lnb1 = ln_g[1:2], ln_b[1:2]
    ws = w_s_b[0]
    bs = b_s_b[0]

    yp = _gmlp(hp.reshape(bsz * seq, D_MODEL), w3b, vg3, vb3, ws, bs[:, :, None], wo3b, lng1, lnb1,
               emit_v=False)[0].reshape(bsz, seq, D_MODEL)

    reps = CHUNK_B // dec_seq
    ws_dec = jnp.einsum('ab,gts->gatbs', jnp.eye(reps, dtype=F32), ws[:, :dec_seq, :dec_seq]).reshape(
        ws.shape[0], CHUNK_B, CHUNK_B)
    bs_dec = jnp.tile(bs[:, :dec_seq], (1, reps))
    ys, vn = _gmlp(hs.reshape(n_seq * dec_seq, D_MODEL), w3b, vg3, vb3, ws_dec, bs_dec[:, :, None], wo3b,
                   lng1, lnb1, emit_v=True)
    ys = ys.reshape(n_seq, dec_seq, D_MODEL)
    vs = vn.transpose(1, 0, 2).reshape(n_seq, dec_seq, E)

    return (yp, ys, sp[None], ss[None], vs[None])
```

```python
import functools
import math

import jax
import jax.numpy as jnp
from jax import lax
from jax.experimental import pallas as pl
from jax.experimental.pallas import tpu as pltpu

F32 = jnp.float32
BF16 = jnp.bfloat16

D_MODEL = 1024
E = 2048
HEAD = 128
N_HEADS = E // HEAD
LANE_GROUP = 256
N_GROUPS = E // LANE_GROUP
HEADS_PER_GROUP = LANE_GROUP // HEAD
CHUNK_A = 64
CHUNK_B = 128
ROW_TILE = 256
HGRN_TILE = 512
HGRN_BLOCK = 2 * CHUNK_A
GMLP_TILE = 512
DEC_PAD = 8
DEPTH = 2
ALPHA = (2 * DEPTH) ** 0.25
LN_EPS = 1e-5
VMEM_LIMIT_BYTES = 56 * 1024 * 1024


def _dot(a, b):
    return jnp.dot(a, b, preferred_element_type=F32)


def _group_cols(w_ref, group):
    if len(w_ref.shape) == 3:
        return w_ref[group]
    return w_ref[:, group * LANE_GROUP:(group + 1) * LANE_GROUP]


def _group_major(w):
    k, n = w.shape
    return w.reshape(k, n // LANE_GROUP, LANE_GROUP).transpose(1, 0, 2).astype(BF16)


def _dot_nt(a, b):
    return lax.dot_general(a, b, (((1,), (1,)), ((), ())), preferred_element_type=F32)


def _dot_tn(a, b):
    return lax.dot_general(a, b, (((0,), (0,)), ((), ())), preferred_element_type=F32)


def _silu(x):
    return x / (1.0 + jnp.exp(-x))


def _gelu_tanh(x):
    cdf = 0.5 * (1.0 + jnp.tanh(math.sqrt(2.0 / math.pi) * (x + 0.044715 * (x * x * x))))
    return x * cdf


def _layer_norm(x, g, b):
    mu = jnp.mean(x, axis=-1, keepdims=True)
    d = x - mu
    var = jnp.mean(d * d, axis=-1, keepdims=True)
    return d * lax.rsqrt(var + LN_EPS) * g + b


def _block_masks(n, block):
    shift = block.bit_length() - 1
    row = lax.broadcasted_iota(jnp.int32, (n, n), 0)
    col = lax.broadcasted_iota(jnp.int32, (n, n), 1)
    same = jnp.right_shift(row, shift) == jnp.right_shift(col, shift)
    causal = jnp.logical_and(same, col <= row)
    return same, causal


def _split_dot(m, x):
    hi = x.astype(BF16)
    lo = (x - hi.astype(F32)).astype(BF16)
    return _dot(m, hi) + _dot(m, lo)


def _forget_lower_bound(logits, layer):
    m = jnp.max(logits, axis=0, keepdims=True)
    e = jnp.exp(logits - m)
    den = jnp.sum(e, axis=0, keepdims=True)
    num = jnp.sum(e[: layer + 1], axis=0, keepdims=True)
    return num / den


def _hgrn_gates(xb, w_q, w_f, lb, tri, valid):
    qp = _dot(xb, w_q)
    q = _silu(qp)
    z = _dot(xb, w_f)
    c1 = 1.0 - lb
    cs = c1 / (1.0 + jnp.exp(-z))
    f = lb + cs
    k = c1 - cs
    lf = jnp.log(f)
    if valid is not None:
        lf = jnp.where(valid, lf, 0.0)
        k = jnp.where(valid, k, 0.0)
    cum = _split_dot(tri, lf)
    return q * jnp.exp(cum), k * jnp.exp(-cum), cum, lf


def _rms_gate(o, gate):
    ms = jnp.mean(o * o, axis=-1, keepdims=True)
    return o * lax.rsqrt(ms + LN_EPS) * gate


def _hgrn_prompt_kernel(x_ref, w_ref, lbl_ref, gn_ref, wo_ref, lng_ref, lnb_ref,
                        y_ref, s_ref, st_scr, og_scr, *, layer):
    j = pl.program_id(1)
    tile = x_ref.shape[1]
    n_blocks = tile // HGRN_BLOCK
    chunks_per_block = HGRN_BLOCK // CHUNK_A

    @pl.when(j == 0)
    def _():
        st_scr[...] = jnp.zeros_like(st_scr)

    x = x_ref[0]
    xb = x.astype(BF16)
    _, causal = _block_masks(HGRN_BLOCK, CHUNK_A)
    tri = jnp.where(causal, 1.0, 0.0).astype(BF16)

    def block(p):
        return slice(p * HGRN_BLOCK, (p + 1) * HGRN_BLOCK)

    def stage_project(hp):
        z = _dot(xb, _group_cols(w_ref, N_GROUPS + hp))
        qp = _dot(xb, _group_cols(w_ref, hp))
        v = _dot(xb, _group_cols(w_ref, 2 * N_GROUPS + hp))
        g = _dot(xb, _group_cols(w_ref, 3 * N_GROUPS + hp))
        lb = _forget_lower_bound(lbl_ref[hp], layer)
        c1 = 1.0 - lb
        cs = c1 / (1.0 + jnp.exp(-z))
        k = c1 - cs
        lf = jnp.log(lb + cs)
        cum = jnp.concatenate([_split_dot(tri, lf[block(p)]) for p in range(n_blocks)], axis=0)
        return _silu(qp), k, cum, v.astype(BF16), _silu(g) * gn_ref[hp]

    def stage_scores(hp, a):
        q, k, cum, vb, gate = a
        qdb = (q * jnp.exp(cum)).astype(BF16)
        ki = k * jnp.exp(-cum)
        kib = ki.astype(BF16)
        n_chunks = tile // CHUNK_A
        dl_rows = jnp.exp(jnp.concatenate(
            [cum[(c + 1) * CHUNK_A - 1:(c + 1) * CHUNK_A] for c in range(n_chunks)]
            + [jnp.zeros((HEAD - n_chunks, LANE_GROUP), F32)], axis=0))
        dl_cols = dl_rows.T
        heads = []
        for i in range(HEADS_PER_GROUP):
            ls = slice(i * HEAD, (i + 1) * HEAD)
            kit = ki[:, ls].T.astype(BF16)
            scores = [_dot(qdb[block(p), ls], kit[:, block(p)]) for p in range(n_blocks)]
            dls, us = [], []
            for c in range(n_chunks):
                lo, hi = c * CHUNK_A, (c + 1) * CHUNK_A
                ke = (ki[lo:hi, ls] * dl_rows[c:c + 1, ls]).astype(BF16)
                dls.append(dl_cols[ls, c:c + 1])
                us.append(_dot_tn(ke, vb[lo:hi, ls]))
            heads.append((qdb[:, ls], vb[:, ls], scores, dls, us))
        return heads, gate

    def stage_output(hp, b):
        heads, gate = b
        ogs = []
        for i, (qdb, vb, scores, dls, us) in enumerate(heads):
            st = st_scr[HEADS_PER_GROUP * hp + i]
            os = []
            for p in range(n_blocks):
                o_intra = _dot(jnp.where(causal, scores[p], 0.0).astype(BF16), vb[block(p)])
                for cc in range(chunks_per_block):
                    c = p * chunks_per_block + cc
                    lo, hi = c * CHUNK_A, (c + 1) * CHUNK_A
                    os.append(o_intra[cc * CHUNK_A:(cc + 1) * CHUNK_A] + _dot(qdb[lo:hi], st.astype(BF16)))
                    st = st * dls[c] + us[c]
            st_scr[HEADS_PER_GROUP * hp + i] = st
            o = jnp.concatenate(os, axis=0)
            ogs.append(_rms_gate(o, gate[:, i * HEAD:(i + 1) * HEAD]))
        og_scr[hp] = jnp.concatenate(ogs, axis=1).astype(BF16)

    def out_proj(groups):
        y = _dot(og_scr[groups[0]], wo_ref[groups[0]])
        for hp in groups[1:]:
            y = y + _dot(og_scr[hp], wo_ref[hp])
        return y

    y_parts = []
    projected, scored = {}, {}
    for t in range(N_GROUPS + 2):
        if t < N_GROUPS:
            projected[t] = stage_project(t)
        if t == N_GROUPS:
            y_parts.append(out_proj(list(range(0, N_GROUPS - 4))))
        if t == N_GROUPS + 1:
            y_parts.append(out_proj([N_GROUPS - 4, N_GROUPS - 3]))
        if 0 <= t - 1 < N_GROUPS:
            scored[t - 1] = stage_scores(t - 1, projected.pop(t - 1))
        if 0 <= t - 2 < N_GROUPS:
            stage_output(t - 2, scored.pop(t - 2))
    y_parts.append(out_proj([N_GROUPS - 2, N_GROUPS - 1]))
    y = y_parts[0] + y_parts[1] + y_parts[2]
    y_ref[0] = _layer_norm(ALPHA * x + y, lng_ref[...], lnb_ref[...])

    @pl.when(j == pl.num_programs(1) - 1)
    def _():
        s_ref[0] = st_scr[...]


def _resident(shape):
    nd = len(shape)
    return pl.BlockSpec(shape, lambda *_: (0,) * nd, pipeline_mode=pl.Buffered(1))


def _hgrn_prompt(x, w3, lbl3, gn3, wo3, lng, lnb, layer):
    bsz, seq, _ = x.shape
    return pl.pallas_call(
        functools.partial(_hgrn_prompt_kernel, layer=layer),
        grid=(bsz, seq // HGRN_TILE),
        in_specs=[
            pl.BlockSpec((1, HGRN_TILE, D_MODEL), lambda b, j: (b, j, 0)),
            _resident(w3.shape), _resident(lbl3.shape), _resident(gn3.shape), _resident(wo3.shape),
            _resident(lng.shape), _resident(lnb.shape),
        ],
        out_specs=[
            pl.BlockSpec((1, HGRN_TILE, D_MODEL), lambda b, j: (b, j, 0)),
            pl.BlockSpec((1, N_HEADS, HEAD, HEAD), lambda b, j: (b, 0, 0, 0)),
        ],
        out_shape=[
            jax.ShapeDtypeStruct(x.shape, F32),
            jax.ShapeDtypeStruct((bsz, N_HEADS, HEAD, HEAD), F32),
        ],
        scratch_shapes=[
            pltpu.VMEM((N_HEADS, HEAD, HEAD), F32),
            pltpu.VMEM((N_GROUPS, HGRN_TILE, LANE_GROUP), BF16),
        ],
        compiler_params=pltpu.CompilerParams(
            dimension_semantics=("arbitrary", "arbitrary"), vmem_limit_bytes=VMEM_LIMIT_BYTES),
        name="hgrn_prompt",
    )(x, w3, lbl3, gn3, wo3, lng, lnb)


def _hgrn_sample_gates_kernel(x_ref, wq_ref, wf_ref, wi_ref, wg_ref, lbl_ref, gn_ref,
                              qd_ref, ki_ref, ke_ref, v_ref, gate_ref, dl_ref, *, layer, n_valid):
    rows = x_ref.shape[0]
    xb = x_ref[...].astype(BF16)
    same, causal = _block_masks(ROW_TILE, DEC_PAD)
    tri = jnp.where(causal, 1.0, 0.0).astype(BF16)
    blk = jnp.where(same, 1.0, 0.0).astype(BF16)
    row = lax.broadcasted_iota(jnp.int32, (rows, LANE_GROUP), 0)
    valid = jnp.bitwise_and(row, DEC_PAD - 1) < n_valid
    lb = _forget_lower_bound(lbl_ref[0], layer)

    qp = _dot(xb, wq_ref[...])
    q = _silu(qp)
    z = _dot(xb, wf_ref[...])
    c1 = 1.0 - lb
    cs = c1 / (1.0 + jnp.exp(-z))
    k = jnp.where(valid, c1 - cs, 0.0)
    lf = jnp.where(valid, jnp.log(lb + cs), 0.0)
    for r in range(rows // ROW_TILE):
        rs = slice(r * ROW_TILE, (r + 1) * ROW_TILE)
        cum = _split_dot(tri, lf[rs])
        cum_last = _split_dot(blk, lf[rs])
        ki = k[rs] * jnp.exp(-cum)
        dl = jnp.exp(cum_last)
        qd_ref[0, rs] = q[rs] * jnp.exp(cum)
        ki_ref[0, rs] = ki
        ke_ref[0, rs] = ki * dl
        dl_ref[0, rs] = dl
    v_ref[0] = _dot(xb, wi_ref[...])
    gate_ref[0] = _silu(_dot(xb, wg_ref[...])) * gn_ref[0]


def _hgrn_sample_gates(x_pad, w3, lbl3, gn3, layer, n_valid):
    rows = x_pad.shape[0]
    w_spec = lambda part: pl.BlockSpec((D_MODEL, LANE_GROUP), lambda g: (0, part * N_GROUPS + g))
    out_spec = pl.BlockSpec((1, rows, LANE_GROUP), lambda g: (g, 0, 0))
    out_shape = jax.ShapeDtypeStruct((N_GROUPS, rows, LANE_GROUP), F32)
    return pl.pallas_call(
        functools.partial(_hgrn_sample_gates_kernel, layer=layer, n_valid=n_valid),
        grid=(N_GROUPS,),
        in_specs=[
            pl.BlockSpec((rows, D_MODEL), lambda g: (0, 0)),
            w_spec(0), w_spec(1), w_spec(2), w_spec(3),
            pl.BlockSpec((1,) + lbl3.shape[1:], lambda g: (g, 0, 0)),
            pl.BlockSpec((1, 1, LANE_GROUP), lambda g: (g, 0, 0)),
        ],
        out_specs=[out_spec] * 6,
        out_shape=[out_shape] * 6,
        compiler_params=pltpu.CompilerParams(
            dimension_semantics=("arbitrary",), vmem_limit_bytes=VMEM_LIMIT_BYTES),
        name="hgrn_sample_gates",
    )(x_pad, w3, w3, w3, w3, lbl3, gn3)


SEQS_PER_STEP = 4


def _hgrn_sample_state_kernel(qd_ref, ki_ref, ke_ref, v_ref, gate_ref, dl_ref, s_ref,
                              og_ref, so_ref, *, n_valid):
    row = lax.broadcasted_iota(jnp.int32, (DEC_PAD, DEC_PAD), 0)
    col = lax.broadcasted_iota(jnp.int32, (DEC_PAD, DEC_PAD), 1)
    causal = col <= row
    trow = lax.broadcasted_iota(jnp.int32, (DEC_PAD, HEAD), 0)
    is_token = trow < n_valid
    ones = jnp.where(jnp.logical_and(trow >= n_valid, trow < n_valid + 3), 1.0, 0.0).astype(BF16)

    def seq_body(b, carry):
        rs = pl.ds(pl.multiple_of(b * DEC_PAD, DEC_PAD), DEC_PAD)
        pending = []
        for hp in range(N_GROUPS):
            qd = qd_ref[hp, rs, :].astype(BF16)
            ki = ki_ref[hp, rs, :].astype(BF16)
            ke = ke_ref[hp, rs, :]
            v = v_ref[hp, rs, :]
            dl = dl_ref[hp, rs, :]
            d_hi = dl.astype(BF16).astype(F32)
            d_mid = (dl - d_hi).astype(BF16).astype(F32)
            d_lo = dl - d_hi - d_mid
            for i in range(HEADS_PER_GROUP):
                ls = slice(i * HEAD, (i + 1) * HEAD)
                h = HEADS_PER_GROUP * hp + i
                st = s_ref[b, h]
                scores = _dot_nt(qd[:, ls], ki[:, ls])
                o_inter = _dot(qd[:, ls], st.astype(BF16))
                split = jnp.where(trow == n_valid, d_hi[:, ls],
                                  jnp.where(trow == n_valid + 1, d_mid[:, ls], d_lo[:, ls]))
                lhs = jnp.where(is_token, ke[:, ls], split).astype(BF16)
                vb = jnp.where(is_token, v[:, ls], 0.0).astype(BF16)
                upd = _dot_tn(lhs, jnp.concatenate([vb, ones], axis=1))
                so_ref[b, h] = upd[:, HEAD:] * st + upd[:, :HEAD]
                pending.append((hp, i, scores, o_inter, vb))
        ogs = {}
        for hp, i, scores, o_inter, vb in pending:
            o = _dot(jnp.where(causal, scores, 0.0).astype(BF16), vb) + o_inter
            gate = gate_ref[hp, rs, :]
            ogs[(hp, i)] = _rms_gate(o, gate[:, i * HEAD:(i + 1) * HEAD])
        for hp in range(N_GROUPS):
            og_ref[hp, rs, :] = jnp.concatenate([ogs[(hp, i)] for i in range(HEADS_PER_GROUP)], axis=1)
        return carry

    lax.fori_loop(0, SEQS_PER_STEP, seq_body, 0)


def _hgrn_sample_state(qd, ki, ke, v, gate, dl, state, n_valid):
    assert n_valid + 3 <= DEC_PAD
    n_seq = state.shape[0]
    rows_per_step = SEQS_PER_STEP * DEC_PAD
    tok_spec = pl.BlockSpec((N_GROUPS, rows_per_step, LANE_GROUP), lambda s: (0, s, 0))
    st_spec = pl.BlockSpec((SEQS_PER_STEP, N_HEADS, HEAD, HEAD), lambda s: (s, 0, 0, 0))
    return pl.pallas_call(
        functools.partial(_hgrn_sample_state_kernel, n_valid=n_valid),
        grid=(n_seq // SEQS_PER_STEP,),
        in_specs=[tok_spec] * 6 + [st_spec],
        out_specs=[tok_spec, st_spec],
        out_shape=[jax.ShapeDtypeStruct(qd.shape, F32), jax.ShapeDtypeStruct(state.shape, F32)],
        compiler_params=pltpu.CompilerParams(
            dimension_semantics=("arbitrary",), vmem_limit_bytes=VMEM_LIMIT_BYTES),
        name="hgrn_sample_state",
    )(qd, ki, ke, v, gate, dl, state)


def _out_proj_ln_kernel(og_ref, x_ref, wo_ref, lng_ref, lnb_ref, y_ref):
    y = _dot(og_ref[0].astype(BF16), wo_ref[0])
    for hp in range(1, N_GROUPS):
        y = y + _dot(og_ref[hp].astype(BF16), wo_ref[hp])
    y_ref[...] = _layer_norm(ALPHA * x_ref[...] + y, lng_ref[...], lnb_ref[...])


def _out_proj_ln(og, x, wo3, lng, lnb):
    rows = x.shape[0]
    return pl.pallas_call(
        _out_proj_ln_kernel,
        grid=(rows // ROW_TILE,),
        in_specs=[
            pl.BlockSpec((N_GROUPS, ROW_TILE, LANE_GROUP), lambda r: (0, r, 0)),
            pl.BlockSpec((ROW_TILE, D_MODEL), lambda r: (r, 0)),
            _resident(wo3.shape), _resident(lng.shape), _resident(lnb.shape),
        ],
        out_specs=pl.BlockSpec((ROW_TILE, D_MODEL), lambda r: (r, 0)),
        out_shape=jax.ShapeDtypeStruct(x.shape, F32),
        compiler_params=pltpu.CompilerParams(
            dimension_semantics=("arbitrary",), vmem_limit_bytes=VMEM_LIMIT_BYTES),
        name="out_proj_ln",
    )(og, x, wo3, lng, lnb)


def _gmlp_kernel(x_ref, w_ref, vg_ref, vb_ref, ws_ref, bs_ref, wo_ref, lng_ref, lnb_ref,
                 *out_and_scratch, emit_v):
    if emit_v:
        y_ref, vn_ref, v_scr, og_scr = out_and_scratch
    else:
        y_ref, v_scr, og_scr = out_and_scratch
        vn_ref = None
    n_chunks = x_ref.shape[0] // CHUNK_B
    x = x_ref[...]
    xb = x.astype(BF16)

    for gp in range(N_GROUPS):
        v_scr[gp] = _gelu_tanh(_dot(xb, _group_cols(w_ref, N_GROUPS + gp)))

    s1 = jnp.sum(v_scr[0], axis=-1, keepdims=True)
    for gp in range(1, N_GROUPS):
        s1 = s1 + jnp.sum(v_scr[gp], axis=-1, keepdims=True)
    mu = s1 * (1.0 / E)
    s2 = jnp.zeros_like(mu)
    for gp in range(N_GROUPS):
        d = v_scr[gp] - mu
        s2 = s2 + jnp.sum(d * d, axis=-1, keepdims=True)
    rstd = lax.rsqrt(s2 * (1.0 / E) + LN_EPS)

    row = lax.broadcasted_iota(jnp.int32, (CHUNK_B, CHUNK_B), 0)
    col = lax.broadcasted_iota(jnp.int32, (CHUNK_B, CHUNK_B), 1)
    causal = col <= row

    def stage_gate(gp):
        return _dot(xb, _group_cols(w_ref, gp)), _dot(xb, _group_cols(w_ref, 2 * N_GROUPS + gp))

    def stage_mix(gp, gate):
        u_pre, z = gate
        vn = (v_scr[gp] - mu) * rstd * vg_ref[gp] + vb_ref[gp]
        if vn_ref is not None:
            vn_ref[gp] = vn
        vnb = vn.astype(BF16)
        cols = []
        for i in range(HEADS_PER_GROUP):
            ls = slice(i * HEAD, (i + 1) * HEAD)
            g = HEADS_PER_GROUP * gp + i
            wc = jnp.where(causal, ws_ref[g], 0.0).astype(BF16)
            bias = bs_ref[g]
            cols.append(jnp.concatenate(
                [_dot(wc, vnb[c * CHUNK_B:(c + 1) * CHUNK_B, ls]) + bias for c in range(n_chunks)], axis=0))
        mixed = jnp.concatenate(cols, axis=1)
        og_scr[gp] = (_gelu_tanh(u_pre) * mixed * _silu(z)).astype(BF16)

    def out_proj(groups):
        y = _dot(og_scr[groups[0]], wo_ref[groups[0]])
        for gp in groups[1:]:
            y = y + _dot(og_scr[gp], wo_ref[gp])
        return y

    y_parts = []
    gates = {0: stage_gate(0)}
    for gp in range(N_GROUPS):
        if gp + 1 < N_GROUPS:
            gates[gp + 1] = stage_gate(gp + 1)
        stage_mix(gp, gates.pop(gp))
        if gp == N_GROUPS - 2:
            y_parts.append(out_proj(list(range(0, N_GROUPS - 3))))
    y_parts.append(out_proj([N_GROUPS - 3, N_GROUPS - 2]))
    y_parts.append(out_proj([N_GROUPS - 1]))
    y = y_parts[0] + y_parts[1] + y_parts[2]
    y_ref[...] = _layer_norm(ALPHA * x + y, lng_ref[...], lnb_ref[...])


def _gmlp(x, w3, vg3, vb3, ws, bs_col, wo3, lng, lnb, emit_v):
    rows = x.shape[0]
    tile = min(GMLP_TILE, rows)
    out_specs = [pl.BlockSpec((tile, D_MODEL), lambda r: (r, 0))]
    out_shape = [jax.ShapeDtypeStruct(x.shape, F32)]
    if emit_v:
        out_specs.append(pl.BlockSpec((N_GROUPS, tile, LANE_GROUP), lambda r: (0, r, 0)))
        out_shape.append(jax.ShapeDtypeStruct((N_GROUPS, rows, LANE_GROUP), F32))
    return pl.pallas_call(
        functools.partial(_gmlp_kernel, emit_v=emit_v),
        grid=(rows // tile,),
        in_specs=[
            pl.BlockSpec((tile, D_MODEL), lambda r: (r, 0)),
            _resident(w3.shape), _resident(vg3.shape), _resident(vb3.shape), _resident(ws.shape),
            _resident(bs_col.shape), _resident(wo3.shape), _resident(lng.shape), _resident(lnb.shape),
        ],
        out_specs=out_specs,
        out_shape=out_shape,
        scratch_shapes=[
            pltpu.VMEM((N_GROUPS, tile, LANE_GROUP), F32),
            pltpu.VMEM((N_GROUPS, tile, LANE_GROUP), BF16),
        ],
        compiler_params=pltpu.CompilerParams(
            dimension_semantics=("arbitrary",), vmem_limit_bytes=VMEM_LIMIT_BYTES),
        name="gmlp_emit_v" if emit_v else "gmlp",
    )(x, w3, vg3, vb3, ws, bs_col, wo3, lng, lnb)


def _lane_groups_of_rows(w):
    return w.reshape(w.shape[0] // LANE_GROUP, LANE_GROUP, w.shape[1]).astype(BF16)


def _lane_groups_of_vector(v):
    return v.reshape(v.shape[0], N_GROUPS, LANE_GROUP).transpose(1, 0, 2)


def kernel(x_prompt, x_sample, state_hgrn, w_in_a, lb_logits_a, gnorm_a, w_out_a, w_in_b, lnv_g_b,
           lnv_b_b, w_s_b, b_s_b, w_out_b, ln_g, ln_b):
    bsz, seq, _ = x_prompt.shape
    n_seq, dec_seq, _ = x_sample.shape

    w3a = w_in_a[0].astype(BF16)
    wo3a = _lane_groups_of_rows(w_out_a[0])
    lbl3 = _lane_groups_of_vector(lb_logits_a)
    gn3 = _lane_groups_of_vector(gnorm_a[0:1])
    lng0, lnb0 = ln_g[0:1], ln_b[0:1]

    hp, sp = _hgrn_prompt(x_prompt, w3a, lbl3, gn3, wo3a, lng0, lnb0, layer=0)

    xs_pad = jnp.pad(x_sample, ((0, 0), (0, DEC_PAD - dec_seq), (0, 0))).reshape(n_seq * DEC_PAD, D_MODEL)
    qd, ki, ke, v, gate, dl = _hgrn_sample_gates(xs_pad, w3a, lbl3, gn3, layer=0, n_valid=dec_seq)
    og, ss = _hgrn_sample_state(qd, ki, ke, v, gate, dl, state_hgrn[0], n_valid=dec_seq)
    hs_pad = _out_proj_ln(og, xs_pad, wo3a, lng0, lnb0)
    hs = hs_pad.reshape(n_seq, DEC_PAD, D_MODEL)[:, :dec_seq]

    w3b = _group_major(w_in_b[0])
    wo3b = _lane_groups_of_rows(w_out_b[0])
    vg3 = _lane_groups_of_vector(lnv_g_b[0:1])
    vb3 = _lane_groups_of_vector(lnv_b_b[0:1])
    lng1, lnb1 = ln_g[1:2], ln_b[1:2]
    ws = w_s_b[0]
    bs = b_s_b[0]

    yp = _gmlp(hp.reshape(bsz * seq, D_MODEL), w3b, vg3, vb3, ws, bs[:, :, None], wo3b, lng1, lnb1,
               emit_v=False)[0].reshape(bsz, seq, D_MODEL)

    reps = CHUNK_B // dec_seq
    ws_dec = jnp.einsum('ab,gts->gatbs', jnp.eye(reps, dtype=F32), ws[:, :dec_seq, :dec_seq]).reshape(
        ws.shape[0], CHUNK_B, CHUNK_B)
    bs_dec = jnp.tile(bs[:, :dec_seq], (1, reps))
    ys, vn = _gmlp(hs.reshape(n_seq * dec_seq, D_MODEL), w3b, vg3, vb3, ws_dec, bs_dec[:, :, None], wo3b,
                   lng1, lnb1, emit_v=True)
    ys = ys.reshape(n_seq, dec_seq, D_MODEL)
    vs = vn.transpose(1, 0, 2).reshape(n_seq, dec_seq, E)

    return (yp, ys, sp[None], ss[None], vs[None])
```

```python
import functools
import math

import jax
import jax.numpy as jnp
from jax import lax
from jax.experimental import pallas as pl
from jax.experimental.pallas import tpu as pltpu

F32 = jnp.float32
BF16 = jnp.bfloat16

D_MODEL = 1024
E = 2048
HEAD = 128
N_HEADS = E // HEAD
LANE_GROUP = 256
N_GROUPS = E // LANE_GROUP
HEADS_PER_GROUP = LANE_GROUP // HEAD
CHUNK_A = 64
CHUNK_B = 128
ROW_TILE = 256
HGRN_TILE = 512
HGRN_BLOCK = 2 * CHUNK_A
GMLP_TILE = 512
SUBLANES = 8
DEPTH = 2
ALPHA = (2 * DEPTH) ** 0.25
LN_EPS = 1e-5
VMEM_LIMIT_BYTES = 56 * 1024 * 1024


def _dot(a, b):
    return jnp.dot(a, b, preferred_element_type=F32)


def _group_cols(w_ref, group):
    if len(w_ref.shape) == 3:
        return w_ref[group]
    return w_ref[:, group * LANE_GROUP:(group + 1) * LANE_GROUP]


def _group_major(w):
    k, n = w.shape
    return w.reshape(k, n // LANE_GROUP, LANE_GROUP).transpose(1, 0, 2).astype(BF16)


def _dot_nt(a, b):
    return lax.dot_general(a, b, (((1,), (1,)), ((), ())), preferred_element_type=F32)


def _dot_tn(a, b):
    return lax.dot_general(a, b, (((0,), (0,)), ((), ())), preferred_element_type=F32)


def _silu(x):
    return x / (1.0 + jnp.exp(-x))


def _gelu_tanh(x):
    cdf = 0.5 * (1.0 + jnp.tanh(math.sqrt(2.0 / math.pi) * (x + 0.044715 * (x * x * x))))
    return x * cdf


def _layer_norm(x, g, b):
    mu = jnp.mean(x, axis=-1, keepdims=True)
    d = x - mu
    var = jnp.mean(d * d, axis=-1, keepdims=True)
    return d * lax.rsqrt(var + LN_EPS) * g + b


def _block_masks(n, block):
    shift = block.bit_length() - 1
    row = lax.broadcasted_iota(jnp.int32, (n, n), 0)
    col = lax.broadcasted_iota(jnp.int32, (n, n), 1)
    same = jnp.right_shift(row, shift) == jnp.right_shift(col, shift)
    causal = jnp.logical_and(same, col <= row)
    return same, causal


def _split_dot(m, x):
    hi = x.astype(BF16)
    lo = (x - hi.astype(F32)).astype(BF16)
    return _dot(jnp.concatenate([m, m], axis=1), jnp.concatenate([hi, lo], axis=0))


def _forget_lower_bound(logits, layer):
    m = jnp.max(logits, axis=0, keepdims=True)
    e = jnp.exp(logits - m)
    den = jnp.sum(e, axis=0, keepdims=True)
    num = jnp.sum(e[: layer + 1], axis=0, keepdims=True)
    return num / den


def _hgrn_gates(xb, w_q, w_f, lb, tri, valid):
    qp = _dot(xb, w_q)
    q = _silu(qp)
    z = _dot(xb, w_f)
    c1 = 1.0 - lb
    cs = c1 / (1.0 + jnp.exp(-z))
    f = lb + cs
    k = c1 - cs
    lf = jnp.log(f)
    if valid is not None:
        lf = jnp.where(valid, lf, 0.0)
        k = jnp.where(valid, k, 0.0)
    cum = _split_dot(tri, lf)
    return q * jnp.exp(cum), k * jnp.exp(-cum), cum, lf


def _rms_gate(o, gate):
    ms = jnp.mean(o * o, axis=-1, keepdims=True)
    return o * lax.rsqrt(ms + LN_EPS) * gate


def _hgrn_prompt_kernel(x_ref, w_ref, lbl_ref, gn_ref, wo_ref, lng_ref, lnb_ref,
                        y_ref, s_ref, st_scr, og_scr, *, layer):
    j = pl.program_id(1)
    tile = x_ref.shape[1]
    n_blocks = tile // HGRN_BLOCK
    chunks_per_block = HGRN_BLOCK // CHUNK_A

    @pl.when(j == 0)
    def _():
        st_scr[...] = jnp.zeros_like(st_scr)

    x = x_ref[0]
    xb = x.astype(BF16)
    _, causal = _block_masks(HGRN_BLOCK, CHUNK_A)
    tri = jnp.where(causal, 1.0, 0.0).astype(BF16)

    def block(p):
        return slice(p * HGRN_BLOCK, (p + 1) * HGRN_BLOCK)

    def stage_project(hp):
        z = _dot(xb, _group_cols(w_ref, N_GROUPS + hp))
        qp = _dot(xb, _group_cols(w_ref, hp))
        v = _dot(xb, _group_cols(w_ref, 2 * N_GROUPS + hp))
        g = _dot(xb, _group_cols(w_ref, 3 * N_GROUPS + hp))
        lb = _forget_lower_bound(lbl_ref[hp], layer)
        c1 = 1.0 - lb
        cs = c1 / (1.0 + jnp.exp(-z))
        k = c1 - cs
        lf = jnp.log(lb + cs)
        cum = jnp.concatenate([_split_dot(tri, lf[block(p)]) for p in range(n_blocks)], axis=0)
        return _silu(qp), k, cum, v.astype(BF16), _silu(g) * gn_ref[hp]

    def stage_scores(hp, a):
        q, k, cum, vb, gate = a
        qdb = (q * jnp.exp(cum)).astype(BF16)
        ki = k * jnp.exp(-cum)
        kib = ki.astype(BF16)
        n_chunks = tile // CHUNK_A
        dl_rows = jnp.exp(jnp.concatenate(
            [cum[(c + 1) * CHUNK_A - 1:(c + 1) * CHUNK_A] for c in range(n_chunks)]
            + [jnp.zeros((HEAD - n_chunks, LANE_GROUP), F32)], axis=0))
        dl_cols = dl_rows.T
        heads = []
        for i in range(HEADS_PER_GROUP):
            ls = slice(i * HEAD, (i + 1) * HEAD)
            kit = ki[:, ls].T.astype(BF16)
            scores = [_dot(qdb[block(p), ls], kit[:, block(p)]) for p in range(n_blocks)]
            dls, us = [], []
            for c in range(n_chunks):
                lo, hi = c * CHUNK_A, (c + 1) * CHUNK_A
                ke = (ki[lo:hi, ls] * dl_rows[c:c + 1, ls]).astype(BF16)
                dls.append(dl_cols[ls, c:c + 1])
                us.append(_dot_tn(ke, vb[lo:hi, ls]))
            heads.append((qdb[:, ls], vb[:, ls], scores, dls, us))
        return heads, gate

    def stage_output(hp, b):
        heads, gate = b
        ogs = []
        for i, (qdb, vb, scores, dls, us) in enumerate(heads):
            st = st_scr[HEADS_PER_GROUP * hp + i]
            os = []
            for p in range(n_blocks):
                o_intra = _dot(jnp.where(causal, scores[p], 0.0).astype(BF16), vb[block(p)])
                for cc in range(chunks_per_block):
                    c = p * chunks_per_block + cc
                    lo, hi = c * CHUNK_A, (c + 1) * CHUNK_A
                    os.append(o_intra[cc * CHUNK_A:(cc + 1) * CHUNK_A] + _dot(qdb[lo:hi], st.astype(BF16)))
                    st = st * dls[c] + us[c]
            st_scr[HEADS_PER_GROUP * hp + i] = st
            o = jnp.concatenate(os, axis=0)
            ogs.append(_rms_gate(o, gate[:, i * HEAD:(i + 1) * HEAD]))
        og_scr[hp] = jnp.concatenate(ogs, axis=1).astype(BF16)

    def out_proj(groups):
        y = _dot(og_scr[groups[0]], wo_ref[groups[0]])
        for hp in groups[1:]:
            y = y + _dot(og_scr[hp], wo_ref[hp])
        return y

    y_parts = []
    projected, scored = {}, {}
    for t in range(N_GROUPS + 2):
        if t < N_GROUPS:
            projected[t] = stage_project(t)
        if t == N_GROUPS:
            y_parts.append(out_proj(list(range(0, N_GROUPS - 4))))
        if t == N_GROUPS + 1:
            y_parts.append(out_proj([N_GROUPS - 4, N_GROUPS - 3]))
        if 0 <= t - 1 < N_GROUPS:
            scored[t - 1] = stage_scores(t - 1, projected.pop(t - 1))
        if 0 <= t - 2 < N_GROUPS:
            stage_output(t - 2, scored.pop(t - 2))
    y_parts.append(out_proj([N_GROUPS - 2, N_GROUPS - 1]))
    y = y_parts[0] + y_parts[1] + y_parts[2]
    y_ref[0] = _layer_norm(ALPHA * x + y, lng_ref[...], lnb_ref[...])

    @pl.when(j == pl.num_programs(1) - 1)
    def _():
        s_ref[0] = st_scr[...]


def _resident(shape):
    nd = len(shape)
    return pl.BlockSpec(shape, lambda *_: (0,) * nd, pipeline_mode=pl.Buffered(1))


def _hgrn_prompt(x, w3, lbl3, gn3, wo3, lng, lnb, layer):
    bsz, seq, _ = x.shape
    return pl.pallas_call(
        functools.partial(_hgrn_prompt_kernel, layer=layer),
        grid=(bsz, seq // HGRN_TILE),
        in_specs=[
            pl.BlockSpec((1, HGRN_TILE, D_MODEL), lambda b, j: (b, j, 0)),
            _resident(w3.shape), _resident(lbl3.shape), _resident(gn3.shape), _resident(wo3.shape),
            _resident(lng.shape), _resident(lnb.shape),
        ],
        out_specs=[
            pl.BlockSpec((1, HGRN_TILE, D_MODEL), lambda b, j: (b, j, 0)),
            pl.BlockSpec((1, N_HEADS, HEAD, HEAD), lambda b, j: (b, 0, 0, 0)),
        ],
        out_shape=[
            jax.ShapeDtypeStruct(x.shape, F32),
            jax.ShapeDtypeStruct((bsz, N_HEADS, HEAD, HEAD), F32),
        ],
        scratch_shapes=[
            pltpu.VMEM((N_HEADS, HEAD, HEAD), F32),
            pltpu.VMEM((N_GROUPS, HGRN_TILE, LANE_GROUP), BF16),
        ],
        compiler_params=pltpu.CompilerParams(
            dimension_semantics=("arbitrary", "arbitrary"), vmem_limit_bytes=VMEM_LIMIT_BYTES),
        name="hgrn_prompt",
    )(x, w3, lbl3, gn3, wo3, lng, lnb)


def _hgrn_sample_gates_kernel(x_ref, wq_ref, wf_ref, wi_ref, wg_ref, lbl_ref, gn_ref,
                              qd_ref, ki_ref, ke_ref, v_ref, gate_ref, dl_ref, *, layer, dec_seq):
    rows = x_ref.shape[0]
    xb = x_ref[...].astype(BF16)
    same, causal = _block_masks(ROW_TILE, dec_seq)
    tri = jnp.where(causal, 1.0, 0.0).astype(BF16)
    blk = jnp.where(same, 1.0, 0.0).astype(BF16)
    lb = _forget_lower_bound(lbl_ref[0], layer)

    qp = _dot(xb, wq_ref[...])
    q = _silu(qp)
    z = _dot(xb, wf_ref[...])
    c1 = 1.0 - lb
    cs = c1 / (1.0 + jnp.exp(-z))
    k = c1 - cs
    lf = jnp.log(lb + cs)
    for r in range(rows // ROW_TILE):
        rs = slice(r * ROW_TILE, (r + 1) * ROW_TILE)
        cum = _split_dot(tri, lf[rs])
        cum_last = _split_dot(blk, lf[rs])
        ki = k[rs] * jnp.exp(-cum)
        dl = jnp.exp(cum_last)
        qd_ref[0, rs] = q[rs] * jnp.exp(cum)
        ki_ref[0, rs] = ki
        ke_ref[0, rs] = ki * dl
        dl_ref[0, rs] = dl
    v_ref[0] = _dot(xb, wi_ref[...])
    gate_ref[0] = _silu(_dot(xb, wg_ref[...])) * gn_ref[0]


def _hgrn_sample_gates(x_rows, w3, lbl3, gn3, layer, dec_seq):
    rows = x_rows.shape[0]
    assert rows % ROW_TILE == 0 and SUBLANES % dec_seq == 0
    w_spec = lambda part: pl.BlockSpec((D_MODEL, LANE_GROUP), lambda g: (0, part * N_GROUPS + g))
    out_spec = pl.BlockSpec((1, rows, LANE_GROUP), lambda g: (g, 0, 0))
    out_shape = jax.ShapeDtypeStruct((N_GROUPS, rows, LANE_GROUP), F32)
    return pl.pallas_call(
        functools.partial(_hgrn_sample_gates_kernel, layer=layer, dec_seq=dec_seq),
        grid=(N_GROUPS,),
        in_specs=[
            pl.BlockSpec((rows, D_MODEL), lambda g: (0, 0)),
            w_spec(0), w_spec(1), w_spec(2), w_spec(3),
            pl.BlockSpec((1,) + lbl3.shape[1:], lambda g: (g, 0, 0)),
            pl.BlockSpec((1, 1, LANE_GROUP), lambda g: (g, 0, 0)),
        ],
        out_specs=[out_spec] * 6,
        out_shape=[out_shape] * 6,
        compiler_params=pltpu.CompilerParams(
            dimension_semantics=("arbitrary",), vmem_limit_bytes=VMEM_LIMIT_BYTES),
        name="hgrn_sample_gates",
    )(x_rows, w3, w3, w3, w3, lbl3, gn3)


SEQS_PER_STEP = 4


def _hgrn_sample_state_kernel(qd_ref, ki_ref, ke_ref, v_ref, gate_ref, dl_ref, s_ref,
                              og_ref, so_ref, *, dec_seq):
    seqs_per_tile = SUBLANES // dec_seq
    _, causal = _block_masks(SUBLANES, dec_seq)
    shift = dec_seq.bit_length() - 1
    trow = jnp.right_shift(lax.broadcasted_iota(jnp.int32, (SUBLANES, HEAD), 0), shift)
    in_seq = [trow == s for s in range(seqs_per_tile)]
    pad_rows = jnp.zeros((HEAD - N_HEADS, HEAD), F32)

    def tile_body(m, carry):
        rs = pl.ds(pl.multiple_of(m * SUBLANES, SUBLANES), SUBLANES)
        dl = [dl_ref[hp, rs, :] for hp in range(N_GROUPS)]
        dcols = []
        for s in range(seqs_per_tile):
            r = s * dec_seq
            rows = [dl[h // HEADS_PER_GROUP][r:r + 1, (h % HEADS_PER_GROUP) * HEAD:(h % HEADS_PER_GROUP + 1) * HEAD]
                    for h in range(N_HEADS)]
            dcols.append(jnp.concatenate(rows + [pad_rows], axis=0).T)
        pending = []
        for hp in range(N_GROUPS):
            qd = qd_ref[hp, rs, :].astype(BF16)
            ki = ki_ref[hp, rs, :].astype(BF16)
            ke = ke_ref[hp, rs, :]
            vb = v_ref[hp, rs, :].astype(BF16)
            for i in range(HEADS_PER_GROUP):
                ls = slice(i * HEAD, (i + 1) * HEAD)
                h = HEADS_PER_GROUP * hp + i
                scores = _dot_nt(qd[:, ls], ki[:, ls])
                o_inter = None
                for s in range(seqs_per_tile):
                    b = m * seqs_per_tile + s
                    st = s_ref[b, h]
                    o_s = _dot(qd[:, ls], st.astype(BF16))
                    o_inter = o_s if s == 0 else jnp.where(in_seq[s], o_s, o_inter)
                    u = _dot_tn(jnp.where(in_seq[s], ke[:, ls], 0.0).astype(BF16), vb[:, ls])
                    so_ref[b, h] = dcols[s][:, h:h + 1] * st + u
                pending.append((hp, i, scores, o_inter, vb[:, ls]))
        ogs = {}
        for hp, i, scores, o_inter, vbh in pending:
            o = _dot(jnp.where(causal, scores, 0.0).astype(BF16), vbh) + o_inter
            gate = gate_ref[hp, rs, :]
            ogs[(hp, i)] = _rms_gate(o, gate[:, i * HEAD:(i + 1) * HEAD])
        for hp in range(N_GROUPS):
            og_ref[hp, rs, :] = jnp.concatenate([ogs[(hp, i)] for i in range(HEADS_PER_GROUP)], axis=1)
        return carry

    lax.fori_loop(0, SEQS_PER_STEP // seqs_per_tile, tile_body, 0)


def _hgrn_sample_state(qd, ki, ke, v, gate, dl, state, dec_seq):
    n_seq = state.shape[0]
    rows_per_step = SEQS_PER_STEP * dec_seq
    assert rows_per_step % SUBLANES == 0
    tok_spec = pl.BlockSpec((N_GROUPS, rows_per_step, LANE_GROUP), lambda s: (0, s, 0))
    st_spec = pl.BlockSpec((SEQS_PER_STEP, N_HEADS, HEAD, HEAD), lambda s: (s, 0, 0, 0))
    return pl.pallas_call(
        functools.partial(_hgrn_sample_state_kernel, dec_seq=dec_seq),
        grid=(n_seq // SEQS_PER_STEP,),
        in_specs=[tok_spec] * 6 + [st_spec],
        out_specs=[tok_spec, st_spec],
        out_shape=[jax.ShapeDtypeStruct(qd.shape, F32), jax.ShapeDtypeStruct(state.shape, F32)],
        compiler_params=pltpu.CompilerParams(
            dimension_semantics=("arbitrary",), vmem_limit_bytes=VMEM_LIMIT_BYTES),
        name="hgrn_sample_state",
    )(qd, ki, ke, v, gate, dl, state)


def _out_proj_ln_kernel(og_ref, x_ref, wo_ref, lng_ref, lnb_ref, y_ref):
    y = _dot(og_ref[0].astype(BF16), wo_ref[0])
    for hp in range(1, N_GROUPS):
        y = y + _dot(og_ref[hp].astype(BF16), wo_ref[hp])
    y_ref[...] = _layer_norm(ALPHA * x_ref[...] + y, lng_ref[...], lnb_ref[...])


def _out_proj_ln(og, x, wo3, lng, lnb):
    rows = x.shape[0]
    return pl.pallas_call(
        _out_proj_ln_kernel,
        grid=(rows // ROW_TILE,),
        in_specs=[
            pl.BlockSpec((N_GROUPS, ROW_TILE, LANE_GROUP), lambda r: (0, r, 0)),
            pl.BlockSpec((ROW_TILE, D_MODEL), lambda r: (r, 0)),
            _resident(wo3.shape), _resident(lng.shape), _resident(lnb.shape),
        ],
        out_specs=pl.BlockSpec((ROW_TILE, D_MODEL), lambda r: (r, 0)),
        out_shape=jax.ShapeDtypeStruct(x.shape, F32),
        compiler_params=pltpu.CompilerParams(
            dimension_semantics=("arbitrary",), vmem_limit_bytes=VMEM_LIMIT_BYTES),
        name="out_proj_ln",
    )(og, x, wo3, lng, lnb)


def _gmlp_kernel(x_ref, w_ref, vg_ref, vb_ref, ws_ref, bs_ref, wo_ref, lng_ref, lnb_ref,
                 *out_and_scratch, emit_v):
    if emit_v:
        y_ref, vn_ref, v_scr, og_scr = out_and_scratch
    else:
        y_ref, v_scr, og_scr = out_and_scratch
        vn_ref = None
    n_chunks = x_ref.shape[0] // CHUNK_B
    x = x_ref[...]
    xb = x.astype(BF16)

    for gp in range(N_GROUPS):
        v_scr[gp] = _gelu_tanh(_dot(xb, _group_cols(w_ref, N_GROUPS + gp)))

    s1 = jnp.sum(v_scr[0], axis=-1, keepdims=True)
    for gp in range(1, N_GROUPS):
        s1 = s1 + jnp.sum(v_scr[gp], axis=-1, keepdims=True)
    mu = s1 * (1.0 / E)
    s2 = jnp.zeros_like(mu)
    for gp in range(N_GROUPS):
        d = v_scr[gp] - mu
        s2 = s2 + jnp.sum(d * d, axis=-1, keepdims=True)
    rstd = lax.rsqrt(s2 * (1.0 / E) + LN_EPS)

    row = lax.broadcasted_iota(jnp.int32, (CHUNK_B, CHUNK_B), 0)
    col = lax.broadcasted_iota(jnp.int32, (CHUNK_B, CHUNK_B), 1)
    causal = col <= row

    def stage_gate(gp):
        return _dot(xb, _group_cols(w_ref, gp)), _dot(xb, _group_cols(w_ref, 2 * N_GROUPS + gp))

    def stage_mix(gp, gate):
        u_pre, z = gate
        vn = (v_scr[gp] - mu) * rstd * vg_ref[gp] + vb_ref[gp]
        if vn_ref is not None:
            vn_ref[gp] = vn
        vnb = vn.astype(BF16)
        cols = []
        for i in range(HEADS_PER_GROUP):
            ls = slice(i * HEAD, (i + 1) * HEAD)
            g = HEADS_PER_GROUP * gp + i
            wc = jnp.where(causal, ws_ref[g], 0.0).astype(BF16)
            bias = bs_ref[g]
            cols.append(jnp.concatenate(
                [_dot(wc, vnb[c * CHUNK_B:(c + 1) * CHUNK_B, ls]) + bias for c in range(n_chunks)], axis=0))
        mixed = jnp.concatenate(cols, axis=1)
        og_scr[gp] = (_gelu_tanh(u_pre) * mixed * _silu(z)).astype(BF16)

    def out_proj(groups):
        y = _dot(og_scr[groups[0]], wo_ref[groups[0]])
        for gp in groups[1:]:
            y = y + _dot(og_scr[gp], wo_ref[gp])
        return y

    y_parts = []
    gates = {0: stage_gate(0)}
    for gp in range(N_GROUPS):
        if gp + 1 < N_GROUPS:
            gates[gp + 1] = stage_gate(gp + 1)
        stage_mix(gp, gates.pop(gp))
        if gp == N_GROUPS - 2:
            y_parts.append(out_proj(list(range(0, N_GROUPS - 3))))
    y_parts.append(out_proj([N_GROUPS - 3, N_GROUPS - 2]))
    y_parts.append(out_proj([N_GROUPS - 1]))
    y = y_parts[0] + y_parts[1] + y_parts[2]
    y_ref[...] = _layer_norm(ALPHA * x + y, lng_ref[...], lnb_ref[...])


def _gmlp(x, w3, vg3, vb3, ws, bs_col, wo3, lng, lnb, emit_v):
    rows = x.shape[0]
    tile = min(GMLP_TILE, rows)
    out_specs = [pl.BlockSpec((tile, D_MODEL), lambda r: (r, 0))]
    out_shape = [jax.ShapeDtypeStruct(x.shape, F32)]
    if emit_v:
        out_specs.append(pl.BlockSpec((N_GROUPS, tile, LANE_GROUP), lambda r: (0, r, 0)))
        out_shape.append(jax.ShapeDtypeStruct((N_GROUPS, rows, LANE_GROUP), F32))
    return pl.pallas_call(
        functools.partial(_gmlp_kernel, emit_v=emit_v),
        grid=(rows // tile,),
        in_specs=[
            pl.BlockSpec((tile, D_MODEL), lambda r: (r, 0)),
            _resident(w3.shape), _resident(vg3.shape), _resident(vb3.shape), _resident(ws.shape),
            _resident(bs_col.shape), _resident(wo3.shape), _resident(lng.shape), _resident(lnb.shape),
        ],
        out_specs=out_specs,
        out_shape=out_shape,
        scratch_shapes=[
            pltpu.VMEM((N_GROUPS, tile, LANE_GROUP), F32),
            pltpu.VMEM((N_GROUPS, tile, LANE_GROUP), BF16),
        ],
        compiler_params=pltpu.CompilerParams(
            dimension_semantics=("arbitrary",), vmem_limit_bytes=VMEM_LIMIT_BYTES),
        name="gmlp_emit_v" if emit_v else "gmlp",
    )(x, w3, vg3, vb3, ws, bs_col, wo3, lng, lnb)


def _lane_groups_of_rows(w):
    return w.reshape(w.shape[0] // LANE_GROUP, LANE_GROUP, w.shape[1]).astype(BF16)


def _lane_groups_of_vector(v):
    return v.reshape(v.shape[0], N_GROUPS, LANE_GROUP).transpose(1, 0, 2)


def kernel(x_prompt, x_sample, state_hgrn, w_in_a, lb_logits_a, gnorm_a, w_out_a, w_in_b, lnv_g_b,
           lnv_b_b, w_s_b, b_s_b, w_out_b, ln_g, ln_b):
    bsz, seq, _ = x_prompt.shape
    n_seq, dec_seq, _ = x_sample.shape

    w3a = w_in_a[0].astype(BF16)
    wo3a = _lane_groups_of_rows(w_out_a[0])
    lbl3 = _lane_groups_of_vector(lb_logits_a)
    gn3 = _lane_groups_of_vector(gnorm_a[0:1])
    lng0, lnb0 = ln_g[0:1], ln_b[0:1]

    hp, sp = _hgrn_prompt(x_prompt, w3a, lbl3, gn3, wo3a, lng0, lnb0, layer=0)

    xs = x_sample.reshape(n_seq * dec_seq, D_MODEL)
    qd, ki, ke, v, gate, dl = _hgrn_sample_gates(xs, w3a, lbl3, gn3, layer=0, dec_seq=dec_seq)
    og, ss = _hgrn_sample_state(qd, ki, ke, v, gate, dl, state_hgrn[0], dec_seq=dec_seq)
    hs = _out_proj_ln(og, xs, wo3a, lng0, lnb0)

    w3b = _group_major(w_in_b[0])
    wo3b = _lane_groups_of_rows(w_out_b[0])
    vg3 = _lane_groups_of_vector(lnv_g_b[0:1])
    vb3 = _lane_groups_of_vector(lnv_b_b[0:1])
    lng1, lnb1 = ln_g[1:2], ln_b[1:2]
    ws = w_s_b[0]
    bs = b_s_b[0]

    yp = _gmlp(hp.reshape(bsz * seq, D_MODEL), w3b, vg3, vb3, ws, bs[:, :, None], wo3b, lng1, lnb1,
               emit_v=False)[0].reshape(bsz, seq, D_MODEL)

    reps = CHUNK_B // dec_seq
    ws_dec = jnp.einsum('ab,gts->gatbs', jnp.eye(reps, dtype=F32), ws[:, :dec_seq, :dec_seq]).reshape(
        ws.shape[0], CHUNK_B, CHUNK_B)
    bs_dec = jnp.tile(bs[:, :dec_seq], (1, reps))
    ys, vn = _gmlp(hs.reshape(n_seq * dec_seq, D_MODEL), w3b, vg3, vb3, ws_dec, bs_dec[:, :, None], wo3b,
                   lng1, lnb1, emit_v=True)
    ys = ys.reshape(n_seq, dec_seq, D_MODEL)
    vs = vn.transpose(1, 0, 2).reshape(n_seq, dec_seq, E)

    return (yp, ys, sp[None], ss[None], vs[None])
```

```python
import functools
import math

import jax
import jax.numpy as jnp
from jax import lax
from jax.experimental import pallas as pl
from jax.experimental.pallas import tpu as pltpu

F32 = jnp.float32
BF16 = jnp.bfloat16

D_MODEL = 1024
E = 2048
HEAD = 128
N_HEADS = E // HEAD
LANE_GROUP = 256
N_GROUPS = E // LANE_GROUP
HEADS_PER_GROUP = LANE_GROUP // HEAD
CHUNK_A = 64
CHUNK_B = 128
ROW_TILE = 256
HGRN_TILE = 512
HGRN_BLOCK = 2 * CHUNK_A
GMLP_TILE = 512
SUBLANES = 8
DEPTH = 2
ALPHA = (2 * DEPTH) ** 0.25
LN_EPS = 1e-5
VMEM_LIMIT_BYTES = 56 * 1024 * 1024


def _dot(a, b):
    return jnp.dot(a, b, preferred_element_type=F32)


def _group_cols(w_ref, group):
    if len(w_ref.shape) == 3:
        return w_ref[group]
    return w_ref[:, group * LANE_GROUP:(group + 1) * LANE_GROUP]


def _group_major(w):
    k, n = w.shape
    return w.reshape(k, n // LANE_GROUP, LANE_GROUP).transpose(1, 0, 2).astype(BF16)


def _dot_nt(a, b):
    return lax.dot_general(a, b, (((1,), (1,)), ((), ())), preferred_element_type=F32)


def _dot_tn(a, b):
    return lax.dot_general(a, b, (((0,), (0,)), ((), ())), preferred_element_type=F32)


def _silu(x):
    return x / (1.0 + jnp.exp(-x))


def _gelu_tanh(x):
    cdf = 0.5 * (1.0 + jnp.tanh(math.sqrt(2.0 / math.pi) * (x + 0.044715 * (x * x * x))))
    return x * cdf


def _layer_norm(x, g, b):
    mu = jnp.mean(x, axis=-1, keepdims=True)
    d = x - mu
    var = jnp.mean(d * d, axis=-1, keepdims=True)
    return d * lax.rsqrt(var + LN_EPS) * g + b


def _block_masks(n, block):
    shift = block.bit_length() - 1
    row = lax.broadcasted_iota(jnp.int32, (n, n), 0)
    col = lax.broadcasted_iota(jnp.int32, (n, n), 1)
    same = jnp.right_shift(row, shift) == jnp.right_shift(col, shift)
    causal = jnp.logical_and(same, col <= row)
    return same, causal


def _split_dot(m, x):
    hi = x.astype(BF16)
    lo = (x - hi.astype(F32)).astype(BF16)
    return _dot(jnp.concatenate([m, m], axis=1), jnp.concatenate([hi, lo], axis=0))


def _forget_lower_bound(logits, layer):
    m = jnp.max(logits, axis=0, keepdims=True)
    e = jnp.exp(logits - m)
    den = jnp.sum(e, axis=0, keepdims=True)
    num = jnp.sum(e[: layer + 1], axis=0, keepdims=True)
    return num / den


def _hgrn_gates(xb, w_q, w_f, lb, tri, valid):
    qp = _dot(xb, w_q)
    q = _silu(qp)
    z = _dot(xb, w_f)
    c1 = 1.0 - lb
    cs = c1 / (1.0 + jnp.exp(-z))
    f = lb + cs
    k = c1 - cs
    lf = jnp.log(f)
    if valid is not None:
        lf = jnp.where(valid, lf, 0.0)
        k = jnp.where(valid, k, 0.0)
    cum = _split_dot(tri, lf)
    return q * jnp.exp(cum), k * jnp.exp(-cum), cum, lf


def _rms_gate(o, gate):
    ms = jnp.mean(o * o, axis=-1, keepdims=True)
    return o * lax.rsqrt(ms + LN_EPS) * gate


def _hgrn_prompt_kernel(x_ref, w_ref, lbl_ref, gn_ref, wo_ref, lng_ref, lnb_ref,
                        y_ref, s_ref, st_scr, og_scr, h_scr, *, layer, tiles_per_row):
    s = pl.program_id(0)
    n_tiles = pl.num_programs(0) - 1

    @pl.when(s == 0)
    def _():
        h_scr[...] = jnp.zeros_like(h_scr)

    @pl.when(s < n_tiles)
    def _():
        _hgrn_tile_step(lax.rem(s, tiles_per_row), tiles_per_row, x_ref, w_ref, lbl_ref, gn_ref, wo_ref,
                        lng_ref, lnb_ref, y_ref, s_ref, st_scr, og_scr, h_scr, layer=layer)

    @pl.when(s == n_tiles)
    def _():
        y_ref[0] = _layer_norm(h_scr[...], lng_ref[...], lnb_ref[...])


def _hgrn_tile_step(j, tiles_per_row, x_ref, w_ref, lbl_ref, gn_ref, wo_ref, lng_ref, lnb_ref,
                    y_ref, s_ref, st_scr, og_scr, h_scr, *, layer):
    tile = x_ref.shape[1]
    n_blocks = tile // HGRN_BLOCK
    chunks_per_block = HGRN_BLOCK // CHUNK_A

    @pl.when(j == 0)
    def _():
        st_scr[...] = jnp.zeros_like(st_scr)

    y_ref[0] = _layer_norm(h_scr[...], lng_ref[...], lnb_ref[...])

    x = x_ref[0]
    xb = x.astype(BF16)
    _, causal = _block_masks(HGRN_BLOCK, CHUNK_A)
    tri = jnp.where(causal, 1.0, 0.0).astype(BF16)

    def block(p):
        return slice(p * HGRN_BLOCK, (p + 1) * HGRN_BLOCK)

    def stage_project(hp):
        z = _dot(xb, _group_cols(w_ref, N_GROUPS + hp))
        qp = _dot(xb, _group_cols(w_ref, hp))
        v = _dot(xb, _group_cols(w_ref, 2 * N_GROUPS + hp))
        g = _dot(xb, _group_cols(w_ref, 3 * N_GROUPS + hp))
        lb = _forget_lower_bound(lbl_ref[hp], layer)
        c1 = 1.0 - lb
        cs = c1 / (1.0 + jnp.exp(-z))
        k = c1 - cs
        lf = jnp.log(lb + cs)
        cum = jnp.concatenate([_split_dot(tri, lf[block(p)]) for p in range(n_blocks)], axis=0)
        return _silu(qp), k, cum, v.astype(BF16), _silu(g) * gn_ref[hp]

    def stage_scores(hp, a):
        q, k, cum, vb, gate = a
        qdb = (q * jnp.exp(cum)).astype(BF16)
        ki = k * jnp.exp(-cum)
        kib = ki.astype(BF16)
        n_chunks = tile // CHUNK_A
        dl_rows = jnp.exp(jnp.concatenate(
            [cum[(c + 1) * CHUNK_A - 1:(c + 1) * CHUNK_A] for c in range(n_chunks)]
            + [jnp.zeros((HEAD - n_chunks, LANE_GROUP), F32)], axis=0))
        dl_cols = dl_rows.T
        heads = []
        for i in range(HEADS_PER_GROUP):
            ls = slice(i * HEAD, (i + 1) * HEAD)
            kit = ki[:, ls].T.astype(BF16)
            scores = [_dot(qdb[block(p), ls], kit[:, block(p)]) for p in range(n_blocks)]
            dls, us = [], []
            for c in range(n_chunks):
                lo, hi = c * CHUNK_A, (c + 1) * CHUNK_A
                ke = (ki[lo:hi, ls] * dl_rows[c:c + 1, ls]).astype(BF16)
                dls.append(dl_cols[ls, c:c + 1])
                us.append(_dot_tn(ke, vb[lo:hi, ls]))
            heads.append((qdb[:, ls], vb[:, ls], scores, dls, us))
        return heads, gate

    def stage_output(hp, b):
        heads, gate = b
        ogs = []
        for i, (qdb, vb, scores, dls, us) in enumerate(heads):
            st = st_scr[HEADS_PER_GROUP * hp + i]
            os = []
            for p in range(n_blocks):
                o_intra = _dot(jnp.where(causal, scores[p], 0.0).astype(BF16), vb[block(p)])
                for cc in range(chunks_per_block):
                    c = p * chunks_per_block + cc
                    lo, hi = c * CHUNK_A, (c + 1) * CHUNK_A
                    os.append(o_intra[cc * CHUNK_A:(cc + 1) * CHUNK_A] + _dot(qdb[lo:hi], st.astype(BF16)))
                    st = st * dls[c] + us[c]
            st_scr[HEADS_PER_GROUP * hp + i] = st
            o = jnp.concatenate(os, axis=0)
            ogs.append(_rms_gate(o, gate[:, i * HEAD:(i + 1) * HEAD]))
        og_scr[hp] = jnp.concatenate(ogs, axis=1).astype(BF16)

    def out_proj(groups):
        y = _dot(og_scr[groups[0]], wo_ref[groups[0]])
        for hp in groups[1:]:
            y = y + _dot(og_scr[hp], wo_ref[hp])
        return y

    y_parts = []
    projected, scored = {}, {}
    for t in range(N_GROUPS + 2):
        if t < N_GROUPS:
            projected[t] = stage_project(t)
        if t == N_GROUPS:
            y_parts.append(out_proj(list(range(0, N_GROUPS - 4))))
        if t == N_GROUPS + 1:
            y_parts.append(out_proj([N_GROUPS - 4, N_GROUPS - 3]))
        if 0 <= t - 1 < N_GROUPS:
            scored[t - 1] = stage_scores(t - 1, projected.pop(t - 1))
        if 0 <= t - 2 < N_GROUPS:
            stage_output(t - 2, scored.pop(t - 2))
    y_parts.append(out_proj([N_GROUPS - 2, N_GROUPS - 1]))
    y = y_parts[0] + y_parts[1] + y_parts[2]
    h_scr[...] = ALPHA * x + y

    @pl.when(j == tiles_per_row - 1)
    def _():
        s_ref[0] = st_scr[...]


def _resident(shape):
    nd = len(shape)
    return pl.BlockSpec(shape, lambda *_: (0,) * nd, pipeline_mode=pl.Buffered(1))


def _hgrn_prompt(x, w3, lbl3, gn3, wo3, lng, lnb, layer):
    bsz, seq, _ = x.shape
    tiles_per_row = seq // HGRN_TILE
    n_tiles = bsz * tiles_per_row

    def tile_in(s):
        t = jnp.minimum(s, n_tiles - 1)
        return t // tiles_per_row, t % tiles_per_row

    def tile_out(s):
        t = jnp.maximum(s - 1, 0)
        return t // tiles_per_row, t % tiles_per_row

    return pl.pallas_call(
        functools.partial(_hgrn_prompt_kernel, layer=layer, tiles_per_row=tiles_per_row),
        grid=(n_tiles + 1,),
        in_specs=[
            pl.BlockSpec((1, HGRN_TILE, D_MODEL), lambda s: (*tile_in(s), 0)),
            _resident(w3.shape), _resident(lbl3.shape), _resident(gn3.shape), _resident(wo3.shape),
            _resident(lng.shape), _resident(lnb.shape),
        ],
        out_specs=[
            pl.BlockSpec((1, HGRN_TILE, D_MODEL), lambda s: (*tile_out(s), 0)),
            pl.BlockSpec((1, N_HEADS, HEAD, HEAD), lambda s: (tile_in(s)[0], 0, 0, 0)),
        ],
        out_shape=[
            jax.ShapeDtypeStruct(x.shape, F32),
            jax.ShapeDtypeStruct((bsz, N_HEADS, HEAD, HEAD), F32),
        ],
        scratch_shapes=[
            pltpu.VMEM((N_HEADS, HEAD, HEAD), F32),
            pltpu.VMEM((N_GROUPS, HGRN_TILE, LANE_GROUP), BF16),
            pltpu.VMEM((HGRN_TILE, D_MODEL), F32),
        ],
        compiler_params=pltpu.CompilerParams(
            dimension_semantics=("arbitrary",), vmem_limit_bytes=VMEM_LIMIT_BYTES),
        name="hgrn_prompt",
    )(x, w3, lbl3, gn3, wo3, lng, lnb)


def _hgrn_sample_gates_kernel(x_ref, wq_ref, wf_ref, wi_ref, wg_ref, lbl_ref, gn_ref,
                              qd_ref, ki_ref, ke_ref, v_ref, gate_ref, dl_ref, *, layer, dec_seq):
    rows = x_ref.shape[0]
    xb = x_ref[...].astype(BF16)
    same, causal = _block_masks(ROW_TILE, dec_seq)
    tri = jnp.where(causal, 1.0, 0.0).astype(BF16)
    blk = jnp.where(same, 1.0, 0.0).astype(BF16)
    lb = _forget_lower_bound(lbl_ref[0], layer)

    qp = _dot(xb, wq_ref[...])
    q = _silu(qp)
    z = _dot(xb, wf_ref[...])
    c1 = 1.0 - lb
    cs = c1 / (1.0 + jnp.exp(-z))
    k = c1 - cs
    lf = jnp.log(lb + cs)
    for r in range(rows // ROW_TILE):
        rs = slice(r * ROW_TILE, (r + 1) * ROW_TILE)
        cum = _split_dot(tri, lf[rs])
        cum_last = _split_dot(blk, lf[rs])
        ki = k[rs] * jnp.exp(-cum)
        dl = jnp.exp(cum_last)
        qd_ref[0, rs] = q[rs] * jnp.exp(cum)
        ki_ref[0, rs] = ki
        ke_ref[0, rs] = ki * dl
        dl_ref[0, rs] = dl
    v_ref[0] = _dot(xb, wi_ref[...])
    gate_ref[0] = _silu(_dot(xb, wg_ref[...])) * gn_ref[0]


def _hgrn_sample_gates(x_rows, w3, lbl3, gn3, layer, dec_seq):
    rows = x_rows.shape[0]
    assert rows % ROW_TILE == 0 and SUBLANES % dec_seq == 0
    w_spec = lambda part: pl.BlockSpec((D_MODEL, LANE_GROUP), lambda g: (0, part * N_GROUPS + g))
    out_spec = pl.BlockSpec((1, rows, LANE_GROUP), lambda g: (g, 0, 0))
    out_shape = jax.ShapeDtypeStruct((N_GROUPS, rows, LANE_GROUP), F32)
    return pl.pallas_call(
        functools.partial(_hgrn_sample_gates_kernel, layer=layer, dec_seq=dec_seq),
        grid=(N_GROUPS,),
        in_specs=[
            pl.BlockSpec((rows, D_MODEL), lambda g: (0, 0)),
            w_spec(0), w_spec(1), w_spec(2), w_spec(3),
            pl.BlockSpec((1,) + lbl3.shape[1:], lambda g: (g, 0, 0)),
            pl.BlockSpec((1, 1, LANE_GROUP), lambda g: (g, 0, 0)),
        ],
        out_specs=[out_spec] * 6,
        out_shape=[out_shape] * 6,
        compiler_params=pltpu.CompilerParams(
            dimension_semantics=("arbitrary",), vmem_limit_bytes=VMEM_LIMIT_BYTES),
        name="hgrn_sample_gates",
    )(x_rows, w3, w3, w3, w3, lbl3, gn3)


SEQS_PER_STEP = 4


def _hgrn_sample_state_kernel(qd_ref, ki_ref, ke_ref, v_ref, gate_ref, dl_ref, s_ref,
                              og_ref, so_ref, *, dec_seq):
    seqs_per_tile = SUBLANES // dec_seq
    _, causal = _block_masks(SUBLANES, dec_seq)
    shift = dec_seq.bit_length() - 1
    trow = jnp.right_shift(lax.broadcasted_iota(jnp.int32, (SUBLANES, HEAD), 0), shift)
    in_seq = [trow == s for s in range(seqs_per_tile)]
    pad_rows = jnp.zeros((HEAD - N_HEADS, HEAD), F32)

    def tile_body(m, carry):
        rs = pl.ds(pl.multiple_of(m * SUBLANES, SUBLANES), SUBLANES)
        dl = [dl_ref[hp, rs, :] for hp in range(N_GROUPS)]
        dcols = []
        for s in range(seqs_per_tile):
            r = s * dec_seq
            rows = [dl[h // HEADS_PER_GROUP][r:r + 1, (h % HEADS_PER_GROUP) * HEAD:(h % HEADS_PER_GROUP + 1) * HEAD]
                    for h in range(N_HEADS)]
            dcols.append(jnp.concatenate(rows + [pad_rows], axis=0).T)
        pending = []
        for hp in range(N_GROUPS):
            qd = qd_ref[hp, rs, :].astype(BF16)
            ki = ki_ref[hp, rs, :].astype(BF16)
            ke = ke_ref[hp, rs, :]
            vb = v_ref[hp, rs, :].astype(BF16)
            for i in range(HEADS_PER_GROUP):
                ls = slice(i * HEAD, (i + 1) * HEAD)
                h = HEADS_PER_GROUP * hp + i
                scores = _dot_nt(qd[:, ls], ki[:, ls])
                o_inter = None
                for s in range(seqs_per_tile):
                    b = m * seqs_per_tile + s
                    st = s_ref[b, h]
                    o_s = _dot(qd[:, ls], st.astype(BF16))
                    o_inter = o_s if s == 0 else jnp.where(in_seq[s], o_s, o_inter)
                    u = _dot_tn(jnp.where(in_seq[s], ke[:, ls], 0.0).astype(BF16), vb[:, ls])
                    so_ref[b, h] = dcols[s][:, h:h + 1] * st + u
                pending.append((hp, i, scores, o_inter, vb[:, ls]))
        ogs = {}
        for hp, i, scores, o_inter, vbh in pending:
            o = _dot(jnp.where(causal, scores, 0.0).astype(BF16), vbh) + o_inter
            gate = gate_ref[hp, rs, :]
            ogs[(hp, i)] = _rms_gate(o, gate[:, i * HEAD:(i + 1) * HEAD])
        for hp in range(N_GROUPS):
            og_ref[hp, rs, :] = jnp.concatenate([ogs[(hp, i)] for i in range(HEADS_PER_GROUP)], axis=1)
        return carry

    lax.fori_loop(0, SEQS_PER_STEP // seqs_per_tile, tile_body, 0)


def _hgrn_sample_state(qd, ki, ke, v, gate, dl, state, dec_seq):
    n_seq = state.shape[0]
    rows_per_step = SEQS_PER_STEP * dec_seq
    assert rows_per_step % SUBLANES == 0
    tok_spec = pl.BlockSpec((N_GROUPS, rows_per_step, LANE_GROUP), lambda s: (0, s, 0))
    st_spec = pl.BlockSpec((SEQS_PER_STEP, N_HEADS, HEAD, HEAD), lambda s: (s, 0, 0, 0))
    return pl.pallas_call(
        functools.partial(_hgrn_sample_state_kernel, dec_seq=dec_seq),
        grid=(n_seq // SEQS_PER_STEP,),
        in_specs=[tok_spec] * 6 + [st_spec],
        out_specs=[tok_spec, st_spec],
        out_shape=[jax.ShapeDtypeStruct(qd.shape, F32), jax.ShapeDtypeStruct(state.shape, F32)],
        compiler_params=pltpu.CompilerParams(
            dimension_semantics=("arbitrary",), vmem_limit_bytes=VMEM_LIMIT_BYTES),
        name="hgrn_sample_state",
    )(qd, ki, ke, v, gate, dl, state)


def _out_proj_ln_kernel(og_ref, x_ref, wo_ref, lng_ref, lnb_ref, y_ref):
    y = _dot(og_ref[0].astype(BF16), wo_ref[0])
    for hp in range(1, N_GROUPS):
        y = y + _dot(og_ref[hp].astype(BF16), wo_ref[hp])
    y_ref[...] = _layer_norm(ALPHA * x_ref[...] + y, lng_ref[...], lnb_ref[...])


def _out_proj_ln(og, x, wo3, lng, lnb):
    rows = x.shape[0]
    return pl.pallas_call(
        _out_proj_ln_kernel,
        grid=(rows // ROW_TILE,),
        in_specs=[
            pl.BlockSpec((N_GROUPS, ROW_TILE, LANE_GROUP), lambda r: (0, r, 0)),
            pl.BlockSpec((ROW_TILE, D_MODEL), lambda r: (r, 0)),
            _resident(wo3.shape), _resident(lng.shape), _resident(lnb.shape),
        ],
        out_specs=pl.BlockSpec((ROW_TILE, D_MODEL), lambda r: (r, 0)),
        out_shape=jax.ShapeDtypeStruct(x.shape, F32),
        compiler_params=pltpu.CompilerParams(
            dimension_semantics=("arbitrary",), vmem_limit_bytes=VMEM_LIMIT_BYTES),
        name="out_proj_ln",
    )(og, x, wo3, lng, lnb)


def _gmlp_kernel(x_ref, w_ref, vg_ref, vb_ref, ws_ref, bs_ref, wo_ref, lng_ref, lnb_ref,
                 *out_and_scratch, emit_v):
    if emit_v:
        y_ref, vn_ref, v_scr, og_scr, h_scr = out_and_scratch
    else:
        y_ref, v_scr, og_scr, h_scr = out_and_scratch
        vn_ref = None
    s = pl.program_id(0)
    n_tiles = pl.num_programs(0) - 1

    @pl.when(s == 0)
    def _():
        h_scr[...] = jnp.zeros_like(h_scr)

    @pl.when(s < n_tiles)
    def _():
        _gmlp_tile_step(x_ref, w_ref, vg_ref, vb_ref, ws_ref, bs_ref, wo_ref, lng_ref, lnb_ref,
                        y_ref, vn_ref, v_scr, og_scr, h_scr)

    @pl.when(s == n_tiles)
    def _():
        y_ref[...] = _layer_norm(h_scr[...], lng_ref[...], lnb_ref[...])


def _gmlp_tile_step(x_ref, w_ref, vg_ref, vb_ref, ws_ref, bs_ref, wo_ref, lng_ref, lnb_ref,
                    y_ref, vn_ref, v_scr, og_scr, h_scr):
    y_ref[...] = _layer_norm(h_scr[...], lng_ref[...], lnb_ref[...])

    n_chunks = x_ref.shape[0] // CHUNK_B
    x = x_ref[...]
    xb = x.astype(BF16)

    half = x_ref.shape[0] // 2
    for gp in range(N_GROUPS):
        for r in range(2):
            rows = slice(r * half, (r + 1) * half)
            v_scr[gp, rows] = _gelu_tanh(_dot(xb[rows], _group_cols(w_ref, N_GROUPS + gp)))

    s1 = jnp.sum(v_scr[0], axis=-1, keepdims=True)
    for gp in range(1, N_GROUPS):
        s1 = s1 + jnp.sum(v_scr[gp], axis=-1, keepdims=True)
    mu = s1 * (1.0 / E)
    s2 = jnp.zeros_like(mu)
    for gp in range(N_GROUPS):
        d = v_scr[gp] - mu
        s2 = s2 + jnp.sum(d * d, axis=-1, keepdims=True)
    rstd = lax.rsqrt(s2 * (1.0 / E) + LN_EPS)

    row = lax.broadcasted_iota(jnp.int32, (CHUNK_B, CHUNK_B), 0)
    col = lax.broadcasted_iota(jnp.int32, (CHUNK_B, CHUNK_B), 1)
    causal = col <= row

    def stage_gate(gp):
        return _dot(xb, _group_cols(w_ref, gp)), _dot(xb, _group_cols(w_ref, 2 * N_GROUPS + gp))

    def stage_mix(gp, gate):
        u_pre, z = gate
        vn = (v_scr[gp] - mu) * rstd * vg_ref[gp] + vb_ref[gp]
        if vn_ref is not None:
            vn_ref[gp] = vn
        vnb = vn.astype(BF16)
        cols = []
        for i in range(HEADS_PER_GROUP):
            ls = slice(i * HEAD, (i + 1) * HEAD)
            g = HEADS_PER_GROUP * gp + i
            wc = jnp.where(causal, ws_ref[g], 0.0).astype(BF16)
            bias = bs_ref[g]
            cols.append(jnp.concatenate(
                [_dot(wc, vnb[c * CHUNK_B:(c + 1) * CHUNK_B, ls]) + bias for c in range(n_chunks)], axis=0))
        mixed = jnp.concatenate(cols, axis=1)
        og_scr[gp] = (_gelu_tanh(u_pre) * mixed * _silu(z)).astype(BF16)

    def out_proj(groups):
        y = _dot(og_scr[groups[0]], wo_ref[groups[0]])
        for gp in groups[1:]:
            y = y + _dot(og_scr[gp], wo_ref[gp])
        return y

    y_parts = []
    gates = {0: stage_gate(0)}
    for gp in range(N_GROUPS):
        if gp + 1 < N_GROUPS:
            gates[gp + 1] = stage_gate(gp + 1)
        stage_mix(gp, gates.pop(gp))
        if gp == N_GROUPS - 2:
            y_parts.append(out_proj(list(range(0, N_GROUPS - 3))))
    y_parts.append(out_proj([N_GROUPS - 3, N_GROUPS - 2]))
    y_parts.append(out_proj([N_GROUPS - 1]))
    y = y_parts[0] + y_parts[1] + y_parts[2]
    h_scr[...] = ALPHA * x + y


def _gmlp(x, w3, vg3, vb3, ws, bs_col, wo3, lng, lnb, emit_v):
    rows = x.shape[0]
    tile = min(GMLP_TILE, rows)
    n_tiles = rows // tile
    tile_in = lambda s: jnp.minimum(s, n_tiles - 1)
    tile_out = lambda s: jnp.maximum(s - 1, 0)
    out_specs = [pl.BlockSpec((tile, D_MODEL), lambda s: (tile_out(s), 0))]
    out_shape = [jax.ShapeDtypeStruct(x.shape, F32)]
    if emit_v:
        out_specs.append(pl.BlockSpec((N_GROUPS, tile, LANE_GROUP), lambda s: (0, tile_in(s), 0)))
        out_shape.append(jax.ShapeDtypeStruct((N_GROUPS, rows, LANE_GROUP), F32))
    return pl.pallas_call(
        functools.partial(_gmlp_kernel, emit_v=emit_v),
        grid=(n_tiles + 1,),
        in_specs=[
            pl.BlockSpec((tile, D_MODEL), lambda s: (tile_in(s), 0)),
            _resident(w3.shape), _resident(vg3.shape), _resident(vb3.shape), _resident(ws.shape),
            _resident(bs_col.shape), _resident(wo3.shape), _resident(lng.shape), _resident(lnb.shape),
        ],
        out_specs=out_specs,
        out_shape=out_shape,
        scratch_shapes=[
            pltpu.VMEM((N_GROUPS, tile, LANE_GROUP), F32),
            pltpu.VMEM((N_GROUPS, tile, LANE_GROUP), BF16),
            pltpu.VMEM((tile, D_MODEL), F32),
        ],
        compiler_params=pltpu.CompilerParams(
            dimension_semantics=("arbitrary",), vmem_limit_bytes=VMEM_LIMIT_BYTES),
        name="gmlp_emit_v" if emit_v else "gmlp",
    )(x, w3, vg3, vb3, ws, bs_col, wo3, lng, lnb)


def _lane_groups_of_rows(w):
    return w.reshape(w.shape[0] // LANE_GROUP, LANE_GROUP, w.shape[1]).astype(BF16)


def _lane_groups_of_vector(v):
    return v.reshape(v.shape[0], N_GROUPS, LANE_GROUP).transpose(1, 0, 2)


def kernel(x_prompt, x_sample, state_hgrn, w_in_a, lb_logits_a, gnorm_a, w_out_a, w_in_b, lnv_g_b,
           lnv_b_b, w_s_b, b_s_b, w_out_b, ln_g, ln_b):
    bsz, seq, _ = x_prompt.shape
    n_seq, dec_seq, _ = x_sample.shape

    w3a = w_in_a[0].astype(BF16)
    wo3a = _lane_groups_of_rows(w_out_a[0])
    lbl3 = _lane_groups_of_vector(lb_logits_a)
    gn3 = _lane_groups_of_vector(gnorm_a[0:1])
    lng0, lnb0 = ln_g[0:1], ln_b[0:1]

    hp, sp = _hgrn_prompt(x_prompt, w3a, lbl3, gn3, wo3a, lng0, lnb0, layer=0)

    xs = x_sample.reshape(n_seq * dec_seq, D_MODEL)
    qd, ki, ke, v, gate, dl = _hgrn_sample_gates(xs, w3a, lbl3, gn3, layer=0, dec_seq=dec_seq)
    og, ss = _hgrn_sample_state(qd, ki, ke, v, gate, dl, state_hgrn[0], dec_seq=dec_seq)
    hs = _out_proj_ln(og, xs, wo3a, lng0, lnb0)

    w3b = _group_major(w_in_b[0])
    wo3b = _lane_groups_of_rows(w_out_b[0])
    vg3 = _lane_groups_of_vector(lnv_g_b[0:1])
    vb3 = _lane_groups_of_vector(lnv_b_b[0:1])
    lng1, lnb1 = ln_g[1:2], ln_b[1:2]
    ws = w_s_b[0]
    bs = b_s_b[0]

    yp = _gmlp(hp.reshape(bsz * seq, D_MODEL), w3b, vg3, vb3, ws, bs[:, :, None], wo3b, lng1, lnb1,
               emit_v=False)[0].reshape(bsz, seq, D_MODEL)

    reps = CHUNK_B // dec_seq
    ws_dec = jnp.einsum('ab,gts->gatbs', jnp.eye(reps, dtype=F32), ws[:, :dec_seq, :dec_seq]).reshape(
        ws.shape[0], CHUNK_B, CHUNK_B)
    bs_dec = jnp.tile(bs[:, :dec_seq], (1, reps))
    ys, vn = _gmlp(hs.reshape(n_seq * dec_seq, D_MODEL), w3b, vg3, vb3, ws_dec, bs_dec[:, :, None], wo3b,
                   lng1, lnb1, emit_v=True)
    ys = ys.reshape(n_seq, dec_seq, D_MODEL)
    vs = vn.transpose(1, 0, 2).reshape(n_seq, dec_seq, E)

    return (yp, ys, sp[None], ss[None], vs[None])
```

```python
import functools
import math

import jax
import jax.numpy as jnp
from jax import lax
from jax.experimental import pallas as pl
from jax.experimental.pallas import tpu as pltpu

F32 = jnp.float32
BF16 = jnp.bfloat16

D_MODEL = 1024
E = 2048
HEAD = 128
N_HEADS = E // HEAD
LANE_GROUP = 256
N_GROUPS = E // LANE_GROUP
HEADS_PER_GROUP = LANE_GROUP // HEAD
CHUNK_A = 64
CHUNK_B = 128
ROW_TILE = 256
HGRN_TILE = 512
HGRN_BLOCK = 2 * CHUNK_A
GMLP_TILE = 512
SUBLANES = 8
DEPTH = 2
ALPHA = (2 * DEPTH) ** 0.25
LN_EPS = 1e-5
VMEM_LIMIT_BYTES = 60000 * 1024


def _dot(a, b):
    return jnp.dot(a, b, preferred_element_type=F32)


def _group_cols(w_ref, group):
    if len(w_ref.shape) == 3:
        return w_ref[group]
    return w_ref[:, group * LANE_GROUP:(group + 1) * LANE_GROUP]


def _group_major(w):
    k, n = w.shape
    return w.reshape(k, n // LANE_GROUP, LANE_GROUP).transpose(1, 0, 2).astype(BF16)


def _dot_nt(a, b):
    return lax.dot_general(a, b, (((1,), (1,)), ((), ())), preferred_element_type=F32)


def _dot_tn(a, b):
    return lax.dot_general(a, b, (((0,), (0,)), ((), ())), preferred_element_type=F32)


def _silu(x):
    return x / (1.0 + jnp.exp(-x))


def _gelu_tanh(x):
    cdf = 0.5 * (1.0 + jnp.tanh(math.sqrt(2.0 / math.pi) * (x + 0.044715 * (x * x * x))))
    return x * cdf


def _layer_norm(x, g, b):
    mu = jnp.mean(x, axis=-1, keepdims=True)
    d = x - mu
    var = jnp.mean(d * d, axis=-1, keepdims=True)
    return d * lax.rsqrt(var + LN_EPS) * g + b


def _block_masks(n, block):
    shift = block.bit_length() - 1
    row = lax.broadcasted_iota(jnp.int32, (n, n), 0)
    col = lax.broadcasted_iota(jnp.int32, (n, n), 1)
    same = jnp.right_shift(row, shift) == jnp.right_shift(col, shift)
    causal = jnp.logical_and(same, col <= row)
    return same, causal


def _split_dot(m, x):
    hi = x.astype(BF16)
    lo = (x - hi.astype(F32)).astype(BF16)
    return _dot(jnp.concatenate([m, m], axis=1), jnp.concatenate([hi, lo], axis=0))


def _forget_lower_bound(logits, layer):
    m = jnp.max(logits, axis=0, keepdims=True)
    e = jnp.exp(logits - m)
    den = jnp.sum(e, axis=0, keepdims=True)
    num = jnp.sum(e[: layer + 1], axis=0, keepdims=True)
    return num / den


def _hgrn_gates(xb, w_q, w_f, lb, tri, valid):
    qp = _dot(xb, w_q)
    q = _silu(qp)
    z = _dot(xb, w_f)
    c1 = 1.0 - lb
    cs = c1 / (1.0 + jnp.exp(-z))
    f = lb + cs
    k = c1 - cs
    lf = jnp.log(f)
    if valid is not None:
        lf = jnp.where(valid, lf, 0.0)
        k = jnp.where(valid, k, 0.0)
    cum = _split_dot(tri, lf)
    return q * jnp.exp(cum), k * jnp.exp(-cum), cum, lf


def _rms_gate(o, gate):
    ms = jnp.mean(o * o, axis=-1, keepdims=True)
    return o * lax.rsqrt(ms + LN_EPS) * gate


def _hgrn_prompt_kernel(x_ref, w_ref, lbl_ref, gn_ref, wo_ref, lng_ref, lnb_ref,
                        y_ref, s_ref, st_scr, og_scr, h_scr, *, layer, tiles_per_row):
    s = pl.program_id(0)
    n_tiles = pl.num_programs(0) - 1

    @pl.when(s == 0)
    def _():
        h_scr[...] = jnp.zeros_like(h_scr)

    @pl.when(s < n_tiles)
    def _():
        _hgrn_tile_step(lax.rem(s, tiles_per_row), tiles_per_row, x_ref, w_ref, lbl_ref, gn_ref, wo_ref,
                        lng_ref, lnb_ref, y_ref, s_ref, st_scr, og_scr, h_scr, layer=layer)

    @pl.when(s == n_tiles)
    def _():
        y_ref[0] = _layer_norm(h_scr[...], lng_ref[...], lnb_ref[...])


def _hgrn_tile_step(j, tiles_per_row, x_ref, w_ref, lbl_ref, gn_ref, wo_ref, lng_ref, lnb_ref,
                    y_ref, s_ref, st_scr, og_scr, h_scr, *, layer):
    tile = x_ref.shape[1]
    n_blocks = tile // HGRN_BLOCK
    chunks_per_block = HGRN_BLOCK // CHUNK_A

    @pl.when(j == 0)
    def _():
        st_scr[...] = jnp.zeros_like(st_scr)

    y_ref[0] = _layer_norm(h_scr[...], lng_ref[...], lnb_ref[...])

    x = x_ref[0]
    xb = x.astype(BF16)
    _, causal = _block_masks(HGRN_BLOCK, CHUNK_A)
    tri = jnp.where(causal, 1.0, 0.0).astype(BF16)

    def block(p):
        return slice(p * HGRN_BLOCK, (p + 1) * HGRN_BLOCK)

    def stage_project(hp):
        z = _dot(xb, _group_cols(w_ref, N_GROUPS + hp))
        qp = _dot(xb, _group_cols(w_ref, hp))
        v = _dot(xb, _group_cols(w_ref, 2 * N_GROUPS + hp))
        g = _dot(xb, _group_cols(w_ref, 3 * N_GROUPS + hp))
        lb = _forget_lower_bound(lbl_ref[hp], layer)
        c1 = 1.0 - lb
        cs = c1 / (1.0 + jnp.exp(-z))
        k = c1 - cs
        lf = jnp.log(lb + cs)
        cum = jnp.concatenate([_split_dot(tri, lf[block(p)]) for p in range(n_blocks)], axis=0)
        return _silu(qp), k, cum, v.astype(BF16), _silu(g) * gn_ref[hp]

    def stage_scores(hp, a):
        q, k, cum, vb, gate = a
        qdb = (q * jnp.exp(cum)).astype(BF16)
        ki = k * jnp.exp(-cum)
        kib = ki.astype(BF16)
        n_chunks = tile // CHUNK_A
        dl_rows = jnp.exp(jnp.concatenate(
            [cum[(c + 1) * CHUNK_A - 1:(c + 1) * CHUNK_A] for c in range(n_chunks)]
            + [jnp.zeros((HEAD - n_chunks, LANE_GROUP), F32)], axis=0))
        dl_cols = dl_rows.T
        heads = []
        for i in range(HEADS_PER_GROUP):
            ls = slice(i * HEAD, (i + 1) * HEAD)
            kit = ki[:, ls].T.astype(BF16)
            scores = [_dot(qdb[block(p), ls], kit[:, block(p)]) for p in range(n_blocks)]
            dls, us = [], []
            for c in range(n_chunks):
                lo, hi = c * CHUNK_A, (c + 1) * CHUNK_A
                ke = (ki[lo:hi, ls] * dl_rows[c:c + 1, ls]).astype(BF16)
                dls.append(dl_cols[ls, c:c + 1])
                us.append(_dot_tn(ke, vb[lo:hi, ls]))
            heads.append((qdb[:, ls], vb[:, ls], scores, dls, us))
        return heads, gate

    def stage_output(hp, b):
        heads, gate = b
        ogs = []
        for i, (qdb, vb, scores, dls, us) in enumerate(heads):
            st = st_scr[HEADS_PER_GROUP * hp + i]
            os = []
            for p in range(n_blocks):
                o_intra = _dot(jnp.where(causal, scores[p], 0.0).astype(BF16), vb[block(p)])
                for cc in range(chunks_per_block):
                    c = p * chunks_per_block + cc
                    lo, hi = c * CHUNK_A, (c + 1) * CHUNK_A
                    os.append(o_intra[cc * CHUNK_A:(cc + 1) * CHUNK_A] + _dot(qdb[lo:hi], st.astype(BF16)))
                    st = st * dls[c] + us[c]
            st_scr[HEADS_PER_GROUP * hp + i] = st
            o = jnp.concatenate(os, axis=0)
            ogs.append(_rms_gate(o, gate[:, i * HEAD:(i + 1) * HEAD]))
        og_scr[hp] = jnp.concatenate(ogs, axis=1).astype(BF16)

    def out_proj(groups):
        y = _dot(og_scr[groups[0]], wo_ref[groups[0]])
        for hp in groups[1:]:
            y = y + _dot(og_scr[hp], wo_ref[hp])
        return y

    y_parts = []
    projected, scored = {}, {}
    for t in range(N_GROUPS + 2):
        if t < N_GROUPS:
            projected[t] = stage_project(t)
        if t == N_GROUPS:
            y_parts.append(out_proj(list(range(0, N_GROUPS - 4))))
        if t == N_GROUPS + 1:
            y_parts.append(out_proj([N_GROUPS - 4, N_GROUPS - 3]))
        if 0 <= t - 1 < N_GROUPS:
            scored[t - 1] = stage_scores(t - 1, projected.pop(t - 1))
        if 0 <= t - 2 < N_GROUPS:
            stage_output(t - 2, scored.pop(t - 2))
    y_parts.append(out_proj([N_GROUPS - 2, N_GROUPS - 1]))
    y = y_parts[0] + y_parts[1] + y_parts[2]
    h_scr[...] = ALPHA * x + y

    @pl.when(j == tiles_per_row - 1)
    def _():
        s_ref[0] = st_scr[...]


def _resident(shape):
    nd = len(shape)
    return pl.BlockSpec(shape, lambda *_: (0,) * nd, pipeline_mode=pl.Buffered(1))


def _hgrn_prompt(x, w3, lbl3, gn3, wo3, lng, lnb, layer):
    bsz, seq, _ = x.shape
    tiles_per_row = seq // HGRN_TILE
    n_tiles = bsz * tiles_per_row

    def tile_in(s):
        t = jnp.minimum(s, n_tiles - 1)
        return t // tiles_per_row, t % tiles_per_row

    def tile_out(s):
        t = jnp.maximum(s - 1, 0)
        return t // tiles_per_row, t % tiles_per_row

    return pl.pallas_call(
        functools.partial(_hgrn_prompt_kernel, layer=layer, tiles_per_row=tiles_per_row),
        grid=(n_tiles + 1,),
        in_specs=[
            pl.BlockSpec((1, HGRN_TILE, D_MODEL), lambda s: (*tile_in(s), 0)),
            _resident(w3.shape), _resident(lbl3.shape), _resident(gn3.shape), _resident(wo3.shape),
            _resident(lng.shape), _resident(lnb.shape),
        ],
        out_specs=[
            pl.BlockSpec((1, HGRN_TILE, D_MODEL), lambda s: (*tile_out(s), 0)),
            pl.BlockSpec((1, N_HEADS, HEAD, HEAD), lambda s: (tile_in(s)[0], 0, 0, 0)),
        ],
        out_shape=[
            jax.ShapeDtypeStruct(x.shape, F32),
            jax.ShapeDtypeStruct((bsz, N_HEADS, HEAD, HEAD), F32),
        ],
        scratch_shapes=[
            pltpu.VMEM((N_HEADS, HEAD, HEAD), F32),
            pltpu.VMEM((N_GROUPS, HGRN_TILE, LANE_GROUP), BF16),
            pltpu.VMEM((HGRN_TILE, D_MODEL), F32),
        ],
        compiler_params=pltpu.CompilerParams(
            dimension_semantics=("arbitrary",), vmem_limit_bytes=VMEM_LIMIT_BYTES),
        name="hgrn_prompt",
    )(x, w3, lbl3, gn3, wo3, lng, lnb)


def _hgrn_sample_gates_kernel(x_ref, wq_ref, wf_ref, wi_ref, wg_ref, lbl_ref, gn_ref,
                              qd_ref, ki_ref, ke_ref, v_ref, gate_ref, dl_ref, *, layer, dec_seq):
    rows = x_ref.shape[0]
    xb = x_ref[...].astype(BF16)
    same, causal = _block_masks(ROW_TILE, dec_seq)
    tri = jnp.where(causal, 1.0, 0.0).astype(BF16)
    blk = jnp.where(same, 1.0, 0.0).astype(BF16)
    lb = _forget_lower_bound(lbl_ref[0], layer)

    qp = _dot(xb, wq_ref[...])
    q = _silu(qp)
    z = _dot(xb, wf_ref[...])
    c1 = 1.0 - lb
    cs = c1 / (1.0 + jnp.exp(-z))
    k = c1 - cs
    lf = jnp.log(lb + cs)
    for r in range(rows // ROW_TILE):
        rs = slice(r * ROW_TILE, (r + 1) * ROW_TILE)
        cum = _split_dot(tri, lf[rs])
        cum_last = _split_dot(blk, lf[rs])
        ki = k[rs] * jnp.exp(-cum)
        dl = jnp.exp(cum_last)
        qd_ref[0, rs] = q[rs] * jnp.exp(cum)
        ki_ref[0, rs] = ki
        ke_ref[0, rs] = ki * dl
        dl_ref[0, rs] = dl
    v_ref[0] = _dot(xb, wi_ref[...])
    gate_ref[0] = _silu(_dot(xb, wg_ref[...])) * gn_ref[0]


def _hgrn_sample_gates(x_rows, w3, lbl3, gn3, layer, dec_seq):
    rows = x_rows.shape[0]
    assert rows % ROW_TILE == 0 and SUBLANES % dec_seq == 0
    w_spec = lambda part: pl.BlockSpec((D_MODEL, LANE_GROUP), lambda g: (0, part * N_GROUPS + g))
    out_spec = pl.BlockSpec((1, rows, LANE_GROUP), lambda g: (g, 0, 0))
    out_shape = jax.ShapeDtypeStruct((N_GROUPS, rows, LANE_GROUP), F32)
    return pl.pallas_call(
        functools.partial(_hgrn_sample_gates_kernel, layer=layer, dec_seq=dec_seq),
        grid=(N_GROUPS,),
        in_specs=[
            pl.BlockSpec((rows, D_MODEL), lambda g: (0, 0)),
            w_spec(0), w_spec(1), w_spec(2), w_spec(3),
            pl.BlockSpec((1,) + lbl3.shape[1:], lambda g: (g, 0, 0)),
            pl.BlockSpec((1, 1, LANE_GROUP), lambda g: (g, 0, 0)),
        ],
        out_specs=[out_spec] * 6,
        out_shape=[out_shape] * 6,
        compiler_params=pltpu.CompilerParams(
            dimension_semantics=("arbitrary",), vmem_limit_bytes=VMEM_LIMIT_BYTES),
        name="hgrn_sample_gates",
    )(x_rows, w3, w3, w3, w3, lbl3, gn3)


def _decode_state_stages(qd_ref, ki_ref, ke_ref, v_ref, gate_ref, dl_ref, s_ref, og_ref, so_ref, dec_seq):
    seqs_per_tile = SUBLANES // dec_seq
    n_tiles = qd_ref.shape[1] // SUBLANES
    shift = dec_seq.bit_length() - 1
    trow = jnp.right_shift(lax.broadcasted_iota(jnp.int32, (SUBLANES, HEAD), 0), shift)
    in_seq = [trow == s for s in range(seqs_per_tile)]
    pad_rows = jnp.zeros((HEAD - N_HEADS, HEAD), F32)
    tiles = [slice(m * SUBLANES, (m + 1) * SUBLANES) for m in range(n_tiles)]
    dcols, pending = {}, {}
    lane_head = lax.broadcasted_iota(jnp.int32, (SUBLANES, LANE_GROUP), 1) // HEAD
    srow = lax.broadcasted_iota(jnp.int32, (SUBLANES, HEADS_PER_GROUP * SUBLANES), 0)
    scol = jnp.bitwise_and(lax.broadcasted_iota(jnp.int32, (SUBLANES, HEADS_PER_GROUP * SUBLANES), 1),
                           SUBLANES - 1)
    causal = jnp.logical_and(jnp.right_shift(srow, shift) == jnp.right_shift(scol, shift), scol <= srow)

    def per_head_rows(t):
        return jnp.concatenate([jnp.where(lane_head == i, t, 0.0) for i in range(HEADS_PER_GROUP)], axis=0)

    def per_seq_lanes(t):
        return jnp.concatenate([jnp.where(in_seq[s], t, 0.0) for s in range(seqs_per_tile)], axis=1)

    def decay_columns(m):
        dl = [dl_ref[hp, tiles[m], :] for hp in range(N_GROUPS)]
        for s in range(seqs_per_tile):
            r = s * dec_seq
            rows = [dl[h // HEADS_PER_GROUP][r:r + 1, (h % HEADS_PER_GROUP) * HEAD:(h % HEADS_PER_GROUP + 1) * HEAD]
                    for h in range(N_HEADS)]
            dcols[(m, s)] = jnp.concatenate(rows + [pad_rows], axis=0).T

    def start(hp):
        for m in range(n_tiles):
            if hp == 0:
                decay_columns(m)
            rs = tiles[m]
            qd = qd_ref[hp, rs, :]
            ke = ke_ref[hp, rs, :]
            v = v_ref[hp, rs, :]
            qdb = qd.astype(BF16)
            scores = _dot_nt(qdb, per_head_rows(ki_ref[hp, rs, :]).astype(BF16))
            o_inter = []
            for i in range(HEADS_PER_GROUP):
                ls = slice(i * HEAD, (i + 1) * HEAD)
                h = HEADS_PER_GROUP * hp + i
                sts = [s_ref[m * seqs_per_tile + s, h] for s in range(seqs_per_tile)]
                st_rows = jnp.concatenate([st.astype(BF16) for st in sts], axis=0)
                o_inter.append(_dot(per_seq_lanes(qd[:, ls]).astype(BF16), st_rows))
                u = _dot_tn(per_seq_lanes(ke[:, ls]).astype(BF16), v[:, ls].astype(BF16))
                for s in range(seqs_per_tile):
                    so_ref[m * seqs_per_tile + s, h] = (dcols[(m, s)][:, h:h + 1] * sts[s]
                                                       + u[s * HEAD:(s + 1) * HEAD])
            pending[(hp, m)] = (scores, jnp.concatenate(o_inter, axis=1), per_head_rows(v).astype(BF16))

    def finish(hp):
        for m in range(n_tiles):
            scores, o_inter, v_rows = pending.pop((hp, m))
            o = _dot(jnp.where(causal, scores, 0.0).astype(BF16), v_rows) + o_inter
            gate = gate_ref[hp, tiles[m], :]
            og_ref[hp, tiles[m], :] = jnp.concatenate(
                [_rms_gate(o[:, i * HEAD:(i + 1) * HEAD], gate[:, i * HEAD:(i + 1) * HEAD])
                 for i in range(HEADS_PER_GROUP)], axis=1)

    return start, finish


def _out_proj_ln_kernel(og_ref, x_ref, wo_ref, lng_ref, lnb_ref, y_ref):
    y = _dot(og_ref[0].astype(BF16), wo_ref[0])
    for hp in range(1, N_GROUPS):
        y = y + _dot(og_ref[hp].astype(BF16), wo_ref[hp])
    y_ref[...] = _layer_norm(ALPHA * x_ref[...] + y, lng_ref[...], lnb_ref[...])


def _out_proj_ln(og, x, wo3, lng, lnb):
    rows = x.shape[0]
    return pl.pallas_call(
        _out_proj_ln_kernel,
        grid=(rows // ROW_TILE,),
        in_specs=[
            pl.BlockSpec((N_GROUPS, ROW_TILE, LANE_GROUP), lambda r: (0, r, 0)),
            pl.BlockSpec((ROW_TILE, D_MODEL), lambda r: (r, 0)),
            _resident(wo3.shape), _resident(lng.shape), _resident(lnb.shape),
        ],
        out_specs=pl.BlockSpec((ROW_TILE, D_MODEL), lambda r: (r, 0)),
        out_shape=jax.ShapeDtypeStruct(x.shape, F32),
        compiler_params=pltpu.CompilerParams(
            dimension_semantics=("arbitrary",), vmem_limit_bytes=VMEM_LIMIT_BYTES),
        name="out_proj_ln",
    )(og, x, wo3, lng, lnb)


N_GMLP_INPUTS = 9
N_DECODE_INPUTS = 7


def _gmlp_kernel(*refs, emit_v, decode_seq):
    refs = list(refs)
    gmlp_in = refs[:N_GMLP_INPUTS]
    del refs[:N_GMLP_INPUTS]
    decode_in = []
    if decode_seq:
        decode_in = refs[:N_DECODE_INPUTS]
        del refs[:N_DECODE_INPUTS]
    y_ref = refs.pop(0)
    vn_ref = refs.pop(0) if emit_v else None
    decode_out = [refs.pop(0), refs.pop(0)] if decode_seq else []
    v_scr, og_scr, h_scr = refs
    lng_ref, lnb_ref = gmlp_in[-2:]
    s = pl.program_id(0)
    n_tiles = pl.num_programs(0) - 1

    @pl.when(s == 0)
    def _():
        h_scr[...] = jnp.zeros_like(h_scr)

    @pl.when(s < n_tiles)
    def _():
        decode = _decode_state_stages(*decode_in, *decode_out, decode_seq) if decode_seq else None
        _gmlp_tile_step(*gmlp_in, y_ref, vn_ref, v_scr, og_scr, h_scr, decode)

    @pl.when(s == n_tiles)
    def _():
        y_ref[...] = _layer_norm(h_scr[...], lng_ref[...], lnb_ref[...])


def _gmlp_tile_step(x_ref, w_ref, vg_ref, vb_ref, ws_ref, bs_ref, wo_ref, lng_ref, lnb_ref,
                    y_ref, vn_ref, v_scr, og_scr, h_scr, decode):
    y_ref[...] = _layer_norm(h_scr[...], lng_ref[...], lnb_ref[...])

    n_chunks = x_ref.shape[0] // CHUNK_B
    x = x_ref[...]
    xb = x.astype(BF16)

    half = x_ref.shape[0] // 2
    for gp in range(N_GROUPS):
        for r in range(2):
            rows = slice(r * half, (r + 1) * half)
            v_scr[gp, rows] = _gelu_tanh(_dot(xb[rows], _group_cols(w_ref, N_GROUPS + gp)))
        if decode is not None:
            decode[0](gp)

    s1 = jnp.sum(v_scr[0], axis=-1, keepdims=True)
    for gp in range(1, N_GROUPS):
        s1 = s1 + jnp.sum(v_scr[gp], axis=-1, keepdims=True)
    mu = s1 * (1.0 / E)
    s2 = jnp.zeros_like(mu)
    for gp in range(N_GROUPS):
        d = v_scr[gp] - mu
        s2 = s2 + jnp.sum(d * d, axis=-1, keepdims=True)
    rstd = lax.rsqrt(s2 * (1.0 / E) + LN_EPS)

    row = lax.broadcasted_iota(jnp.int32, (CHUNK_B, CHUNK_B), 0)
    col = lax.broadcasted_iota(jnp.int32, (CHUNK_B, CHUNK_B), 1)
    causal = col <= row
    n_mix_groups = bs_ref.shape[0]
    bias_cols = jnp.concatenate(
        [bs_ref[...], jnp.zeros((CHUNK_B - n_mix_groups, CHUNK_B), F32)], axis=0).T

    def stage_gate(gp):
        return _dot(xb, _group_cols(w_ref, gp)), _dot(xb, _group_cols(w_ref, 2 * N_GROUPS + gp))

    def stage_mix(gp, gate):
        u_pre, z = gate
        vn = (v_scr[gp] - mu) * rstd * vg_ref[gp] + vb_ref[gp]
        if vn_ref is not None:
            vn_ref[gp] = vn
        vnb = vn.astype(BF16)
        cols = []
        for i in range(HEADS_PER_GROUP):
            ls = slice(i * HEAD, (i + 1) * HEAD)
            g = HEADS_PER_GROUP * gp + i
            wc = jnp.where(causal, ws_ref[g], 0.0).astype(BF16)
            bias = bias_cols[:, g:g + 1]
            cols.append(jnp.concatenate(
                [_dot(wc, vnb[c * CHUNK_B:(c + 1) * CHUNK_B, ls]) + bias for c in range(n_chunks)], axis=0))
        mixed = jnp.concatenate(cols, axis=1)
        og_scr[gp] = (_gelu_tanh(u_pre) * mixed * _silu(z)).astype(BF16)

    def out_proj(groups):
        y = _dot(og_scr[groups[0]], wo_ref[groups[0]])
        for gp in groups[1:]:
            y = y + _dot(og_scr[gp], wo_ref[gp])
        return y

    y_parts = []
    gates = {0: stage_gate(0)}
    for gp in range(N_GROUPS):
        if gp + 1 < N_GROUPS:
            gates[gp + 1] = stage_gate(gp + 1)
        stage_mix(gp, gates.pop(gp))
        if decode is not None:
            decode[1](gp)
        if gp == N_GROUPS - 2:
            y_parts.append(out_proj(list(range(0, N_GROUPS - 3))))
    y_parts.append(out_proj([N_GROUPS - 3, N_GROUPS - 2]))
    y_parts.append(out_proj([N_GROUPS - 1]))
    y = y_parts[0] + y_parts[1] + y_parts[2]
    h_scr[...] = ALPHA * x + y


def _gmlp(x, w3, vg3, vb3, ws, bs_rows, wo3, lng, lnb, emit_v, decode=None, decode_seq=0):
    rows = x.shape[0]
    tile = min(GMLP_TILE, rows)
    n_tiles = rows // tile
    tile_in = lambda s: jnp.minimum(s, n_tiles - 1)
    tile_out = lambda s: jnp.maximum(s - 1, 0)
    in_specs = [
        pl.BlockSpec((tile, D_MODEL), lambda s: (tile_in(s), 0)),
        _resident(w3.shape), _resident(vg3.shape), _resident(vb3.shape), _resident(ws.shape),
        _resident(bs_rows.shape), _resident(wo3.shape), _resident(lng.shape), _resident(lnb.shape),
    ]
    operands = [x, w3, vg3, vb3, ws, bs_rows, wo3, lng, lnb]
    assert len(operands) == N_GMLP_INPUTS
    out_specs = [pl.BlockSpec((tile, D_MODEL), lambda s: (tile_out(s), 0))]
    out_shape = [jax.ShapeDtypeStruct(x.shape, F32)]
    if emit_v:
        out_specs.append(pl.BlockSpec((N_GROUPS, tile, LANE_GROUP), lambda s: (0, tile_in(s), 0)))
        out_shape.append(jax.ShapeDtypeStruct((N_GROUPS, rows, LANE_GROUP), F32))
    if decode is not None:
        assert len(decode) == N_DECODE_INPUTS
        state = decode[-1]
        n_seq = state.shape[0]
        seqs_per_step = n_seq // n_tiles
        rows_per_step = seqs_per_step * decode_seq
        assert n_seq % n_tiles == 0 and rows_per_step % SUBLANES == 0 and SUBLANES % decode_seq == 0
        tok_spec = pl.BlockSpec((N_GROUPS, rows_per_step, LANE_GROUP), lambda s: (0, tile_in(s), 0))
        st_spec = pl.BlockSpec((seqs_per_step, N_HEADS, HEAD, HEAD), lambda s: (tile_in(s), 0, 0, 0))
        in_specs += [tok_spec] * (N_DECODE_INPUTS - 1) + [st_spec]
        operands += list(decode)
        out_specs += [tok_spec, st_spec]
        out_shape += [jax.ShapeDtypeStruct(decode[0].shape, F32), jax.ShapeDtypeStruct(state.shape, F32)]
    return pl.pallas_call(
        functools.partial(_gmlp_kernel, emit_v=emit_v, decode_seq=decode_seq if decode is not None else 0),
        grid=(n_tiles + 1,),
        in_specs=in_specs,
        out_specs=out_specs,
        out_shape=out_shape,
        scratch_shapes=[
            pltpu.VMEM((N_GROUPS, tile, LANE_GROUP), F32),
            pltpu.VMEM((N_GROUPS, tile, LANE_GROUP), BF16),
            pltpu.VMEM((tile, D_MODEL), F32),
        ],
        compiler_params=pltpu.CompilerParams(
            dimension_semantics=("arbitrary",), vmem_limit_bytes=VMEM_LIMIT_BYTES),
        name="gmlp_emit_v" if emit_v else "gmlp",
    )(*operands)


def _lane_groups_of_rows(w):
    return w.reshape(w.shape[0] // LANE_GROUP, LANE_GROUP, w.shape[1]).astype(BF16)


def _lane_groups_of_vector(v):
    return v.reshape(v.shape[0], N_GROUPS, LANE_GROUP).transpose(1, 0, 2)


def kernel(x_prompt, x_sample, state_hgrn, w_in_a, lb_logits_a, gnorm_a, w_out_a, w_in_b, lnv_g_b,
           lnv_b_b, w_s_b, b_s_b, w_out_b, ln_g, ln_b):
    bsz, seq, _ = x_prompt.shape
    n_seq, dec_seq, _ = x_sample.shape

    w3a = w_in_a[0].astype(BF16)
    wo3a = _lane_groups_of_rows(w_out_a[0])
    lbl3 = _lane_groups_of_vector(lb_logits_a)
    gn3 = _lane_groups_of_vector(gnorm_a[0:1])
    lng0, lnb0 = ln_g[0:1], ln_b[0:1]

    hp, sp = _hgrn_prompt(x_prompt, w3a, lbl3, gn3, wo3a, lng0, lnb0, layer=0)

    xs = x_sample.reshape(n_seq * dec_seq, D_MODEL)
    decode_gates = _hgrn_sample_gates(xs, w3a, lbl3, gn3, layer=0, dec_seq=dec_seq)

    w3b = _group_major(w_in_b[0])
    wo3b = _lane_groups_of_rows(w_out_b[0])
    vg3 = _lane_groups_of_vector(lnv_g_b[0:1])
    vb3 = _lane_groups_of_vector(lnv_b_b[0:1])
    lng1, lnb1 = ln_g[1:2], ln_b[1:2]
    ws = w_s_b[0]
    bs = b_s_b[0]

    yp, og, ss = _gmlp(hp.reshape(bsz * seq, D_MODEL), w3b, vg3, vb3, ws, bs, wo3b, lng1, lnb1,
                       emit_v=False, decode=(*decode_gates, state_hgrn[0]), decode_seq=dec_seq)
    yp = yp.reshape(bsz, seq, D_MODEL)
    hs = _out_proj_ln(og, xs, wo3a, lng0, lnb0)

    reps = CHUNK_B // dec_seq
    ws_dec = jnp.einsum('ab,gts->gatbs', jnp.eye(reps, dtype=F32), ws[:, :dec_seq, :dec_seq]).reshape(
        ws.shape[0], CHUNK_B, CHUNK_B)
    bs_dec = jnp.tile(bs[:, :dec_seq], (1, reps))
    ys, vn = _gmlp(hs.reshape(n_seq * dec_seq, D_MODEL), w3b, vg3, vb3, ws_dec, bs_dec, wo3b,
                   lng1, lnb1, emit_v=True)
    ys = ys.reshape(n_seq, dec_seq, D_MODEL)
    vs = vn.transpose(1, 0, 2).reshape(n_seq, dec_seq, E)

    return (yp, ys, sp[None], ss[None], vs[None])
```

```python
import functools
import math

import jax
import jax.numpy as jnp
from jax import lax
from jax.experimental import pallas as pl
from jax.experimental.pallas import tpu as pltpu

F32 = jnp.float32
BF16 = jnp.bfloat16

D_MODEL = 1024
E = 2048
HEAD = 128
N_HEADS = E // HEAD
LANE_GROUP = 256
N_GROUPS = E // LANE_GROUP
HEADS_PER_GROUP = LANE_GROUP // HEAD
CHUNK_A = 64
CHUNK_B = 128
ROW_TILE = 256
HGRN_TILE = 512
HGRN_BLOCK = 2 * CHUNK_A
GMLP_TILE = 512
SUBLANES = 8
DEPTH = 2
ALPHA = (2 * DEPTH) ** 0.25
LN_EPS = 1e-5
VMEM_LIMIT_BYTES = 60000 * 1024


def _dot(a, b):
    return jnp.dot(a, b, preferred_element_type=F32)


def _group_cols(w_ref, group):
    if len(w_ref.shape) == 3:
        return w_ref[group]
    return w_ref[:, group * LANE_GROUP:(group + 1) * LANE_GROUP]


STAGE_ROWS = 512
STAGE_SLOTS = 8


def _stage_weight(w_hbm, stage, sems, store):
    n_row_blocks = w_hbm.shape[0] // STAGE_ROWS
    n_col_blocks = w_hbm.shape[1] // LANE_GROUP
    chunks = [(r, c) for c in range(n_col_blocks) for r in range(n_row_blocks)]

    def copy(i):
        r, c = chunks[i]
        slot = i % STAGE_SLOTS
        return pltpu.make_async_copy(
            w_hbm.at[pl.ds(r * STAGE_ROWS, STAGE_ROWS), pl.ds(c * LANE_GROUP, LANE_GROUP)],
            stage.at[slot], sems.at[slot])

    for i in range(min(STAGE_SLOTS, len(chunks))):
        copy(i).start()
    for i, (r, c) in enumerate(chunks):
        copy(i).wait()
        store(r, c, stage[i % STAGE_SLOTS].astype(BF16))
        if i + STAGE_SLOTS < len(chunks):
            copy(i + STAGE_SLOTS).start()


def _stage_projection_weights(w_in_hbm, w_out_hbm, w_scr, wo_scr, stage, sems):
    def store_in(r, c, chunk):
        w_scr[c, r * STAGE_ROWS:(r + 1) * STAGE_ROWS, :] = chunk

    def store_out(r, c, chunk):
        for k in range(STAGE_ROWS // LANE_GROUP):
            wo_scr[r * (STAGE_ROWS // LANE_GROUP) + k, :, c * LANE_GROUP:(c + 1) * LANE_GROUP] = (
                chunk[k * LANE_GROUP:(k + 1) * LANE_GROUP])

    _stage_weight(w_in_hbm, stage, sems, store_in)
    _stage_weight(w_out_hbm, stage, sems, store_out)


def _dot_nt(a, b):
    return lax.dot_general(a, b, (((1,), (1,)), ((), ())), preferred_element_type=F32)


def _dot_tn(a, b):
    return lax.dot_general(a, b, (((0,), (0,)), ((), ())), preferred_element_type=F32)


def _silu(x):
    return x / (1.0 + jnp.exp(-x))


def _gelu_tanh(x):
    cdf = 0.5 * (1.0 + jnp.tanh(math.sqrt(2.0 / math.pi) * (x + 0.044715 * (x * x * x))))
    return x * cdf


def _layer_norm(x, g, b):
    mu = jnp.mean(x, axis=-1, keepdims=True)
    d = x - mu
    var = jnp.mean(d * d, axis=-1, keepdims=True)
    return d * lax.rsqrt(var + LN_EPS) * g + b


def _block_masks(n, block):
    shift = block.bit_length() - 1
    row = lax.broadcasted_iota(jnp.int32, (n, n), 0)
    col = lax.broadcasted_iota(jnp.int32, (n, n), 1)
    same = jnp.right_shift(row, shift) == jnp.right_shift(col, shift)
    causal = jnp.logical_and(same, col <= row)
    return same, causal


def _split_dot(m, x):
    hi = x.astype(BF16)
    lo = (x - hi.astype(F32)).astype(BF16)
    return _dot(jnp.concatenate([m, m], axis=1), jnp.concatenate([hi, lo], axis=0))


def _forget_lower_bound(logits, layer):
    m = jnp.max(logits, axis=0, keepdims=True)
    e = jnp.exp(logits - m)
    den = jnp.sum(e, axis=0, keepdims=True)
    num = jnp.sum(e[: layer + 1], axis=0, keepdims=True)
    return num / den


def _hgrn_gates(xb, w_q, w_f, lb, tri, valid):
    qp = _dot(xb, w_q)
    q = _silu(qp)
    z = _dot(xb, w_f)
    c1 = 1.0 - lb
    cs = c1 / (1.0 + jnp.exp(-z))
    f = lb + cs
    k = c1 - cs
    lf = jnp.log(f)
    if valid is not None:
        lf = jnp.where(valid, lf, 0.0)
        k = jnp.where(valid, k, 0.0)
    cum = _split_dot(tri, lf)
    return q * jnp.exp(cum), k * jnp.exp(-cum), cum, lf


def _rms_gate(o, gate):
    ms = jnp.mean(o * o, axis=-1, keepdims=True)
    return o * lax.rsqrt(ms + LN_EPS) * gate


def _hgrn_prompt_kernel(x_ref, w_hbm, lbl_ref, gn_ref, wo_hbm, lng_ref, lnb_ref,
                        y_ref, s_ref, st_scr, og_scr, h_scr, w_ref, wo_ref, stage, sems,
                        *, layer, tiles_per_row):
    s = pl.program_id(0)
    n_tiles = pl.num_programs(0) - 1

    @pl.when(s == 0)
    def _():
        _stage_projection_weights(w_hbm, wo_hbm, w_ref, wo_ref, stage, sems)
        h_scr[...] = jnp.zeros_like(h_scr)

    @pl.when(s < n_tiles)
    def _():
        _hgrn_tile_step(lax.rem(s, tiles_per_row), tiles_per_row, x_ref, w_ref, lbl_ref, gn_ref, wo_ref,
                        lng_ref, lnb_ref, y_ref, s_ref, st_scr, og_scr, h_scr, layer=layer)

    @pl.when(s == n_tiles)
    def _():
        y_ref[0] = _layer_norm(h_scr[...], lng_ref[...], lnb_ref[...])


def _hgrn_tile_step(j, tiles_per_row, x_ref, w_ref, lbl_ref, gn_ref, wo_ref, lng_ref, lnb_ref,
                    y_ref, s_ref, st_scr, og_scr, h_scr, *, layer):
    tile = x_ref.shape[1]
    n_blocks = tile // HGRN_BLOCK
    chunks_per_block = HGRN_BLOCK // CHUNK_A

    @pl.when(j == 0)
    def _():
        st_scr[...] = jnp.zeros_like(st_scr)

    y_ref[0] = _layer_norm(h_scr[...], lng_ref[...], lnb_ref[...])

    x = x_ref[0]
    xb = x.astype(BF16)
    _, causal = _block_masks(HGRN_BLOCK, CHUNK_A)
    tri = jnp.where(causal, 1.0, 0.0).astype(BF16)

    def block(p):
        return slice(p * HGRN_BLOCK, (p + 1) * HGRN_BLOCK)

    def stage_project(hp):
        z = _dot(xb, _group_cols(w_ref, N_GROUPS + hp))
        qp = _dot(xb, _group_cols(w_ref, hp))
        v = _dot(xb, _group_cols(w_ref, 2 * N_GROUPS + hp))
        g = _dot(xb, _group_cols(w_ref, 3 * N_GROUPS + hp))
        lb = _forget_lower_bound(lbl_ref[hp], layer)
        c1 = 1.0 - lb
        cs = c1 / (1.0 + jnp.exp(-z))
        k = c1 - cs
        lf = jnp.log(lb + cs)
        cum = jnp.concatenate([_split_dot(tri, lf[block(p)]) for p in range(n_blocks)], axis=0)
        return _silu(qp), k, cum, v.astype(BF16), _silu(g) * gn_ref[hp]

    def stage_scores(hp, a):
        q, k, cum, vb, gate = a
        qdb = (q * jnp.exp(cum)).astype(BF16)
        ki = k * jnp.exp(-cum)
        kib = ki.astype(BF16)
        n_chunks = tile // CHUNK_A
        dl_rows = jnp.exp(jnp.concatenate(
            [cum[(c + 1) * CHUNK_A - 1:(c + 1) * CHUNK_A] for c in range(n_chunks)]
            + [jnp.zeros((HEAD - n_chunks, LANE_GROUP), F32)], axis=0))
        dl_cols = dl_rows.T
        heads = []
        for i in range(HEADS_PER_GROUP):
            ls = slice(i * HEAD, (i + 1) * HEAD)
            kit = ki[:, ls].T.astype(BF16)
            scores = [_dot(qdb[block(p), ls], kit[:, block(p)]) for p in range(n_blocks)]
            dls, us = [], []
            for c in range(n_chunks):
                lo, hi = c * CHUNK_A, (c + 1) * CHUNK_A
                ke = (ki[lo:hi, ls] * dl_rows[c:c + 1, ls]).astype(BF16)
                dls.append(dl_cols[ls, c:c + 1])
                us.append(_dot_tn(ke, vb[lo:hi, ls]))
            heads.append((qdb[:, ls], vb[:, ls], scores, dls, us))
        return heads, gate

    def stage_output(hp, b):
        heads, gate = b
        ogs = []
        for i, (qdb, vb, scores, dls, us) in enumerate(heads):
            st = st_scr[HEADS_PER_GROUP * hp + i]
            os = []
            for p in range(n_blocks):
                o_intra = _dot(jnp.where(causal, scores[p], 0.0).astype(BF16), vb[block(p)])
                for cc in range(chunks_per_block):
                    c = p * chunks_per_block + cc
                    lo, hi = c * CHUNK_A, (c + 1) * CHUNK_A
                    os.append(o_intra[cc * CHUNK_A:(cc + 1) * CHUNK_A] + _dot(qdb[lo:hi], st.astype(BF16)))
                    st = st * dls[c] + us[c]
            st_scr[HEADS_PER_GROUP * hp + i] = st
            o = jnp.concatenate(os, axis=0)
            ogs.append(_rms_gate(o, gate[:, i * HEAD:(i + 1) * HEAD]))
        og_scr[hp] = jnp.concatenate(ogs, axis=1).astype(BF16)

    def out_proj(groups):
        y = _dot(og_scr[groups[0]], wo_ref[groups[0]])
        for hp in groups[1:]:
            y = y + _dot(og_scr[hp], wo_ref[hp])
        return y

    y_parts = []
    projected, scored = {}, {}
    for t in range(N_GROUPS + 2):
        if t < N_GROUPS:
            projected[t] = stage_project(t)
        if t == N_GROUPS:
            y_parts.append(out_proj(list(range(0, N_GROUPS - 4))))
        if t == N_GROUPS + 1:
            y_parts.append(out_proj([N_GROUPS - 4, N_GROUPS - 3]))
        if 0 <= t - 1 < N_GROUPS:
            scored[t - 1] = stage_scores(t - 1, projected.pop(t - 1))
        if 0 <= t - 2 < N_GROUPS:
            stage_output(t - 2, scored.pop(t - 2))
    y_parts.append(out_proj([N_GROUPS - 2, N_GROUPS - 1]))
    y = y_parts[0] + y_parts[1] + y_parts[2]
    h_scr[...] = ALPHA * x + y

    @pl.when(j == tiles_per_row - 1)
    def _():
        s_ref[0] = st_scr[...]


def _resident(shape):
    nd = len(shape)
    return pl.BlockSpec(shape, lambda *_: (0,) * nd, pipeline_mode=pl.Buffered(1))


def _weight_scratch(w_in, w_out):
    k, n = w_in.shape
    return [
        pltpu.VMEM((n // LANE_GROUP, k, LANE_GROUP), BF16),
        pltpu.VMEM((w_out.shape[0] // LANE_GROUP, LANE_GROUP, w_out.shape[1]), BF16),
        pltpu.VMEM((STAGE_SLOTS, STAGE_ROWS, LANE_GROUP), F32),
        pltpu.SemaphoreType.DMA((STAGE_SLOTS,)),
    ]


def _hgrn_prompt(x, w_in, lbl3, gn3, w_out, lng, lnb, layer):
    bsz, seq, _ = x.shape
    tiles_per_row = seq // HGRN_TILE
    n_tiles = bsz * tiles_per_row
    hbm = pl.BlockSpec(memory_space=pl.ANY)

    def tile_in(s):
        t = jnp.minimum(s, n_tiles - 1)
        return t // tiles_per_row, t % tiles_per_row

    def tile_out(s):
        t = jnp.maximum(s - 1, 0)
        return t // tiles_per_row, t % tiles_per_row

    return pl.pallas_call(
        functools.partial(_hgrn_prompt_kernel, layer=layer, tiles_per_row=tiles_per_row),
        grid=(n_tiles + 1,),
        in_specs=[
            pl.BlockSpec((1, HGRN_TILE, D_MODEL), lambda s: (*tile_in(s), 0)),
            hbm, _resident(lbl3.shape), _resident(gn3.shape), hbm,
            _resident(lng.shape), _resident(lnb.shape),
        ],
        out_specs=[
            pl.BlockSpec((1, HGRN_TILE, D_MODEL), lambda s: (*tile_out(s), 0)),
            pl.BlockSpec((1, N_HEADS, HEAD, HEAD), lambda s: (tile_in(s)[0], 0, 0, 0)),
        ],
        out_shape=[
            jax.ShapeDtypeStruct(x.shape, F32),
            jax.ShapeDtypeStruct((bsz, N_HEADS, HEAD, HEAD), F32),
        ],
        scratch_shapes=[
            pltpu.VMEM((N_HEADS, HEAD, HEAD), F32),
            pltpu.VMEM((N_GROUPS, HGRN_TILE, LANE_GROUP), BF16),
            pltpu.VMEM((HGRN_TILE, D_MODEL), F32),
            *_weight_scratch(w_in, w_out),
        ],
        compiler_params=pltpu.CompilerParams(
            dimension_semantics=("arbitrary",), vmem_limit_bytes=VMEM_LIMIT_BYTES),
        name="hgrn_prompt",
    )(x, w_in, lbl3, gn3, w_out, lng, lnb)


def _hgrn_sample_gates_kernel(x_ref, wq_ref, wf_ref, wi_ref, wg_ref, lbl_ref, gn_ref,
                              qd_ref, ki_ref, ke_ref, v_ref, gate_ref, dl_ref, *, layer, dec_seq):
    rows = x_ref.shape[0]
    xb = x_ref[...].astype(BF16)
    same, causal = _block_masks(ROW_TILE, dec_seq)
    tri = jnp.where(causal, 1.0, 0.0).astype(BF16)
    blk = jnp.where(same, 1.0, 0.0).astype(BF16)
    lb = _forget_lower_bound(lbl_ref[0], layer)

    qp = _dot(xb, wq_ref[...].astype(BF16))
    q = _silu(qp)
    z = _dot(xb, wf_ref[...].astype(BF16))
    c1 = 1.0 - lb
    cs = c1 / (1.0 + jnp.exp(-z))
    k = c1 - cs
    lf = jnp.log(lb + cs)
    for r in range(rows // ROW_TILE):
        rs = slice(r * ROW_TILE, (r + 1) * ROW_TILE)
        cum = _split_dot(tri, lf[rs])
        cum_last = _split_dot(blk, lf[rs])
        ki = k[rs] * jnp.exp(-cum)
        dl = jnp.exp(cum_last)
        qd_ref[0, rs] = q[rs] * jnp.exp(cum)
        ki_ref[0, rs] = ki
        ke_ref[0, rs] = ki * dl
        dl_ref[0, rs] = dl
    v_ref[0] = _dot(xb, wi_ref[...].astype(BF16))
    gate_ref[0] = _silu(_dot(xb, wg_ref[...].astype(BF16))) * gn_ref[0]


def _hgrn_sample_gates(x_rows, w3, lbl3, gn3, layer, dec_seq):
    rows = x_rows.shape[0]
    assert rows % ROW_TILE == 0 and SUBLANES % dec_seq == 0
    w_spec = lambda part: pl.BlockSpec((D_MODEL, LANE_GROUP), lambda g: (0, part * N_GROUPS + g))
    out_spec = pl.BlockSpec((1, rows, LANE_GROUP), lambda g: (g, 0, 0))
    out_shape = jax.ShapeDtypeStruct((N_GROUPS, rows, LANE_GROUP), F32)
    return pl.pallas_call(
        functools.partial(_hgrn_sample_gates_kernel, layer=layer, dec_seq=dec_seq),
        grid=(N_GROUPS,),
        in_specs=[
            pl.BlockSpec((rows, D_MODEL), lambda g: (0, 0)),
            w_spec(0), w_spec(1), w_spec(2), w_spec(3),
            pl.BlockSpec((1,) + lbl3.shape[1:], lambda g: (g, 0, 0)),
            pl.BlockSpec((1, 1, LANE_GROUP), lambda g: (g, 0, 0)),
        ],
        out_specs=[out_spec] * 6,
        out_shape=[out_shape] * 6,
        compiler_params=pltpu.CompilerParams(
            dimension_semantics=("arbitrary",), vmem_limit_bytes=VMEM_LIMIT_BYTES),
        name="hgrn_sample_gates",
    )(x_rows, w3, w3, w3, w3, lbl3, gn3)


def _decode_state_stages(qd_ref, ki_ref, ke_ref, v_ref, gate_ref, dl_ref, s_ref, og_ref, so_ref, dec_seq):
    seqs_per_tile = SUBLANES // dec_seq
    n_tiles = qd_ref.shape[1] // SUBLANES
    shift = dec_seq.bit_length() - 1
    trow = jnp.right_shift(lax.broadcasted_iota(jnp.int32, (SUBLANES, HEAD), 0), shift)
    in_seq = [trow == s for s in range(seqs_per_tile)]
    pad_rows = jnp.zeros((HEAD - N_HEADS, HEAD), F32)
    tiles = [slice(m * SUBLANES, (m + 1) * SUBLANES) for m in range(n_tiles)]
    dcols, pending = {}, {}
    lane_head = lax.broadcasted_iota(jnp.int32, (SUBLANES, LANE_GROUP), 1) // HEAD
    srow = lax.broadcasted_iota(jnp.int32, (SUBLANES, HEADS_PER_GROUP * SUBLANES), 0)
    scol = jnp.bitwise_and(lax.broadcasted_iota(jnp.int32, (SUBLANES, HEADS_PER_GROUP * SUBLANES), 1),
                           SUBLANES - 1)
    causal = jnp.logical_and(jnp.right_shift(srow, shift) == jnp.right_shift(scol, shift), scol <= srow)

    def per_head_rows(t):
        return jnp.concatenate([jnp.where(lane_head == i, t, 0.0) for i in range(HEADS_PER_GROUP)], axis=0)

    def per_seq_lanes(t):
        return jnp.concatenate([jnp.where(in_seq[s], t, 0.0) for s in range(seqs_per_tile)], axis=1)

    def decay_columns(m):
        dl = [dl_ref[hp, tiles[m], :] for hp in range(N_GROUPS)]
        for s in range(seqs_per_tile):
            r = s * dec_seq
            rows = [dl[h // HEADS_PER_GROUP][r:r + 1, (h % HEADS_PER_GROUP) * HEAD:(h % HEADS_PER_GROUP + 1) * HEAD]
                    for h in range(N_HEADS)]
            dcols[(m, s)] = jnp.concatenate(rows + [pad_rows], axis=0).T

    def start(hp):
        for m in range(n_tiles):
            if hp == 0:
                decay_columns(m)
            rs = tiles[m]
            qd = qd_ref[hp, rs, :]
            ke = ke_ref[hp, rs, :]
            v = v_ref[hp, rs, :]
            qdb = qd.astype(BF16)
            scores = _dot_nt(qdb, per_head_rows(ki_ref[hp, rs, :]).astype(BF16))
            o_inter = []
            for i in range(HEADS_PER_GROUP):
                ls = slice(i * HEAD, (i + 1) * HEAD)
                h = HEADS_PER_GROUP * hp + i
                sts = [s_ref[m * seqs_per_tile + s, h] for s in range(seqs_per_tile)]
                st_rows = jnp.concatenate([st.astype(BF16) for st in sts], axis=0)
                o_inter.append(_dot(per_seq_lanes(qd[:, ls]).astype(BF16), st_rows))
                u = _dot_tn(per_seq_lanes(ke[:, ls]).astype(BF16), v[:, ls].astype(BF16))
                for s in range(seqs_per_tile):
                    so_ref[m * seqs_per_tile + s, h] = (dcols[(m, s)][:, h:h + 1] * sts[s]
                                                       + u[s * HEAD:(s + 1) * HEAD])
            pending[(hp, m)] = (scores, jnp.concatenate(o_inter, axis=1), per_head_rows(v).astype(BF16))

    def finish(hp):
        for m in range(n_tiles):
            scores, o_inter, v_rows = pending.pop((hp, m))
            o = _dot(jnp.where(causal, scores, 0.0).astype(BF16), v_rows) + o_inter
            gate = gate_ref[hp, tiles[m], :]
            og_ref[hp, tiles[m], :] = jnp.concatenate(
                [_rms_gate(o[:, i * HEAD:(i + 1) * HEAD], gate[:, i * HEAD:(i + 1) * HEAD])
                 for i in range(HEADS_PER_GROUP)], axis=1)

    return start, finish


def _out_proj_ln_kernel(og_ref, x_ref, wo_ref, lng_ref, lnb_ref, y_ref):
    y = _dot(og_ref[0].astype(BF16), wo_ref[0].astype(BF16))
    for hp in range(1, N_GROUPS):
        y = y + _dot(og_ref[hp].astype(BF16), wo_ref[hp].astype(BF16))
    y_ref[...] = _layer_norm(ALPHA * x_ref[...] + y, lng_ref[...], lnb_ref[...])


def _out_proj_ln(og, x, wo3, lng, lnb):
    rows = x.shape[0]
    return pl.pallas_call(
        _out_proj_ln_kernel,
        grid=(rows // ROW_TILE,),
        in_specs=[
            pl.BlockSpec((N_GROUPS, ROW_TILE, LANE_GROUP), lambda r: (0, r, 0)),
            pl.BlockSpec((ROW_TILE, D_MODEL), lambda r: (r, 0)),
            _resident(wo3.shape), _resident(lng.shape), _resident(lnb.shape),
        ],
        out_specs=pl.BlockSpec((ROW_TILE, D_MODEL), lambda r: (r, 0)),
        out_shape=jax.ShapeDtypeStruct(x.shape, F32),
        compiler_params=pltpu.CompilerParams(
            dimension_semantics=("arbitrary",), vmem_limit_bytes=VMEM_LIMIT_BYTES),
        name="out_proj_ln",
    )(og, x, wo3, lng, lnb)


N_GMLP_INPUTS = 9
N_DECODE_INPUTS = 7


def _gmlp_kernel(*refs, emit_v, decode_seq):
    refs = list(refs)
    gmlp_in = refs[:N_GMLP_INPUTS]
    del refs[:N_GMLP_INPUTS]
    decode_in = []
    if decode_seq:
        decode_in = refs[:N_DECODE_INPUTS]
        del refs[:N_DECODE_INPUTS]
    y_ref = refs.pop(0)
    vn_ref = refs.pop(0) if emit_v else None
    decode_out = [refs.pop(0), refs.pop(0)] if decode_seq else []
    v_scr, og_scr, h_scr, w_ref, wo_ref, sems = refs
    x_ref, w_hbm, vg_ref, vb_ref, ws_ref, bs_ref, wo_hbm, lng_ref, lnb_ref = gmlp_in
    s = pl.program_id(0)
    n_tiles = pl.num_programs(0) - 1

    @pl.when(s == 0)
    def _():
        _stage_projection_weights(w_hbm, wo_hbm, w_ref, wo_ref, v_scr, sems)
        h_scr[...] = jnp.zeros_like(h_scr)

    @pl.when(s < n_tiles)
    def _():
        decode = _decode_state_stages(*decode_in, *decode_out, decode_seq) if decode_seq else None
        _gmlp_tile_step(x_ref, w_ref, vg_ref, vb_ref, ws_ref, bs_ref, wo_ref, lng_ref, lnb_ref,
                        y_ref, vn_ref, v_scr, og_scr, h_scr, decode)

    @pl.when(s == n_tiles)
    def _():
        y_ref[...] = _layer_norm(h_scr[...], lng_ref[...], lnb_ref[...])


def _gmlp_tile_step(x_ref, w_ref, vg_ref, vb_ref, ws_ref, bs_ref, wo_ref, lng_ref, lnb_ref,
                    y_ref, vn_ref, v_scr, og_scr, h_scr, decode):
    y_ref[...] = _layer_norm(h_scr[...], lng_ref[...], lnb_ref[...])

    n_chunks = x_ref.shape[0] // CHUNK_B
    x = x_ref[...]
    xb = x.astype(BF16)

    half = x_ref.shape[0] // 2
    for gp in range(N_GROUPS):
        for r in range(2):
            rows = slice(r * half, (r + 1) * half)
            v_scr[gp, rows] = _gelu_tanh(_dot(xb[rows], _group_cols(w_ref, N_GROUPS + gp)))
        if decode is not None:
            decode[0](gp)

    s1 = jnp.sum(v_scr[0], axis=-1, keepdims=True)
    for gp in range(1, N_GROUPS):
        s1 = s1 + jnp.sum(v_scr[gp], axis=-1, keepdims=True)
    mu = s1 * (1.0 / E)
    s2 = jnp.zeros_like(mu)
    for gp in range(N_GROUPS):
        d = v_scr[gp] - mu
        s2 = s2 + jnp.sum(d * d, axis=-1, keepdims=True)
    rstd = lax.rsqrt(s2 * (1.0 / E) + LN_EPS)

    row = lax.broadcasted_iota(jnp.int32, (CHUNK_B, CHUNK_B), 0)
    col = lax.broadcasted_iota(jnp.int32, (CHUNK_B, CHUNK_B), 1)
    causal = col <= row
    n_mix_groups = bs_ref.shape[0]
    bias_cols = jnp.concatenate(
        [bs_ref[...], jnp.zeros((CHUNK_B - n_mix_groups, CHUNK_B), F32)], axis=0).T

    def stage_gate(gp):
        return _dot(xb, _group_cols(w_ref, gp)), _dot(xb, _group_cols(w_ref, 2 * N_GROUPS + gp))

    def stage_mix(gp, gate):
        u_pre, z = gate
        vn = (v_scr[gp] - mu) * rstd * vg_ref[gp] + vb_ref[gp]
        if vn_ref is not None:
            vn_ref[gp] = vn
        vnb = vn.astype(BF16)
        cols = []
        for i in range(HEADS_PER_GROUP):
            ls = slice(i * HEAD, (i + 1) * HEAD)
            g = HEADS_PER_GROUP * gp + i
            wc = jnp.where(causal, ws_ref[g], 0.0).astype(BF16)
            bias = bias_cols[:, g:g + 1]
            cols.append(jnp.concatenate(
                [_dot(wc, vnb[c * CHUNK_B:(c + 1) * CHUNK_B, ls]) + bias for c in range(n_chunks)], axis=0))
        mixed = jnp.concatenate(cols, axis=1)
        og_scr[gp] = (_gelu_tanh(u_pre) * mixed * _silu(z)).astype(BF16)

    def out_proj(groups):
        y = _dot(og_scr[groups[0]], wo_ref[groups[0]])
        for gp in groups[1:]:
            y = y + _dot(og_scr[gp], wo_ref[gp])
        return y

    y_parts = []
    gates = {0: stage_gate(0)}
    for gp in range(N_GROUPS):
        if gp + 1 < N_GROUPS:
            gates[gp + 1] = stage_gate(gp + 1)
        stage_mix(gp, gates.pop(gp))
        if decode is not None:
            decode[1](gp)
        if gp == N_GROUPS - 2:
            y_parts.append(out_proj(list(range(0, N_GROUPS - 3))))
    y_parts.append(out_proj([N_GROUPS - 3, N_GROUPS - 2]))
    y_parts.append(out_proj([N_GROUPS - 1]))
    y = y_parts[0] + y_parts[1] + y_parts[2]
    h_scr[...] = ALPHA * x + y


def _gmlp(x, w_in, vg3, vb3, ws, bs_rows, w_out, lng, lnb, emit_v, decode=None, decode_seq=0):
    rows = x.shape[0]
    tile = min(GMLP_TILE, rows)
    assert tile == STAGE_ROWS and N_GROUPS == STAGE_SLOTS
    n_tiles = rows // tile
    tile_in = lambda s: jnp.minimum(s, n_tiles - 1)
    tile_out = lambda s: jnp.maximum(s - 1, 0)
    hbm = pl.BlockSpec(memory_space=pl.ANY)
    in_specs = [
        pl.BlockSpec((tile, D_MODEL), lambda s: (tile_in(s), 0)),
        hbm, _resident(vg3.shape), _resident(vb3.shape), _resident(ws.shape),
        _resident(bs_rows.shape), hbm, _resident(lng.shape), _resident(lnb.shape),
    ]
    operands = [x, w_in, vg3, vb3, ws, bs_rows, w_out, lng, lnb]
    w_scr, wo_scr, _, sems = _weight_scratch(w_in, w_out)
    assert len(operands) == N_GMLP_INPUTS
    out_specs = [pl.BlockSpec((tile, D_MODEL), lambda s: (tile_out(s), 0))]
    out_shape = [jax.ShapeDtypeStruct(x.shape, F32)]
    if emit_v:
        out_specs.append(pl.BlockSpec((N_GROUPS, tile, LANE_GROUP), lambda s: (0, tile_in(s), 0)))
        out_shape.append(jax.ShapeDtypeStruct((N_GROUPS, rows, LANE_GROUP), F32))
    if decode is not None:
        assert len(decode) == N_DECODE_INPUTS
        state = decode[-1]
        n_seq = state.shape[0]
        seqs_per_step = n_seq // n_tiles
        rows_per_step = seqs_per_step * decode_seq
        assert n_seq % n_tiles == 0 and rows_per_step % SUBLANES == 0 and SUBLANES % decode_seq == 0
        tok_spec = pl.BlockSpec((N_GROUPS, rows_per_step, LANE_GROUP), lambda s: (0, tile_in(s), 0))
        st_spec = pl.BlockSpec((seqs_per_step, N_HEADS, HEAD, HEAD), lambda s: (tile_in(s), 0, 0, 0))
        in_specs += [tok_spec] * (N_DECODE_INPUTS - 1) + [st_spec]
        operands += list(decode)
        out_specs += [tok_spec, st_spec]
        out_shape += [jax.ShapeDtypeStruct(decode[0].shape, F32), jax.ShapeDtypeStruct(state.shape, F32)]
    return pl.pallas_call(
        functools.partial(_gmlp_kernel, emit_v=emit_v, decode_seq=decode_seq if decode is not None else 0),
        grid=(n_tiles + 1,),
        in_specs=in_specs,
        out_specs=out_specs,
        out_shape=out_shape,
        scratch_shapes=[
            pltpu.VMEM((N_GROUPS, tile, LANE_GROUP), F32),
            pltpu.VMEM((N_GROUPS, tile, LANE_GROUP), BF16),
            pltpu.VMEM((tile, D_MODEL), F32),
            w_scr, wo_scr, sems,
        ],
        compiler_params=pltpu.CompilerParams(
            dimension_semantics=("arbitrary",), vmem_limit_bytes=VMEM_LIMIT_BYTES),
        name="gmlp_emit_v" if emit_v else "gmlp",
    )(*operands)


def _lane_groups_of_rows(w):
    return w.reshape(w.shape[0] // LANE_GROUP, LANE_GROUP, w.shape[1])


def _lane_groups_of_vector(v):
    return v.reshape(v.shape[0], N_GROUPS, LANE_GROUP).transpose(1, 0, 2)


def kernel(x_prompt, x_sample, state_hgrn, w_in_a, lb_logits_a, gnorm_a, w_out_a, w_in_b, lnv_g_b,
           lnv_b_b, w_s_b, b_s_b, w_out_b, ln_g, ln_b):
    bsz, seq, _ = x_prompt.shape
    n_seq, dec_seq, _ = x_sample.shape

    wa, woa = w_in_a[0], w_out_a[0]
    lbl3 = _lane_groups_of_vector(lb_logits_a)
    gn3 = _lane_groups_of_vector(gnorm_a[0:1])
    lng0, lnb0 = ln_g[0:1], ln_b[0:1]

    hp, sp = _hgrn_prompt(x_prompt, wa, lbl3, gn3, woa, lng0, lnb0, layer=0)

    xs = x_sample.reshape(n_seq * dec_seq, D_MODEL)
    decode_gates = _hgrn_sample_gates(xs, wa, lbl3, gn3, layer=0, dec_seq=dec_seq)

    w3b, wo3b = w_in_b[0], w_out_b[0]
    vg3 = _lane_groups_of_vector(lnv_g_b[0:1])
    vb3 = _lane_groups_of_vector(lnv_b_b[0:1])
    lng1, lnb1 = ln_g[1:2], ln_b[1:2]
    ws = w_s_b[0]
    bs = b_s_b[0]

    yp, og, ss = _gmlp(hp.reshape(bsz * seq, D_MODEL), w3b, vg3, vb3, ws, bs, wo3b, lng1, lnb1,
                       emit_v=False, decode=(*decode_gates, state_hgrn[0]), decode_seq=dec_seq)
    yp = yp.reshape(bsz, seq, D_MODEL)
    hs = _out_proj_ln(og, xs, _lane_groups_of_rows(woa), lng0, lnb0)

    reps = CHUNK_B // dec_seq
    ws_dec = jnp.einsum('ab,gts->gatbs', jnp.eye(reps, dtype=F32), ws[:, :dec_seq, :dec_seq]).reshape(
        ws.shape[0], CHUNK_B, CHUNK_B)
    bs_dec = jnp.tile(bs[:, :dec_seq], (1, reps))
    ys, vn = _gmlp(hs.reshape(n_seq * dec_seq, D_MODEL), w3b, vg3, vb3, ws_dec, bs_dec, wo3b,
                   lng1, lnb1, emit_v=True)
    ys = ys.reshape(n_seq, dec_seq, D_MODEL)
    vs = vn.transpose(1, 0, 2).reshape(n_seq, dec_seq, E)

    return (yp, ys, sp[None], ss[None], vs[None])
```

```python
import functools
import math

import jax
import jax.numpy as jnp
from jax import lax
from jax.experimental import pallas as pl
from jax.experimental.pallas import tpu as pltpu

F32 = jnp.float32
BF16 = jnp.bfloat16

D_MODEL = 1024
E = 2048
HEAD = 128
N_HEADS = E // HEAD
LANE_GROUP = 256
N_GROUPS = E // LANE_GROUP
HEADS_PER_GROUP = LANE_GROUP // HEAD
CHUNK_A = 64
CHUNK_B = 128
ROW_TILE = 256
HGRN_TILE = 512
HGRN_BLOCK = 2 * CHUNK_A
GMLP_TILE = 512
SUBLANES = 8
DEPTH = 2
ALPHA = (2 * DEPTH) ** 0.25
LN_EPS = 1e-5
VMEM_LIMIT_BYTES = 60000 * 1024


def _dot(a, b):
    return jnp.dot(a, b, preferred_element_type=F32)


def _group_cols(w_ref, group):
    if len(w_ref.shape) == 3:
        return w_ref[group]
    return w_ref[:, group * LANE_GROUP:(group + 1) * LANE_GROUP]


STAGE_ROWS = 512
STAGE_SLOTS = 8


def _stage_weight(w_hbm, stage, sems, store):
    n_row_blocks = w_hbm.shape[0] // STAGE_ROWS
    n_col_blocks = w_hbm.shape[1] // LANE_GROUP
    chunks = [(r, c) for c in range(n_col_blocks) for r in range(n_row_blocks)]

    def copy(i):
        r, c = chunks[i]
        slot = i % STAGE_SLOTS
        return pltpu.make_async_copy(
            w_hbm.at[pl.ds(r * STAGE_ROWS, STAGE_ROWS), pl.ds(c * LANE_GROUP, LANE_GROUP)],
            stage.at[slot], sems.at[slot])

    for i in range(min(STAGE_SLOTS, len(chunks))):
        copy(i).start()
    for i, (r, c) in enumerate(chunks):
        copy(i).wait()
        store(r, c, stage[i % STAGE_SLOTS].astype(BF16))
        if i + STAGE_SLOTS < len(chunks):
            copy(i + STAGE_SLOTS).start()


def _stage_projection_weights(w_in_hbm, w_out_hbm, w_scr, wo_scr, stage, sems):
    def store_in(r, c, chunk):
        w_scr[c, r * STAGE_ROWS:(r + 1) * STAGE_ROWS, :] = chunk

    def store_out(r, c, chunk):
        for k in range(STAGE_ROWS // LANE_GROUP):
            wo_scr[r * (STAGE_ROWS // LANE_GROUP) + k, :, c * LANE_GROUP:(c + 1) * LANE_GROUP] = (
                chunk[k * LANE_GROUP:(k + 1) * LANE_GROUP])

    _stage_weight(w_in_hbm, stage, sems, store_in)
    _stage_weight(w_out_hbm, stage, sems, store_out)


def _dot_nt(a, b):
    return lax.dot_general(a, b, (((1,), (1,)), ((), ())), preferred_element_type=F32)


def _dot_tn(a, b):
    return lax.dot_general(a, b, (((0,), (0,)), ((), ())), preferred_element_type=F32)


def _silu(x):
    return x / (1.0 + jnp.exp(-x))


def _gelu_tanh(x):
    cdf = 0.5 * (1.0 + jnp.tanh(math.sqrt(2.0 / math.pi) * (x + 0.044715 * (x * x * x))))
    return x * cdf


def _layer_norm(x, g, b):
    mu = jnp.mean(x, axis=-1, keepdims=True)
    d = x - mu
    var = jnp.mean(d * d, axis=-1, keepdims=True)
    return d * lax.rsqrt(var + LN_EPS) * g + b


def _block_masks(n, block):
    shift = block.bit_length() - 1
    row = lax.broadcasted_iota(jnp.int32, (n, n), 0)
    col = lax.broadcasted_iota(jnp.int32, (n, n), 1)
    same = jnp.right_shift(row, shift) == jnp.right_shift(col, shift)
    causal = jnp.logical_and(same, col <= row)
    return same, causal


def _split_dot(m, x):
    hi = x.astype(BF16)
    lo = (x - hi.astype(F32)).astype(BF16)
    return _dot(jnp.concatenate([m, m], axis=1), jnp.concatenate([hi, lo], axis=0))


def _forget_lower_bound(logits, layer):
    m = jnp.max(logits, axis=0, keepdims=True)
    e = jnp.exp(logits - m)
    den = jnp.sum(e, axis=0, keepdims=True)
    num = jnp.sum(e[: layer + 1], axis=0, keepdims=True)
    return num / den


def _hgrn_gates(xb, w_q, w_f, lb, tri, valid):
    qp = _dot(xb, w_q)
    q = _silu(qp)
    z = _dot(xb, w_f)
    c1 = 1.0 - lb
    cs = c1 / (1.0 + jnp.exp(-z))
    f = lb + cs
    k = c1 - cs
    lf = jnp.log(f)
    if valid is not None:
        lf = jnp.where(valid, lf, 0.0)
        k = jnp.where(valid, k, 0.0)
    cum = _split_dot(tri, lf)
    return q * jnp.exp(cum), k * jnp.exp(-cum), cum, lf


def _rms_gate(o, gate):
    ms = jnp.mean(o * o, axis=-1, keepdims=True)
    return o * lax.rsqrt(ms + LN_EPS) * gate


def _hgrn_prompt_kernel(x_ref, w_hbm, lbl_ref, gn_ref, wo_hbm, lng_ref, lnb_ref,
                        y_ref, s_ref, st_scr, og_scr, h_scr, w_ref, wo_ref, stage, sems,
                        *, layer, tiles_per_row):
    s = pl.program_id(0)
    n_tiles = pl.num_programs(0) - 1

    @pl.when(s == 0)
    def _():
        _stage_projection_weights(w_hbm, wo_hbm, w_ref, wo_ref, stage, sems)
        h_scr[...] = jnp.zeros_like(h_scr)

    @pl.when(s < n_tiles)
    def _():
        _hgrn_tile_step(lax.rem(s, tiles_per_row), tiles_per_row, x_ref, w_ref, lbl_ref, gn_ref, wo_ref,
                        lng_ref, lnb_ref, y_ref, s_ref, st_scr, og_scr, h_scr, layer=layer)

    @pl.when(s == n_tiles)
    def _():
        y_ref[0] = _layer_norm(h_scr[...], lng_ref[...], lnb_ref[...])


def _hgrn_tile_step(j, tiles_per_row, x_ref, w_ref, lbl_ref, gn_ref, wo_ref, lng_ref, lnb_ref,
                    y_ref, s_ref, st_scr, og_scr, h_scr, *, layer):
    tile = x_ref.shape[1]
    n_blocks = tile // HGRN_BLOCK
    chunks_per_block = HGRN_BLOCK // CHUNK_A

    @pl.when(j == 0)
    def _():
        st_scr[...] = jnp.zeros_like(st_scr)

    y_ref[0] = _layer_norm(h_scr[...], lng_ref[...], lnb_ref[...])

    x = x_ref[0]
    xb = x.astype(BF16)
    _, causal = _block_masks(HGRN_BLOCK, CHUNK_A)
    tri = jnp.where(causal, 1.0, 0.0).astype(BF16)

    def block(p):
        return slice(p * HGRN_BLOCK, (p + 1) * HGRN_BLOCK)

    def stage_project(hp):
        z = _dot(xb, _group_cols(w_ref, N_GROUPS + hp))
        qp = _dot(xb, _group_cols(w_ref, hp))
        v = _dot(xb, _group_cols(w_ref, 2 * N_GROUPS + hp))
        g = _dot(xb, _group_cols(w_ref, 3 * N_GROUPS + hp))
        lb = _forget_lower_bound(lbl_ref[hp], layer)
        c1 = 1.0 - lb
        cs = c1 / (1.0 + jnp.exp(-z))
        k = c1 - cs
        lf = jnp.log(lb + cs)
        cum = jnp.concatenate([_split_dot(tri, lf[block(p)]) for p in range(n_blocks)], axis=0)
        return _silu(qp), k, cum, v.astype(BF16), _silu(g) * gn_ref[hp]

    def stage_scores(hp, a):
        q, k, cum, vb, gate = a
        qdb = (q * jnp.exp(cum)).astype(BF16)
        ki = k * jnp.exp(-cum)
        kib = ki.astype(BF16)
        n_chunks = tile // CHUNK_A
        dl_rows = jnp.exp(jnp.concatenate(
            [cum[(c + 1) * CHUNK_A - 1:(c + 1) * CHUNK_A] for c in range(n_chunks)]
            + [jnp.zeros((HEAD - n_chunks, LANE_GROUP), F32)], axis=0))
        dl_cols = dl_rows.T
        heads = []
        for i in range(HEADS_PER_GROUP):
            ls = slice(i * HEAD, (i + 1) * HEAD)
            kit = ki[:, ls].T.astype(BF16)
            scores = [_dot(qdb[block(p), ls], kit[:, block(p)]) for p in range(n_blocks)]
            dls, us = [], []
            for c in range(n_chunks):
                lo, hi = c * CHUNK_A, (c + 1) * CHUNK_A
                ke = (ki[lo:hi, ls] * dl_rows[c:c + 1, ls]).astype(BF16)
                dls.append(dl_cols[ls, c:c + 1])
                us.append(_dot_tn(ke, vb[lo:hi, ls]))
            heads.append((qdb[:, ls], vb[:, ls], scores, dls, us))
        return heads, gate

    def stage_output(hp, b):
        heads, gate = b
        ogs = []
        for i, (qdb, vb, scores, dls, us) in enumerate(heads):
            st = st_scr[HEADS_PER_GROUP * hp + i]
            os = []
            for p in range(n_blocks):
                o_intra = _dot(jnp.where(causal, scores[p], 0.0).astype(BF16), vb[block(p)])
                for cc in range(chunks_per_block):
                    c = p * chunks_per_block + cc
                    lo, hi = c * CHUNK_A, (c + 1) * CHUNK_A
                    os.append(o_intra[cc * CHUNK_A:(cc + 1) * CHUNK_A] + _dot(qdb[lo:hi], st.astype(BF16)))
                    st = st * dls[c] + us[c]
            st_scr[HEADS_PER_GROUP * hp + i] = st
            o = jnp.concatenate(os, axis=0)
            ogs.append(_rms_gate(o, gate[:, i * HEAD:(i + 1) * HEAD]))
        og_scr[hp] = jnp.concatenate(ogs, axis=1).astype(BF16)

    def out_proj(groups):
        y = _dot(og_scr[groups[0]], wo_ref[groups[0]])
        for hp in groups[1:]:
            y = y + _dot(og_scr[hp], wo_ref[hp])
        return y

    y_parts = []
    projected, scored = {}, {}
    for t in range(N_GROUPS + 2):
        if t < N_GROUPS:
            projected[t] = stage_project(t)
        if t == N_GROUPS:
            y_parts.append(out_proj(list(range(0, N_GROUPS - 4))))
        if t == N_GROUPS + 1:
            y_parts.append(out_proj([N_GROUPS - 4, N_GROUPS - 3]))
        if 0 <= t - 1 < N_GROUPS:
            scored[t - 1] = stage_scores(t - 1, projected.pop(t - 1))
        if 0 <= t - 2 < N_GROUPS:
            stage_output(t - 2, scored.pop(t - 2))
    y_parts.append(out_proj([N_GROUPS - 2, N_GROUPS - 1]))
    y = y_parts[0] + y_parts[1] + y_parts[2]
    h_scr[...] = ALPHA * x + y

    @pl.when(j == tiles_per_row - 1)
    def _():
        s_ref[0] = st_scr[...]


def _resident(shape):
    nd = len(shape)
    return pl.BlockSpec(shape, lambda *_: (0,) * nd, pipeline_mode=pl.Buffered(1))


def _weight_scratch(w_in, w_out):
    k, n = w_in.shape
    return [
        pltpu.VMEM((n // LANE_GROUP, k, LANE_GROUP), BF16),
        pltpu.VMEM((w_out.shape[0] // LANE_GROUP, LANE_GROUP, w_out.shape[1]), BF16),
        pltpu.VMEM((STAGE_SLOTS, STAGE_ROWS, LANE_GROUP), F32),
        pltpu.SemaphoreType.DMA((STAGE_SLOTS,)),
    ]


def _hgrn_prompt(x, w_in, lbl3, gn3, w_out, lng, lnb, layer):
    bsz, seq, _ = x.shape
    tiles_per_row = seq // HGRN_TILE
    n_tiles = bsz * tiles_per_row
    hbm = pl.BlockSpec(memory_space=pl.ANY)

    def tile_in(s):
        t = jnp.minimum(s, n_tiles - 1)
        return t // tiles_per_row, t % tiles_per_row

    def tile_out(s):
        t = jnp.maximum(s - 1, 0)
        return t // tiles_per_row, t % tiles_per_row

    return pl.pallas_call(
        functools.partial(_hgrn_prompt_kernel, layer=layer, tiles_per_row=tiles_per_row),
        grid=(n_tiles + 1,),
        in_specs=[
            pl.BlockSpec((1, HGRN_TILE, D_MODEL), lambda s: (*tile_in(s), 0)),
            hbm, _resident(lbl3.shape), _resident(gn3.shape), hbm,
            _resident(lng.shape), _resident(lnb.shape),
        ],
        out_specs=[
            pl.BlockSpec((1, HGRN_TILE, D_MODEL), lambda s: (*tile_out(s), 0)),
            pl.BlockSpec((1, N_HEADS, HEAD, HEAD), lambda s: (tile_in(s)[0], 0, 0, 0)),
        ],
        out_shape=[
            jax.ShapeDtypeStruct(x.shape, F32),
            jax.ShapeDtypeStruct((bsz, N_HEADS, HEAD, HEAD), F32),
        ],
        scratch_shapes=[
            pltpu.VMEM((N_HEADS, HEAD, HEAD), F32),
            pltpu.VMEM((N_GROUPS, HGRN_TILE, LANE_GROUP), BF16),
            pltpu.VMEM((HGRN_TILE, D_MODEL), F32),
            *_weight_scratch(w_in, w_out),
        ],
        compiler_params=pltpu.CompilerParams(
            dimension_semantics=("arbitrary",), vmem_limit_bytes=VMEM_LIMIT_BYTES),
        name="hgrn_prompt",
    )(x, w_in, lbl3, gn3, w_out, lng, lnb)


def _hgrn_sample_gates_kernel(x_ref, wq_ref, wf_ref, wi_ref, wg_ref, lbl_ref, gn_ref,
                              qd_ref, ki_ref, ke_ref, v_ref, gate_ref, dl_ref, *, layer, dec_seq):
    rows = x_ref.shape[0]
    xb = x_ref[...].astype(BF16)
    same, causal = _block_masks(ROW_TILE, dec_seq)
    tri = jnp.where(causal, 1.0, 0.0).astype(BF16)
    blk = jnp.where(same, 1.0, 0.0).astype(BF16)
    lb = _forget_lower_bound(lbl_ref[0], layer)

    qp = _dot(xb, wq_ref[...].astype(BF16))
    q = _silu(qp)
    z = _dot(xb, wf_ref[...].astype(BF16))
    c1 = 1.0 - lb
    cs = c1 / (1.0 + jnp.exp(-z))
    k = c1 - cs
    lf = jnp.log(lb + cs)
    for r in range(rows // ROW_TILE):
        rs = slice(r * ROW_TILE, (r + 1) * ROW_TILE)
        cum = _split_dot(tri, lf[rs])
        cum_last = _split_dot(blk, lf[rs])
        ki = k[rs] * jnp.exp(-cum)
        dl = jnp.exp(cum_last)
        qd_ref[0, rs] = q[rs] * jnp.exp(cum)
        ki_ref[0, rs] = ki
        ke_ref[0, rs] = ki * dl
        dl_ref[0, rs] = dl
    v_ref[0] = _dot(xb, wi_ref[...].astype(BF16))
    gate_ref[0] = _silu(_dot(xb, wg_ref[...].astype(BF16))) * gn_ref[0]


def _hgrn_sample_gates(x_rows, w3, lbl3, gn3, layer, dec_seq):
    rows = x_rows.shape[0]
    assert rows % ROW_TILE == 0 and SUBLANES % dec_seq == 0
    w_spec = lambda part: pl.BlockSpec((D_MODEL, LANE_GROUP), lambda g: (0, part * N_GROUPS + g))
    out_spec = pl.BlockSpec((1, rows, LANE_GROUP), lambda g: (g, 0, 0))
    out_shape = jax.ShapeDtypeStruct((N_GROUPS, rows, LANE_GROUP), F32)
    return pl.pallas_call(
        functools.partial(_hgrn_sample_gates_kernel, layer=layer, dec_seq=dec_seq),
        grid=(N_GROUPS,),
        in_specs=[
            pl.BlockSpec((rows, D_MODEL), lambda g: (0, 0)),
            w_spec(0), w_spec(1), w_spec(2), w_spec(3),
            pl.BlockSpec((1,) + lbl3.shape[1:], lambda g: (g, 0, 0)),
            pl.BlockSpec((1, 1, LANE_GROUP), lambda g: (g, 0, 0)),
        ],
        out_specs=[out_spec] * 6,
        out_shape=[out_shape] * 6,
        compiler_params=pltpu.CompilerParams(
            dimension_semantics=("arbitrary",), vmem_limit_bytes=VMEM_LIMIT_BYTES),
        name="hgrn_sample_gates",
    )(x_rows, w3, w3, w3, w3, lbl3, gn3)


def _decode_state_stages(qd_ref, ki_ref, ke_ref, v_ref, gate_ref, dl_ref, s_ref, og_ref, so_ref, dec_seq):
    seqs_per_tile = SUBLANES // dec_seq
    n_tiles = qd_ref.shape[1] // SUBLANES
    shift = dec_seq.bit_length() - 1
    trow = jnp.right_shift(lax.broadcasted_iota(jnp.int32, (SUBLANES, HEAD), 0), shift)
    in_seq = [trow == s for s in range(seqs_per_tile)]
    pad_rows = jnp.zeros((HEAD - N_HEADS, HEAD), F32)
    tiles = [slice(m * SUBLANES, (m + 1) * SUBLANES) for m in range(n_tiles)]
    dcols, pending = {}, {}
    lane_head = lax.broadcasted_iota(jnp.int32, (SUBLANES, LANE_GROUP), 1) // HEAD
    srow = lax.broadcasted_iota(jnp.int32, (SUBLANES, HEADS_PER_GROUP * SUBLANES), 0)
    scol = jnp.bitwise_and(lax.broadcasted_iota(jnp.int32, (SUBLANES, HEADS_PER_GROUP * SUBLANES), 1),
                           SUBLANES - 1)
    causal = jnp.logical_and(jnp.right_shift(srow, shift) == jnp.right_shift(scol, shift), scol <= srow)

    def per_head_rows(t):
        return jnp.concatenate([jnp.where(lane_head == i, t, 0.0) for i in range(HEADS_PER_GROUP)], axis=0)

    def per_seq_lanes(t):
        return jnp.concatenate([jnp.where(in_seq[s], t, 0.0) for s in range(seqs_per_tile)], axis=1)

    def decay_columns(m):
        dl = [dl_ref[hp, tiles[m], :] for hp in range(N_GROUPS)]
        for s in range(seqs_per_tile):
            r = s * dec_seq
            rows = [dl[h // HEADS_PER_GROUP][r:r + 1, (h % HEADS_PER_GROUP) * HEAD:(h % HEADS_PER_GROUP + 1) * HEAD]
                    for h in range(N_HEADS)]
            dcols[(m, s)] = jnp.concatenate(rows + [pad_rows], axis=0).T

    def start(hp):
        for m in range(n_tiles):
            if hp == 0:
                decay_columns(m)
            rs = tiles[m]
            qd = qd_ref[hp, rs, :]
            ke = ke_ref[hp, rs, :]
            v = v_ref[hp, rs, :]
            qdb = qd.astype(BF16)
            scores = _dot_nt(qdb, per_head_rows(ki_ref[hp, rs, :]).astype(BF16))
            o_inter = []
            for i in range(HEADS_PER_GROUP):
                ls = slice(i * HEAD, (i + 1) * HEAD)
                h = HEADS_PER_GROUP * hp + i
                sts = [s_ref[m * seqs_per_tile + s, h] for s in range(seqs_per_tile)]
                st_rows = jnp.concatenate([st.astype(BF16) for st in sts], axis=0)
                o_inter.append(_dot(per_seq_lanes(qd[:, ls]).astype(BF16), st_rows))
                u = _dot_tn(per_seq_lanes(ke[:, ls]).astype(BF16), v[:, ls].astype(BF16))
                for s in range(seqs_per_tile):
                    so_ref[m * seqs_per_tile + s, h] = (dcols[(m, s)][:, h:h + 1] * sts[s]
                                                       + u[s * HEAD:(s + 1) * HEAD])
            pending[(hp, m)] = (scores, jnp.concatenate(o_inter, axis=1), per_head_rows(v).astype(BF16))

    def finish(hp):
        for m in range(n_tiles):
            scores, o_inter, v_rows = pending.pop((hp, m))
            o = _dot(jnp.where(causal, scores, 0.0).astype(BF16), v_rows) + o_inter
            gate = gate_ref[hp, tiles[m], :]
            og_ref[hp, tiles[m], :] = jnp.concatenate(
                [_rms_gate(o[:, i * HEAD:(i + 1) * HEAD], gate[:, i * HEAD:(i + 1) * HEAD])
                 for i in range(HEADS_PER_GROUP)], axis=1)

    return start, finish


def _out_proj_ln_kernel(og_ref, x_ref, wo_ref, lng_ref, lnb_ref, y_ref):
    y = _dot(og_ref[0].astype(BF16), wo_ref[0].astype(BF16))
    for hp in range(1, N_GROUPS):
        y = y + _dot(og_ref[hp].astype(BF16), wo_ref[hp].astype(BF16))
    y_ref[...] = _layer_norm(ALPHA * x_ref[...] + y, lng_ref[...], lnb_ref[...])


def _out_proj_ln(og, x, wo3, lng, lnb):
    rows = x.shape[0]
    return pl.pallas_call(
        _out_proj_ln_kernel,
        grid=(rows // ROW_TILE,),
        in_specs=[
            pl.BlockSpec((N_GROUPS, ROW_TILE, LANE_GROUP), lambda r: (0, r, 0)),
            pl.BlockSpec((ROW_TILE, D_MODEL), lambda r: (r, 0)),
            _resident(wo3.shape), _resident(lng.shape), _resident(lnb.shape),
        ],
        out_specs=pl.BlockSpec((ROW_TILE, D_MODEL), lambda r: (r, 0)),
        out_shape=jax.ShapeDtypeStruct(x.shape, F32),
        compiler_params=pltpu.CompilerParams(
            dimension_semantics=("arbitrary",), vmem_limit_bytes=VMEM_LIMIT_BYTES),
        name="out_proj_ln",
    )(og, x, wo3, lng, lnb)


N_GMLP_INPUTS = 9
N_DECODE_INPUTS = 7


def _gmlp_kernel(*refs, emit_v, decode_seq, mix_block):
    refs = list(refs)
    gmlp_in = refs[:N_GMLP_INPUTS]
    del refs[:N_GMLP_INPUTS]
    decode_in = []
    if decode_seq:
        decode_in = refs[:N_DECODE_INPUTS]
        del refs[:N_DECODE_INPUTS]
    y_ref = refs.pop(0)
    vn_ref = refs.pop(0) if emit_v else None
    decode_out = [refs.pop(0), refs.pop(0)] if decode_seq else []
    v_scr, og_scr, h_scr, w_ref, wo_ref, sems = refs
    x_ref, w_hbm, vg_ref, vb_ref, ws_ref, bs_ref, wo_hbm, lng_ref, lnb_ref = gmlp_in
    s = pl.program_id(0)
    n_tiles = pl.num_programs(0) - 1

    @pl.when(s == 0)
    def _():
        _stage_projection_weights(w_hbm, wo_hbm, w_ref, wo_ref, v_scr, sems)
        h_scr[...] = jnp.zeros_like(h_scr)

    @pl.when(s < n_tiles)
    def _():
        decode = _decode_state_stages(*decode_in, *decode_out, decode_seq) if decode_seq else None
        _gmlp_tile_step(x_ref, w_ref, vg_ref, vb_ref, ws_ref, bs_ref, wo_ref, lng_ref, lnb_ref,
                        y_ref, vn_ref, v_scr, og_scr, h_scr, decode, mix_block)

    @pl.when(s == n_tiles)
    def _():
        y_ref[...] = _layer_norm(h_scr[...], lng_ref[...], lnb_ref[...])


def _gmlp_tile_step(x_ref, w_ref, vg_ref, vb_ref, ws_ref, bs_ref, wo_ref, lng_ref, lnb_ref,
                    y_ref, vn_ref, v_scr, og_scr, h_scr, decode, mix_block):
    y_ref[...] = _layer_norm(h_scr[...], lng_ref[...], lnb_ref[...])

    n_chunks = x_ref.shape[0] // CHUNK_B
    x = x_ref[...]
    xb = x.astype(BF16)

    half = x_ref.shape[0] // 2
    for gp in range(N_GROUPS):
        for r in range(2):
            rows = slice(r * half, (r + 1) * half)
            v_scr[gp, rows] = _gelu_tanh(_dot(xb[rows], _group_cols(w_ref, N_GROUPS + gp)))
        if decode is not None:
            decode[0](gp)

    s1 = jnp.sum(v_scr[0], axis=-1, keepdims=True)
    for gp in range(1, N_GROUPS):
        s1 = s1 + jnp.sum(v_scr[gp], axis=-1, keepdims=True)
    mu = s1 * (1.0 / E)
    s2 = jnp.zeros_like(mu)
    for gp in range(N_GROUPS):
        d = v_scr[gp] - mu
        s2 = s2 + jnp.sum(d * d, axis=-1, keepdims=True)
    rstd = lax.rsqrt(s2 * (1.0 / E) + LN_EPS)

    _, causal = _block_masks(CHUNK_B, mix_block)
    ws_reps = CHUNK_B // ws_ref.shape[1]
    n_mix_groups = bs_ref.shape[0]
    bias_cols = jnp.concatenate(
        [bs_ref[...], jnp.zeros((CHUNK_B - n_mix_groups, CHUNK_B), F32)], axis=0).T

    def stage_gate(gp):
        return _dot(xb, _group_cols(w_ref, gp)), _dot(xb, _group_cols(w_ref, 2 * N_GROUPS + gp))

    def stage_mix(gp, gate):
        u_pre, z = gate
        vn = (v_scr[gp] - mu) * rstd * vg_ref[gp] + vb_ref[gp]
        if vn_ref is not None:
            vn_ref[:, gp * LANE_GROUP:(gp + 1) * LANE_GROUP] = vn
        vnb = vn.astype(BF16)
        cols = []
        for i in range(HEADS_PER_GROUP):
            ls = slice(i * HEAD, (i + 1) * HEAD)
            g = HEADS_PER_GROUP * gp + i
            wg = jnp.concatenate([ws_ref[g]] * ws_reps, axis=0) if ws_reps > 1 else ws_ref[g]
            wc = jnp.where(causal, wg, 0.0).astype(BF16)
            bias = bias_cols[:, g:g + 1]
            cols.append(jnp.concatenate(
                [_dot(wc, vnb[c * CHUNK_B:(c + 1) * CHUNK_B, ls]) + bias for c in range(n_chunks)], axis=0))
        mixed = jnp.concatenate(cols, axis=1)
        og_scr[gp] = (_gelu_tanh(u_pre) * mixed * _silu(z)).astype(BF16)

    def out_proj(groups):
        y = _dot(og_scr[groups[0]], wo_ref[groups[0]])
        for gp in groups[1:]:
            y = y + _dot(og_scr[gp], wo_ref[gp])
        return y

    y_parts = []
    gates = {0: stage_gate(0)}
    for gp in range(N_GROUPS):
        if gp + 1 < N_GROUPS:
            gates[gp + 1] = stage_gate(gp + 1)
        stage_mix(gp, gates.pop(gp))
        if decode is not None:
            decode[1](gp)
        if gp == N_GROUPS - 2:
            y_parts.append(out_proj(list(range(0, N_GROUPS - 3))))
    y_parts.append(out_proj([N_GROUPS - 3, N_GROUPS - 2]))
    y_parts.append(out_proj([N_GROUPS - 1]))
    y = y_parts[0] + y_parts[1] + y_parts[2]
    h_scr[...] = ALPHA * x + y


def _gmlp(x, w_in, vg3, vb3, ws, bs_rows, w_out, lng, lnb, emit_v, mix_block=CHUNK_B, decode=None,
          decode_seq=0):
    rows = x.shape[0]
    tile = min(GMLP_TILE, rows)
    assert tile == STAGE_ROWS and N_GROUPS == STAGE_SLOTS
    n_tiles = rows // tile
    tile_in = lambda s: jnp.minimum(s, n_tiles - 1)
    tile_out = lambda s: jnp.maximum(s - 1, 0)
    hbm = pl.BlockSpec(memory_space=pl.ANY)
    in_specs = [
        pl.BlockSpec((tile, D_MODEL), lambda s: (tile_in(s), 0)),
        hbm, _resident(vg3.shape), _resident(vb3.shape), _resident(ws.shape),
        _resident(bs_rows.shape), hbm, _resident(lng.shape), _resident(lnb.shape),
    ]
    operands = [x, w_in, vg3, vb3, ws, bs_rows, w_out, lng, lnb]
    w_scr, wo_scr, _, sems = _weight_scratch(w_in, w_out)
    assert len(operands) == N_GMLP_INPUTS
    out_specs = [pl.BlockSpec((tile, D_MODEL), lambda s: (tile_out(s), 0))]
    out_shape = [jax.ShapeDtypeStruct(x.shape, F32)]
    if emit_v:
        out_specs.append(pl.BlockSpec((tile, E), lambda s: (tile_in(s), 0)))
        out_shape.append(jax.ShapeDtypeStruct((rows, E), F32))
    if decode is not None:
        assert len(decode) == N_DECODE_INPUTS
        state = decode[-1]
        n_seq = state.shape[0]
        seqs_per_step = n_seq // n_tiles
        rows_per_step = seqs_per_step * decode_seq
        assert n_seq % n_tiles == 0 and rows_per_step % SUBLANES == 0 and SUBLANES % decode_seq == 0
        tok_spec = pl.BlockSpec((N_GROUPS, rows_per_step, LANE_GROUP), lambda s: (0, tile_in(s), 0))
        st_spec = pl.BlockSpec((seqs_per_step, N_HEADS, HEAD, HEAD), lambda s: (tile_in(s), 0, 0, 0))
        in_specs += [tok_spec] * (N_DECODE_INPUTS - 1) + [st_spec]
        operands += list(decode)
        out_specs += [tok_spec, st_spec]
        out_shape += [jax.ShapeDtypeStruct(decode[0].shape, F32), jax.ShapeDtypeStruct(state.shape, F32)]
    return pl.pallas_call(
        functools.partial(_gmlp_kernel, emit_v=emit_v, decode_seq=decode_seq if decode is not None else 0,
                          mix_block=mix_block),
        grid=(n_tiles + 1,),
        in_specs=in_specs,
        out_specs=out_specs,
        out_shape=out_shape,
        scratch_shapes=[
            pltpu.VMEM((N_GROUPS, tile, LANE_GROUP), F32),
            pltpu.VMEM((N_GROUPS, tile, LANE_GROUP), BF16),
            pltpu.VMEM((tile, D_MODEL), F32),
            w_scr, wo_scr, sems,
        ],
        compiler_params=pltpu.CompilerParams(
            dimension_semantics=("arbitrary",), vmem_limit_bytes=VMEM_LIMIT_BYTES),
        name="gmlp_emit_v" if emit_v else "gmlp",
    )(*operands)


def _lane_groups_of_rows(w):
    return w.reshape(w.shape[0] // LANE_GROUP, LANE_GROUP, w.shape[1])


def _lane_groups_of_vector(v):
    return v.reshape(v.shape[0], N_GROUPS, LANE_GROUP).transpose(1, 0, 2)


def kernel(x_prompt, x_sample, state_hgrn, w_in_a, lb_logits_a, gnorm_a, w_out_a, w_in_b, lnv_g_b,
           lnv_b_b, w_s_b, b_s_b, w_out_b, ln_g, ln_b):
    bsz, seq, _ = x_prompt.shape
    n_seq, dec_seq, _ = x_sample.shape

    wa, woa = w_in_a[0], w_out_a[0]
    lbl3 = _lane_groups_of_vector(lb_logits_a)
    gn3 = _lane_groups_of_vector(gnorm_a[0:1])
    lng0, lnb0 = ln_g[0:1], ln_b[0:1]

    hp, sp = _hgrn_prompt(x_prompt, wa, lbl3, gn3, woa, lng0, lnb0, layer=0)

    xs = x_sample.reshape(n_seq * dec_seq, D_MODEL)
    decode_gates = _hgrn_sample_gates(xs, wa, lbl3, gn3, layer=0, dec_seq=dec_seq)

    w3b, wo3b = w_in_b[0], w_out_b[0]
    vg3 = _lane_groups_of_vector(lnv_g_b[0:1])
    vb3 = _lane_groups_of_vector(lnv_b_b[0:1])
    lng1, lnb1 = ln_g[1:2], ln_b[1:2]
    ws = w_s_b[0]
    bs = b_s_b[0]

    yp, og, ss = _gmlp(hp.reshape(bsz * seq, D_MODEL), w3b, vg3, vb3, ws, bs, wo3b, lng1, lnb1,
                       emit_v=False, decode=(*decode_gates, state_hgrn[0]), decode_seq=dec_seq)
    yp = yp.reshape(bsz, seq, D_MODEL)
    hs = _out_proj_ln(og, xs, _lane_groups_of_rows(woa), lng0, lnb0)

    reps = CHUNK_B // dec_seq
    ws_dec = jnp.tile(ws[:, :dec_seq, :dec_seq], (1, SUBLANES // dec_seq, reps))
    bs_dec = jnp.tile(bs[:, :dec_seq], (1, reps))
    ys, vn = _gmlp(hs, w3b, vg3, vb3, ws_dec, bs_dec, wo3b, lng1, lnb1, emit_v=True, mix_block=dec_seq)
    ys = ys.reshape(n_seq, dec_seq, D_MODEL)
    vs = vn.reshape(n_seq, dec_seq, E)

    return (yp, ys, sp[None], ss[None], vs[None])
```

```python
import functools
import math

import jax
import jax.numpy as jnp
from jax import lax
from jax.experimental import pallas as pl
from jax.experimental.pallas import tpu as pltpu

F32 = jnp.float32
BF16 = jnp.bfloat16

D_MODEL = 1024
E = 2048
HEAD = 128
N_HEADS = E // HEAD
LANE_GROUP = 256
N_GROUPS = E // LANE_GROUP
HEADS_PER_GROUP = LANE_GROUP // HEAD
CHUNK_A = 64
CHUNK_B = 128
ROW_TILE = 256
HGRN_TILE = 512
HGRN_BLOCK = 2 * CHUNK_A
GMLP_TILE = 512
SUBLANES = 8
DEPTH = 2
ALPHA = (2 * DEPTH) ** 0.25
LN_EPS = 1e-5
VMEM_LIMIT_BYTES = 60000 * 1024


def _dot(a, b):
    return jnp.dot(a, b, preferred_element_type=F32)


STAGE_ROWS = 512
STAGE_SLOTS = 8


def _stage_weight(w_hbm, stage, sems, store):
    n_row_blocks = w_hbm.shape[0] // STAGE_ROWS
    n_col_blocks = w_hbm.shape[1] // LANE_GROUP
    chunks = [(r, c) for c in range(n_col_blocks) for r in range(n_row_blocks)]

    def copy(i):
        r, c = chunks[i]
        slot = i % STAGE_SLOTS
        return pltpu.make_async_copy(
            w_hbm.at[pl.ds(r * STAGE_ROWS, STAGE_ROWS), pl.ds(c * LANE_GROUP, LANE_GROUP)],
            stage.at[slot], sems.at[slot])

    for i in range(min(STAGE_SLOTS, len(chunks))):
        copy(i).start()
    for i, (r, c) in enumerate(chunks):
        copy(i).wait()
        store(r, c, stage[i % STAGE_SLOTS].astype(BF16))
        if i + STAGE_SLOTS < len(chunks):
            copy(i + STAGE_SLOTS).start()


def _stage_projection_weights(w_in_hbm, w_out_hbm, w_scr, wo_scr, stage, sems):
    def store_in(r, c, chunk):
        w_scr[c, r * STAGE_ROWS:(r + 1) * STAGE_ROWS, :] = chunk

    def store_out(r, c, chunk):
        for k in range(STAGE_ROWS // LANE_GROUP):
            wo_scr[r * (STAGE_ROWS // LANE_GROUP) + k, :, c * LANE_GROUP:(c + 1) * LANE_GROUP] = (
                chunk[k * LANE_GROUP:(k + 1) * LANE_GROUP])

    _stage_weight(w_in_hbm, stage, sems, store_in)
    _stage_weight(w_out_hbm, stage, sems, store_out)


def _interleave(stages):
    stages = list(stages)
    while stages:
        for stage in list(stages):
            if next(stage, stages) is stages:
                stages.remove(stage)


def _dot_nt(a, b):
    return lax.dot_general(a, b, (((1,), (1,)), ((), ())), preferred_element_type=F32)


def _dot_tn(a, b):
    return lax.dot_general(a, b, (((0,), (0,)), ((), ())), preferred_element_type=F32)


def _silu(x):
    return x / (1.0 + jnp.exp(-x))


def _gelu_tanh(x):
    cdf = 0.5 * (1.0 + jnp.tanh(math.sqrt(2.0 / math.pi) * (x + 0.044715 * (x * x * x))))
    return x * cdf


def _layer_norm(x, g, b):
    mu = jnp.mean(x, axis=-1, keepdims=True)
    d = x - mu
    var = jnp.mean(d * d, axis=-1, keepdims=True)
    return d * lax.rsqrt(var + LN_EPS) * g + b


def _block_masks(n, block):
    shift = block.bit_length() - 1
    row = lax.broadcasted_iota(jnp.int32, (n, n), 0)
    col = lax.broadcasted_iota(jnp.int32, (n, n), 1)
    same = jnp.right_shift(row, shift) == jnp.right_shift(col, shift)
    causal = jnp.logical_and(same, col <= row)
    return same, causal


def _split_dot(m, x):
    hi = x.astype(BF16)
    lo = (x - hi.astype(F32)).astype(BF16)
    return _dot(jnp.concatenate([m, m], axis=1), jnp.concatenate([hi, lo], axis=0))


def _forget_lower_bound(logits, layer):
    m = jnp.max(logits, axis=0, keepdims=True)
    e = jnp.exp(logits - m)
    den = jnp.sum(e, axis=0, keepdims=True)
    num = jnp.sum(e[: layer + 1], axis=0, keepdims=True)
    return num / den


def _hgrn_gates(xb, w_q, w_f, lb, tri, valid):
    qp = _dot(xb, w_q)
    q = _silu(qp)
    z = _dot(xb, w_f)
    c1 = 1.0 - lb
    cs = c1 / (1.0 + jnp.exp(-z))
    f = lb + cs
    k = c1 - cs
    lf = jnp.log(f)
    if valid is not None:
        lf = jnp.where(valid, lf, 0.0)
        k = jnp.where(valid, k, 0.0)
    cum = _split_dot(tri, lf)
    return q * jnp.exp(cum), k * jnp.exp(-cum), cum, lf


def _rms_gate(o, gate):
    ms = jnp.mean(o * o, axis=-1, keepdims=True)
    return o * lax.rsqrt(ms + LN_EPS) * gate


def _hgrn_prompt_kernel(x_ref, w_hbm, lbl_ref, gn_ref, wo_hbm, lng_ref, lnb_ref,
                        y_ref, s_ref, st_scr, og_scr, h_scr, w_ref, wo_ref, stage, sems,
                        *, layer, tiles_per_row):
    s = pl.program_id(0)
    n_tiles = pl.num_programs(0) - 1

    @pl.when(s == 0)
    def _():
        _stage_projection_weights(w_hbm, wo_hbm, w_ref, wo_ref, stage, sems)
        h_scr[...] = jnp.zeros_like(h_scr)

    @pl.when(s < n_tiles)
    def _():
        _hgrn_tile_step(lax.rem(s, tiles_per_row), tiles_per_row, x_ref, w_ref, lbl_ref, gn_ref, wo_ref,
                        lng_ref, lnb_ref, y_ref, s_ref, st_scr, og_scr, h_scr, layer=layer)

    @pl.when(s == n_tiles)
    def _():
        y_ref[0] = _layer_norm(h_scr[...], lng_ref[...], lnb_ref[...])


def _hgrn_tile_step(j, tiles_per_row, x_ref, w_ref, lbl_ref, gn_ref, wo_ref, lng_ref, lnb_ref,
                    y_ref, s_ref, st_scr, og_scr, h_scr, *, layer):
    tile = x_ref.shape[1]
    n_blocks = tile // HGRN_BLOCK
    chunks_per_block = HGRN_BLOCK // CHUNK_A

    @pl.when(j == 0)
    def _():
        st_scr[...] = jnp.zeros_like(st_scr)

    y_ref[0] = _layer_norm(h_scr[...], lng_ref[...], lnb_ref[...])

    x = x_ref[0]
    xb = x.astype(BF16)
    _, causal = _block_masks(HGRN_BLOCK, CHUNK_A)
    tri = jnp.where(causal, 1.0, 0.0).astype(BF16)

    def block(p):
        return slice(p * HGRN_BLOCK, (p + 1) * HGRN_BLOCK)

    projected, scored = {}, {}

    def stage_project(hp):
        z = _dot(xb, w_ref[N_GROUPS + hp])
        yield
        qp = _dot(xb, w_ref[hp])
        yield
        v = _dot(xb, w_ref[2 * N_GROUPS + hp])
        yield
        g = _dot(xb, w_ref[3 * N_GROUPS + hp])
        yield
        lb = _forget_lower_bound(lbl_ref[hp], layer)
        c1 = 1.0 - lb
        cs = c1 / (1.0 + jnp.exp(-z))
        k = c1 - cs
        lf = jnp.log(lb + cs)
        cum = jnp.concatenate([_split_dot(tri, lf[block(p)]) for p in range(n_blocks)], axis=0)
        projected[hp] = (_silu(qp), k, cum, v.astype(BF16), _silu(g) * gn_ref[hp])

    def stage_scores(hp):
        q, k, cum, vb, gate = projected.pop(hp)
        qdb = (q * jnp.exp(cum)).astype(BF16)
        ki = k * jnp.exp(-cum)
        kib = ki.astype(BF16)
        n_chunks = tile // CHUNK_A
        dl_rows = jnp.exp(jnp.concatenate(
            [cum[(c + 1) * CHUNK_A - 1:(c + 1) * CHUNK_A] for c in range(n_chunks)]
            + [jnp.zeros((HEAD - n_chunks, LANE_GROUP), F32)], axis=0))
        dl_cols = dl_rows.T
        heads = []
        for i in range(HEADS_PER_GROUP):
            ls = slice(i * HEAD, (i + 1) * HEAD)
            kit = ki[:, ls].T.astype(BF16)
            scores = [_dot(qdb[block(p), ls], kit[:, block(p)]) for p in range(n_blocks)]
            yield
            dls, us = [], []
            for c in range(n_chunks):
                lo, hi = c * CHUNK_A, (c + 1) * CHUNK_A
                ke = (ki[lo:hi, ls] * dl_rows[c:c + 1, ls]).astype(BF16)
                dls.append(dl_cols[ls, c:c + 1])
                us.append(_dot_tn(ke, vb[lo:hi, ls]))
                if c % chunks_per_block == chunks_per_block - 1:
                    yield
            heads.append((qdb[:, ls], vb[:, ls], scores, dls, us))
        scored[hp] = (heads, gate)

    def stage_output(hp):
        heads, gate = scored.pop(hp)
        ogs = []
        for i, (qdb, vb, scores, dls, us) in enumerate(heads):
            st = st_scr[HEADS_PER_GROUP * hp + i]
            os = []
            for p in range(n_blocks):
                o_intra = _dot(jnp.where(causal, scores[p], 0.0).astype(BF16), vb[block(p)])
                for cc in range(chunks_per_block):
                    c = p * chunks_per_block + cc
                    lo, hi = c * CHUNK_A, (c + 1) * CHUNK_A
                    os.append(o_intra[cc * CHUNK_A:(cc + 1) * CHUNK_A] + _dot(qdb[lo:hi], st.astype(BF16)))
                    st = st * dls[c] + us[c]
                yield
            st_scr[HEADS_PER_GROUP * hp + i] = st
            o = jnp.concatenate(os, axis=0)
            ogs.append(_rms_gate(o, gate[:, i * HEAD:(i + 1) * HEAD]))
        og_scr[hp] = jnp.concatenate(ogs, axis=1).astype(BF16)

    y_parts = []

    def out_proj(groups):
        y = _dot(og_scr[groups[0]], wo_ref[groups[0]])
        for hp in groups[1:]:
            yield
            y = y + _dot(og_scr[hp], wo_ref[hp])
        y_parts.append(y)

    for t in range(N_GROUPS + 2):
        stages = []
        if t < N_GROUPS:
            stages.append(stage_project(t))
        if t == N_GROUPS:
            stages.append(out_proj(list(range(0, N_GROUPS - 4))))
        if t == N_GROUPS + 1:
            stages.append(out_proj([N_GROUPS - 4, N_GROUPS - 3]))
        if 0 <= t - 1 < N_GROUPS:
            stages.append(stage_scores(t - 1))
        if 0 <= t - 2 < N_GROUPS:
            stages.append(stage_output(t - 2))
        _interleave(stages)
    _interleave([out_proj([N_GROUPS - 2, N_GROUPS - 1])])
    y = y_parts[0] + y_parts[1] + y_parts[2]
    h_scr[...] = ALPHA * x + y

    @pl.when(j == tiles_per_row - 1)
    def _():
        s_ref[0] = st_scr[...]


def _resident(shape):
    nd = len(shape)
    return pl.BlockSpec(shape, lambda *_: (0,) * nd, pipeline_mode=pl.Buffered(1))


def _weight_scratch(w_in, w_out):
    k, n = w_in.shape
    return [
        pltpu.VMEM((n // LANE_GROUP, k, LANE_GROUP), BF16),
        pltpu.VMEM((w_out.shape[0] // LANE_GROUP, LANE_GROUP, w_out.shape[1]), BF16),
        pltpu.VMEM((STAGE_SLOTS, STAGE_ROWS, LANE_GROUP), F32),
        pltpu.SemaphoreType.DMA((STAGE_SLOTS,)),
    ]


def _hgrn_prompt(x, w_in, lbl3, gn3, w_out, lng, lnb, layer):
    bsz, seq, _ = x.shape
    tiles_per_row = seq // HGRN_TILE
    n_tiles = bsz * tiles_per_row
    hbm = pl.BlockSpec(memory_space=pl.ANY)

    def tile_in(s):
        t = jnp.minimum(s, n_tiles - 1)
        return t // tiles_per_row, t % tiles_per_row

    def tile_out(s):
        t = jnp.maximum(s - 1, 0)
        return t // tiles_per_row, t % tiles_per_row

    return pl.pallas_call(
        functools.partial(_hgrn_prompt_kernel, layer=layer, tiles_per_row=tiles_per_row),
        grid=(n_tiles + 1,),
        in_specs=[
            pl.BlockSpec((1, HGRN_TILE, D_MODEL), lambda s: (*tile_in(s), 0)),
            hbm, _resident(lbl3.shape), _resident(gn3.shape), hbm,
            _resident(lng.shape), _resident(lnb.shape),
        ],
        out_specs=[
            pl.BlockSpec((1, HGRN_TILE, D_MODEL), lambda s: (*tile_out(s), 0)),
            pl.BlockSpec((1, N_HEADS, HEAD, HEAD), lambda s: (tile_in(s)[0], 0, 0, 0)),
        ],
        out_shape=[
            jax.ShapeDtypeStruct(x.shape, F32),
            jax.ShapeDtypeStruct((bsz, N_HEADS, HEAD, HEAD), F32),
        ],
        scratch_shapes=[
            pltpu.VMEM((N_HEADS, HEAD, HEAD), F32),
            pltpu.VMEM((N_GROUPS, HGRN_TILE, LANE_GROUP), BF16),
            pltpu.VMEM((HGRN_TILE, D_MODEL), F32),
            *_weight_scratch(w_in, w_out),
        ],
        compiler_params=pltpu.CompilerParams(
            dimension_semantics=("arbitrary",), vmem_limit_bytes=VMEM_LIMIT_BYTES),
        name="hgrn_prompt",
    )(x, w_in, lbl3, gn3, w_out, lng, lnb)


def _hgrn_sample_gates_kernel(x_ref, wq_ref, wf_ref, wi_ref, wg_ref, lbl_ref, gn_ref,
                              qd_ref, ki_ref, ke_ref, v_ref, gate_ref, dl_ref, *, layer, dec_seq):
    rows = x_ref.shape[0]
    xb = x_ref[...].astype(BF16)
    same, causal = _block_masks(ROW_TILE, dec_seq)
    tri = jnp.where(causal, 1.0, 0.0).astype(BF16)
    blk = jnp.where(same, 1.0, 0.0).astype(BF16)
    lb = _forget_lower_bound(lbl_ref[0], layer)

    qp = _dot(xb, wq_ref[...].astype(BF16))
    q = _silu(qp)
    z = _dot(xb, wf_ref[...].astype(BF16))
    c1 = 1.0 - lb
    cs = c1 / (1.0 + jnp.exp(-z))
    k = c1 - cs
    lf = jnp.log(lb + cs)
    for r in range(rows // ROW_TILE):
        rs = slice(r * ROW_TILE, (r + 1) * ROW_TILE)
        cum = _split_dot(tri, lf[rs])
        cum_last = _split_dot(blk, lf[rs])
        ki = k[rs] * jnp.exp(-cum)
        dl = jnp.exp(cum_last)
        qd_ref[0, rs] = q[rs] * jnp.exp(cum)
        ki_ref[0, rs] = ki
        ke_ref[0, rs] = ki * dl
        dl_ref[0, rs] = dl
    v_ref[0] = _dot(xb, wi_ref[...].astype(BF16))
    gate_ref[0] = _silu(_dot(xb, wg_ref[...].astype(BF16))) * gn_ref[0]


def _hgrn_sample_gates(x_rows, w3, lbl3, gn3, layer, dec_seq):
    rows = x_rows.shape[0]
    assert rows % ROW_TILE == 0 and SUBLANES % dec_seq == 0
    w_spec = lambda part: pl.BlockSpec((D_MODEL, LANE_GROUP), lambda g: (0, part * N_GROUPS + g))
    out_spec = pl.BlockSpec((1, rows, LANE_GROUP), lambda g: (g, 0, 0))
    out_shape = jax.ShapeDtypeStruct((N_GROUPS, rows, LANE_GROUP), F32)
    return pl.pallas_call(
        functools.partial(_hgrn_sample_gates_kernel, layer=layer, dec_seq=dec_seq),
        grid=(N_GROUPS,),
        in_specs=[
            pl.BlockSpec((rows, D_MODEL), lambda g: (0, 0)),
            w_spec(0), w_spec(1), w_spec(2), w_spec(3),
            pl.BlockSpec((1,) + lbl3.shape[1:], lambda g: (g, 0, 0)),
            pl.BlockSpec((1, 1, LANE_GROUP), lambda g: (g, 0, 0)),
        ],
        out_specs=[out_spec] * 6,
        out_shape=[out_shape] * 6,
        compiler_params=pltpu.CompilerParams(
            dimension_semantics=("arbitrary",), vmem_limit_bytes=VMEM_LIMIT_BYTES),
        name="hgrn_sample_gates",
    )(x_rows, w3, w3, w3, w3, lbl3, gn3)


def _decode_state_stages(qd_ref, ki_ref, ke_ref, v_ref, gate_ref, dl_ref, s_ref, og_ref, so_ref, dec_seq):
    seqs_per_tile = SUBLANES // dec_seq
    n_tiles = qd_ref.shape[1] // SUBLANES
    shift = dec_seq.bit_length() - 1
    trow = jnp.right_shift(lax.broadcasted_iota(jnp.int32, (SUBLANES, HEAD), 0), shift)
    in_seq = [trow == s for s in range(seqs_per_tile)]
    pad_rows = jnp.zeros((HEAD - N_HEADS, HEAD), F32)
    tiles = [slice(m * SUBLANES, (m + 1) * SUBLANES) for m in range(n_tiles)]
    dcols, pending = {}, {}
    lane_head = lax.broadcasted_iota(jnp.int32, (SUBLANES, LANE_GROUP), 1) // HEAD
    srow = lax.broadcasted_iota(jnp.int32, (SUBLANES, HEADS_PER_GROUP * SUBLANES), 0)
    scol = jnp.bitwise_and(lax.broadcasted_iota(jnp.int32, (SUBLANES, HEADS_PER_GROUP * SUBLANES), 1),
                           SUBLANES - 1)
    causal = jnp.logical_and(jnp.right_shift(srow, shift) == jnp.right_shift(scol, shift), scol <= srow)

    def per_head_rows(t):
        return jnp.concatenate([jnp.where(lane_head == i, t, 0.0) for i in range(HEADS_PER_GROUP)], axis=0)

    def per_seq_lanes(t):
        return jnp.concatenate([jnp.where(in_seq[s], t, 0.0) for s in range(seqs_per_tile)], axis=1)

    def decay_columns(m):
        dl = [dl_ref[hp, tiles[m], :] for hp in range(N_GROUPS)]
        for s in range(seqs_per_tile):
            r = s * dec_seq
            rows = [dl[h // HEADS_PER_GROUP][r:r + 1, (h % HEADS_PER_GROUP) * HEAD:(h % HEADS_PER_GROUP + 1) * HEAD]
                    for h in range(N_HEADS)]
            dcols[(m, s)] = jnp.concatenate(rows + [pad_rows], axis=0).T

    def start(hp):
        for m in range(n_tiles):
            if hp == 0:
                decay_columns(m)
            rs = tiles[m]
            qd = qd_ref[hp, rs, :]
            ke = ke_ref[hp, rs, :]
            v = v_ref[hp, rs, :]
            qdb = qd.astype(BF16)
            scores = _dot_nt(qdb, per_head_rows(ki_ref[hp, rs, :]).astype(BF16))
            o_inter = []
            for i in range(HEADS_PER_GROUP):
                ls = slice(i * HEAD, (i + 1) * HEAD)
                h = HEADS_PER_GROUP * hp + i
                sts = [s_ref[m * seqs_per_tile + s, h] for s in range(seqs_per_tile)]
                st_rows = jnp.concatenate([st.astype(BF16) for st in sts], axis=0)
                o_inter.append(_dot(per_seq_lanes(qd[:, ls]).astype(BF16), st_rows))
                u = _dot_tn(per_seq_lanes(ke[:, ls]).astype(BF16), v[:, ls].astype(BF16))
                for s in range(seqs_per_tile):
                    so_ref[m * seqs_per_tile + s, h] = (dcols[(m, s)][:, h:h + 1] * sts[s]
                                                       + u[s * HEAD:(s + 1) * HEAD])
            pending[(hp, m)] = (scores, jnp.concatenate(o_inter, axis=1), per_head_rows(v).astype(BF16))

    def finish(hp):
        for m in range(n_tiles):
            scores, o_inter, v_rows = pending.pop((hp, m))
            o = _dot(jnp.where(causal, scores, 0.0).astype(BF16), v_rows) + o_inter
            gate = gate_ref[hp, tiles[m], :]
            og_ref[hp, tiles[m], :] = jnp.concatenate(
                [_rms_gate(o[:, i * HEAD:(i + 1) * HEAD], gate[:, i * HEAD:(i + 1) * HEAD])
                 for i in range(HEADS_PER_GROUP)], axis=1)

    return start, finish


def _out_proj_ln_kernel(og_ref, x_ref, wo_ref, lng_ref, lnb_ref, y_ref):
    y = _dot(og_ref[0].astype(BF16), wo_ref[0].astype(BF16))
    for hp in range(1, N_GROUPS):
        y = y + _dot(og_ref[hp].astype(BF16), wo_ref[hp].astype(BF16))
    y_ref[...] = _layer_norm(ALPHA * x_ref[...] + y, lng_ref[...], lnb_ref[...])


def _out_proj_ln(og, x, wo3, lng, lnb):
    rows = x.shape[0]
    return pl.pallas_call(
        _out_proj_ln_kernel,
        grid=(rows // ROW_TILE,),
        in_specs=[
            pl.BlockSpec((N_GROUPS, ROW_TILE, LANE_GROUP), lambda r: (0, r, 0)),
            pl.BlockSpec((ROW_TILE, D_MODEL), lambda r: (r, 0)),
            _resident(wo3.shape), _resident(lng.shape), _resident(lnb.shape),
        ],
        out_specs=pl.BlockSpec((ROW_TILE, D_MODEL), lambda r: (r, 0)),
        out_shape=jax.ShapeDtypeStruct(x.shape, F32),
        compiler_params=pltpu.CompilerParams(
            dimension_semantics=("arbitrary",), vmem_limit_bytes=VMEM_LIMIT_BYTES),
        name="out_proj_ln",
    )(og, x, wo3, lng, lnb)


N_GMLP_INPUTS = 9
N_DECODE_INPUTS = 7


def _gmlp_kernel(*refs, emit_v, decode_seq, mix_block):
    refs = list(refs)
    gmlp_in = refs[:N_GMLP_INPUTS]
    del refs[:N_GMLP_INPUTS]
    decode_in = []
    if decode_seq:
        decode_in = refs[:N_DECODE_INPUTS]
        del refs[:N_DECODE_INPUTS]
    y_ref = refs.pop(0)
    vn_ref = refs.pop(0) if emit_v else None
    decode_out = [refs.pop(0), refs.pop(0)] if decode_seq else []
    v_scr, og_scr, h_scr, w_ref, wo_ref, sems = refs
    x_ref, w_hbm, vg_ref, vb_ref, ws_ref, bs_ref, wo_hbm, lng_ref, lnb_ref = gmlp_in
    s = pl.program_id(0)
    n_tiles = pl.num_programs(0) - 1

    @pl.when(s == 0)
    def _():
        _stage_projection_weights(w_hbm, wo_hbm, w_ref, wo_ref, v_scr, sems)
        h_scr[...] = jnp.zeros_like(h_scr)

    @pl.when(s < n_tiles)
    def _():
        decode = _decode_state_stages(*decode_in, *decode_out, decode_seq) if decode_seq else None
        _gmlp_tile_step(x_ref, w_ref, vg_ref, vb_ref, ws_ref, bs_ref, wo_ref, lng_ref, lnb_ref,
                        y_ref, vn_ref, v_scr, og_scr, h_scr, decode, mix_block)

    @pl.when(s == n_tiles)
    def _():
        y_ref[...] = _layer_norm(h_scr[...], lng_ref[...], lnb_ref[...])


def _gmlp_tile_step(x_ref, w_ref, vg_ref, vb_ref, ws_ref, bs_ref, wo_ref, lng_ref, lnb_ref,
                    y_ref, vn_ref, v_scr, og_scr, h_scr, decode, mix_block):
    y_ref[...] = _layer_norm(h_scr[...], lng_ref[...], lnb_ref[...])

    n_chunks = x_ref.shape[0] // CHUNK_B
    x = x_ref[...]
    xb = x.astype(BF16)

    half = x_ref.shape[0] // 2
    for gp in range(N_GROUPS):
        for r in range(2):
            rows = slice(r * half, (r + 1) * half)
            v_scr[gp, rows] = _gelu_tanh(_dot(xb[rows], w_ref[N_GROUPS + gp]))
        if decode is not None:
            decode[0](gp)

    s1 = jnp.sum(v_scr[0], axis=-1, keepdims=True)
    for gp in range(1, N_GROUPS):
        s1 = s1 + jnp.sum(v_scr[gp], axis=-1, keepdims=True)
    mu = s1 * (1.0 / E)
    s2 = jnp.zeros_like(mu)
    for gp in range(N_GROUPS):
        d = v_scr[gp] - mu
        s2 = s2 + jnp.sum(d * d, axis=-1, keepdims=True)
    rstd = lax.rsqrt(s2 * (1.0 / E) + LN_EPS)

    _, causal = _block_masks(CHUNK_B, mix_block)
    ws_reps = CHUNK_B // ws_ref.shape[1]
    n_mix_groups = bs_ref.shape[0]
    bias_cols = jnp.concatenate(
        [bs_ref[...], jnp.zeros((CHUNK_B - n_mix_groups, CHUNK_B), F32)], axis=0).T

    def stage_gate(gp):
        return _dot(xb, w_ref[gp]), _dot(xb, w_ref[2 * N_GROUPS + gp])

    def stage_mix(gp, gate):
        u_pre, z = gate
        vn = (v_scr[gp] - mu) * rstd * vg_ref[gp] + vb_ref[gp]
        if vn_ref is not None:
            vn_ref[:, gp * LANE_GROUP:(gp + 1) * LANE_GROUP] = vn
        vnb = vn.astype(BF16)
        cols = []
        for i in range(HEADS_PER_GROUP):
            ls = slice(i * HEAD, (i + 1) * HEAD)
            g = HEADS_PER_GROUP * gp + i
            wg = jnp.concatenate([ws_ref[g]] * ws_reps, axis=0) if ws_reps > 1 else ws_ref[g]
            wc = jnp.where(causal, wg, 0.0).astype(BF16)
            bias = bias_cols[:, g:g + 1]
            cols.append(jnp.concatenate(
                [_dot(wc, vnb[c * CHUNK_B:(c + 1) * CHUNK_B, ls]) + bias for c in range(n_chunks)], axis=0))
        mixed = jnp.concatenate(cols, axis=1)
        og_scr[gp] = (_gelu_tanh(u_pre) * mixed * _silu(z)).astype(BF16)

    def out_proj(groups):
        y = _dot(og_scr[groups[0]], wo_ref[groups[0]])
        for gp in groups[1:]:
            y = y + _dot(og_scr[gp], wo_ref[gp])
        return y

    y_parts = []
    gates = {0: stage_gate(0)}
    for gp in range(N_GROUPS):
        if gp + 1 < N_GROUPS:
            gates[gp + 1] = stage_gate(gp + 1)
        stage_mix(gp, gates.pop(gp))
        if decode is not None:
            decode[1](gp)
        if gp == N_GROUPS - 2:
            y_parts.append(out_proj(list(range(0, N_GROUPS - 3))))
    y_parts.append(out_proj([N_GROUPS - 3, N_GROUPS - 2]))
    y_parts.append(out_proj([N_GROUPS - 1]))
    y = y_parts[0] + y_parts[1] + y_parts[2]
    h_scr[...] = ALPHA * x + y


def _gmlp(x, w_in, vg3, vb3, ws, bs_rows, w_out, lng, lnb, emit_v, mix_block=CHUNK_B, decode=None,
          decode_seq=0):
    rows = x.shape[0]
    tile = min(GMLP_TILE, rows)
    assert tile == STAGE_ROWS and N_GROUPS == STAGE_SLOTS
    n_tiles = rows // tile
    tile_in = lambda s: jnp.minimum(s, n_tiles - 1)
    tile_out = lambda s: jnp.maximum(s - 1, 0)
    hbm = pl.BlockSpec(memory_space=pl.ANY)
    in_specs = [
        pl.BlockSpec((tile, D_MODEL), lambda s: (tile_in(s), 0)),
        hbm, _resident(vg3.shape), _resident(vb3.shape), _resident(ws.shape),
        _resident(bs_rows.shape), hbm, _resident(lng.shape), _resident(lnb.shape),
    ]
    operands = [x, w_in, vg3, vb3, ws, bs_rows, w_out, lng, lnb]
    w_scr, wo_scr, _, sems = _weight_scratch(w_in, w_out)
    assert len(operands) == N_GMLP_INPUTS
    out_specs = [pl.BlockSpec((tile, D_MODEL), lambda s: (tile_out(s), 0))]
    out_shape = [jax.ShapeDtypeStruct(x.shape, F32)]
    if emit_v:
        out_specs.append(pl.BlockSpec((tile, E), lambda s: (tile_in(s), 0)))
        out_shape.append(jax.ShapeDtypeStruct((rows, E), F32))
    if decode is not None:
        assert len(decode) == N_DECODE_INPUTS
        state = decode[-1]
        n_seq = state.shape[0]
        seqs_per_step = n_seq // n_tiles
        rows_per_step = seqs_per_step * decode_seq
        assert n_seq % n_tiles == 0 and rows_per_step % SUBLANES == 0 and SUBLANES % decode_seq == 0
        tok_spec = pl.BlockSpec((N_GROUPS, rows_per_step, LANE_GROUP), lambda s: (0, tile_in(s), 0))
        st_spec = pl.BlockSpec((seqs_per_step, N_HEADS, HEAD, HEAD), lambda s: (tile_in(s), 0, 0, 0))
        in_specs += [tok_spec] * (N_DECODE_INPUTS - 1) + [st_spec]
        operands += list(decode)
        out_specs += [tok_spec, st_spec]
        out_shape += [jax.ShapeDtypeStruct(decode[0].shape, F32), jax.ShapeDtypeStruct(state.shape, F32)]
    return pl.pallas_call(
        functools.partial(_gmlp_kernel, emit_v=emit_v, decode_seq=decode_seq if decode is not None else 0,
                          mix_block=mix_block),
        grid=(n_tiles + 1,),
        in_specs=in_specs,
        out_specs=out_specs,
        out_shape=out_shape,
        scratch_shapes=[
            pltpu.VMEM((N_GROUPS, tile, LANE_GROUP), F32),
            pltpu.VMEM((N_GROUPS, tile, LANE_GROUP), BF16),
            pltpu.VMEM((tile, D_MODEL), F32),
            w_scr, wo_scr, sems,
        ],
        compiler_params=pltpu.CompilerParams(
            dimension_semantics=("arbitrary",), vmem_limit_bytes=VMEM_LIMIT_BYTES),
        name="gmlp_emit_v" if emit_v else "gmlp",
    )(*operands)


def _lane_groups_of_rows(w):
    return w.reshape(w.shape[0] // LANE_GROUP, LANE_GROUP, w.shape[1])


def _lane_groups_of_vector(v):
    return v.reshape(v.shape[0], N_GROUPS, LANE_GROUP).transpose(1, 0, 2)


def kernel(x_prompt, x_sample, state_hgrn, w_in_a, lb_logits_a, gnorm_a, w_out_a, w_in_b, lnv_g_b,
           lnv_b_b, w_s_b, b_s_b, w_out_b, ln_g, ln_b):
    bsz, seq, _ = x_prompt.shape
    n_seq, dec_seq, _ = x_sample.shape

    wa, woa = w_in_a[0], w_out_a[0]
    lbl3 = _lane_groups_of_vector(lb_logits_a)
    gn3 = _lane_groups_of_vector(gnorm_a[0:1])
    lng0, lnb0 = ln_g[0:1], ln_b[0:1]

    hp, sp = _hgrn_prompt(x_prompt, wa, lbl3, gn3, woa, lng0, lnb0, layer=0)

    xs = x_sample.reshape(n_seq * dec_seq, D_MODEL)
    decode_gates = _hgrn_sample_gates(xs, wa, lbl3, gn3, layer=0, dec_seq=dec_seq)

    w3b, wo3b = w_in_b[0], w_out_b[0]
    vg3 = _lane_groups_of_vector(lnv_g_b[0:1])
    vb3 = _lane_groups_of_vector(lnv_b_b[0:1])
    lng1, lnb1 = ln_g[1:2], ln_b[1:2]
    ws = w_s_b[0]
    bs = b_s_b[0]

    yp, og, ss = _gmlp(hp.reshape(bsz * seq, D_MODEL), w3b, vg3, vb3, ws, bs, wo3b, lng1, lnb1,
                       emit_v=False, decode=(*decode_gates, state_hgrn[0]), decode_seq=dec_seq)
    yp = yp.reshape(bsz, seq, D_MODEL)
    hs = _out_proj_ln(og, xs, _lane_groups_of_rows(woa), lng0, lnb0)

    reps = CHUNK_B // dec_seq
    ws_dec = jnp.tile(ws[:, :dec_seq, :dec_seq], (1, SUBLANES // dec_seq, reps))
    bs_dec = jnp.tile(bs[:, :dec_seq], (1, reps))
    ys, vn = _gmlp(hs, w3b, vg3, vb3, ws_dec, bs_dec, wo3b, lng1, lnb1, emit_v=True, mix_block=dec_seq)
    ys = ys.reshape(n_seq, dec_seq, D_MODEL)
    vs = vn.reshape(n_seq, dec_seq, E)

    return (yp, ys, sp[None], ss[None], vs[None])
```

```python
import functools
import math

import jax
import jax.numpy as jnp
from jax import lax
from jax.experimental import pallas as pl
from jax.experimental.pallas import tpu as pltpu

F32 = jnp.float32
BF16 = jnp.bfloat16

D_MODEL = 1024
E = 2048
HEAD = 128
N_HEADS = E // HEAD
LANE_GROUP = 256
N_GROUPS = E // LANE_GROUP
HEADS_PER_GROUP = LANE_GROUP // HEAD
CHUNK_A = 64
CHUNK_B = 128
ROW_TILE = 256
HGRN_TILE = 512
HGRN_BLOCK = 2 * CHUNK_A
GMLP_TILE = 512
SUBLANES = 8
DEPTH = 2
ALPHA = (2 * DEPTH) ** 0.25
LN_EPS = 1e-5
VMEM_LIMIT_BYTES = 60000 * 1024


def _dot(a, b):
    return jnp.dot(a, b, preferred_element_type=F32)


STAGE_ROWS = 512
STAGE_SLOTS = 8


def _stage_weight(w_hbm, stage, sems, store):
    n_row_blocks = w_hbm.shape[0] // STAGE_ROWS
    n_col_blocks = w_hbm.shape[1] // LANE_GROUP
    chunks = [(r, c) for c in range(n_col_blocks) for r in range(n_row_blocks)]

    def copy(i):
        r, c = chunks[i]
        slot = i % STAGE_SLOTS
        return pltpu.make_async_copy(
            w_hbm.at[pl.ds(r * STAGE_ROWS, STAGE_ROWS), pl.ds(c * LANE_GROUP, LANE_GROUP)],
            stage.at[slot], sems.at[slot])

    for i in range(min(STAGE_SLOTS, len(chunks))):
        copy(i).start()
    for i, (r, c) in enumerate(chunks):
        copy(i).wait()
        store(r, c, stage[i % STAGE_SLOTS].astype(BF16))
        if i + STAGE_SLOTS < len(chunks):
            copy(i + STAGE_SLOTS).start()


def _stage_projection_weights(w_in_hbm, w_out_hbm, w_scr, wo_scr, stage, sems):
    def store_in(r, c, chunk):
        w_scr[c, r * STAGE_ROWS:(r + 1) * STAGE_ROWS, :] = chunk

    def store_out(r, c, chunk):
        for k in range(STAGE_ROWS // LANE_GROUP):
            wo_scr[r * (STAGE_ROWS // LANE_GROUP) + k, :, c * LANE_GROUP:(c + 1) * LANE_GROUP] = (
                chunk[k * LANE_GROUP:(k + 1) * LANE_GROUP])

    _stage_weight(w_in_hbm, stage, sems, store_in)
    _stage_weight(w_out_hbm, stage, sems, store_out)


def _interleave(stages):
    stages = list(stages)
    while stages:
        for stage in list(stages):
            if next(stage, stages) is stages:
                stages.remove(stage)


def _dot_nt(a, b):
    return lax.dot_general(a, b, (((1,), (1,)), ((), ())), preferred_element_type=F32)


def _dot_tn(a, b):
    return lax.dot_general(a, b, (((0,), (0,)), ((), ())), preferred_element_type=F32)


def _silu(x):
    return x / (1.0 + jnp.exp(-x))


def _gelu_tanh(x):
    cdf = 0.5 * (1.0 + jnp.tanh(math.sqrt(2.0 / math.pi) * (x + 0.044715 * (x * x * x))))
    return x * cdf


def _layer_norm(x, g, b):
    mu = jnp.mean(x, axis=-1, keepdims=True)
    d = x - mu
    var = jnp.mean(d * d, axis=-1, keepdims=True)
    return d * lax.rsqrt(var + LN_EPS) * g + b


def _block_masks(n, block):
    shift = block.bit_length() - 1
    row = lax.broadcasted_iota(jnp.int32, (n, n), 0)
    col = lax.broadcasted_iota(jnp.int32, (n, n), 1)
    same = jnp.right_shift(row, shift) == jnp.right_shift(col, shift)
    causal = jnp.logical_and(same, col <= row)
    return same, causal


def _split_dot(m, x):
    hi = x.astype(BF16)
    lo = (x - hi.astype(F32)).astype(BF16)
    return _dot(jnp.concatenate([m, m], axis=1), jnp.concatenate([hi, lo], axis=0))


def _forget_lower_bound(logits, layer):
    m = jnp.max(logits, axis=0, keepdims=True)
    e = jnp.exp(logits - m)
    den = jnp.sum(e, axis=0, keepdims=True)
    num = jnp.sum(e[: layer + 1], axis=0, keepdims=True)
    return num / den


def _hgrn_gates(xb, w_q, w_f, lb, tri, valid):
    qp = _dot(xb, w_q)
    q = _silu(qp)
    z = _dot(xb, w_f)
    c1 = 1.0 - lb
    cs = c1 / (1.0 + jnp.exp(-z))
    f = lb + cs
    k = c1 - cs
    lf = jnp.log(f)
    if valid is not None:
        lf = jnp.where(valid, lf, 0.0)
        k = jnp.where(valid, k, 0.0)
    cum = _split_dot(tri, lf)
    return q * jnp.exp(cum), k * jnp.exp(-cum), cum, lf


def _rms_gate(o, gate):
    ms = jnp.mean(o * o, axis=-1, keepdims=True)
    return o * lax.rsqrt(ms + LN_EPS) * gate


def _hgrn_prompt_kernel(x_ref, w_hbm, lbl_ref, gn_ref, wo_hbm, lng_ref, lnb_ref,
                        y_ref, s_ref, st_scr, og_scr, h_scr, w_ref, wo_ref, stage, sems,
                        *, layer, tiles_per_row):
    s = pl.program_id(0)
    n_tiles = pl.num_programs(0) - 1

    @pl.when(s == 0)
    def _():
        _stage_projection_weights(w_hbm, wo_hbm, w_ref, wo_ref, stage, sems)
        h_scr[...] = jnp.zeros_like(h_scr)

    @pl.when(s < n_tiles)
    def _():
        _hgrn_tile_step(lax.rem(s, tiles_per_row), tiles_per_row, x_ref, w_ref, lbl_ref, gn_ref, wo_ref,
                        lng_ref, lnb_ref, y_ref, s_ref, st_scr, og_scr, h_scr, layer=layer)

    @pl.when(s == n_tiles)
    def _():
        y_ref[0] = _layer_norm(h_scr[...], lng_ref[...], lnb_ref[...])


def _hgrn_tile_step(j, tiles_per_row, x_ref, w_ref, lbl_ref, gn_ref, wo_ref, lng_ref, lnb_ref,
                    y_ref, s_ref, st_scr, og_scr, h_scr, *, layer):
    tile = x_ref.shape[1]
    n_blocks = tile // HGRN_BLOCK
    n_chunks = tile // CHUNK_A
    chunks_per_block = HGRN_BLOCK // CHUNK_A

    @pl.when(j == 0)
    def _():
        st_scr[...] = jnp.zeros_like(st_scr)

    y_ref[0] = _layer_norm(h_scr[...], lng_ref[...], lnb_ref[...])

    x = x_ref[0]
    xb = x.astype(BF16)
    _, causal = _block_masks(HGRN_BLOCK, CHUNK_A)
    tri = jnp.where(causal, 1.0, 0.0).astype(BF16)

    def block(p):
        return slice(p * HGRN_BLOCK, (p + 1) * HGRN_BLOCK)

    projected, scored = {}, {}

    def stage_project(hp):
        z = _dot(xb, w_ref[N_GROUPS + hp])
        yield
        qp = _dot(xb, w_ref[hp])
        yield
        v = _dot(xb, w_ref[2 * N_GROUPS + hp])
        yield
        g = _dot(xb, w_ref[3 * N_GROUPS + hp])
        yield
        lb = _forget_lower_bound(lbl_ref[hp], layer)
        c1 = 1.0 - lb
        cs = c1 / (1.0 + jnp.exp(-z))
        k = c1 - cs
        lf = jnp.log(lb + cs)
        cum = jnp.concatenate([_split_dot(tri, lf[block(p)]) for p in range(n_blocks)], axis=0)
        projected[hp] = (_silu(qp), k, cum, v.astype(BF16), _silu(g) * gn_ref[hp])

    def stage_scores(hp):
        q, k, cum, vb, gate = projected.pop(hp)
        qdb = (q * jnp.exp(cum)).astype(BF16)
        ki = k * jnp.exp(-cum)
        dl_rows = jnp.exp(jnp.concatenate(
            [cum[(c + 1) * CHUNK_A - 1:(c + 1) * CHUNK_A] for c in range(n_chunks)]
            + [jnp.zeros((HEAD - n_chunks, LANE_GROUP), F32)], axis=0))
        dl_cols = dl_rows.T
        heads = []
        for i in range(HEADS_PER_GROUP):
            ls = slice(i * HEAD, (i + 1) * HEAD)
            kit = ki[:, ls].T.astype(BF16)
            scores = [_dot(qdb[block(p), ls], kit[:, block(p)]) for p in range(n_blocks)]
            yield
            dls, us = [], []
            for c in range(n_chunks):
                lo, hi = c * CHUNK_A, (c + 1) * CHUNK_A
                ke = (ki[lo:hi, ls] * dl_rows[c:c + 1, ls]).astype(BF16)
                dls.append(dl_cols[ls, c:c + 1])
                us.append(_dot_tn(ke, vb[lo:hi, ls]))
                if c % chunks_per_block == chunks_per_block - 1:
                    yield
            heads.append((qdb[:, ls], vb[:, ls], scores, dls, us))
        scored[hp] = (heads, gate)

    def stage_output(hp):
        heads, gate = scored.pop(hp)
        ogs = []
        for i, (qdb, vb, scores, dls, us) in enumerate(heads):
            st = st_scr[HEADS_PER_GROUP * hp + i]
            os = []
            for p in range(n_blocks):
                o_intra = _dot(jnp.where(causal, scores[p], 0.0).astype(BF16), vb[block(p)])
                for cc in range(chunks_per_block):
                    c = p * chunks_per_block + cc
                    lo, hi = c * CHUNK_A, (c + 1) * CHUNK_A
                    os.append(o_intra[cc * CHUNK_A:(cc + 1) * CHUNK_A] + _dot(qdb[lo:hi], st.astype(BF16)))
                    st = st * dls[c] + us[c]
                yield
            st_scr[HEADS_PER_GROUP * hp + i] = st
            o = jnp.concatenate(os, axis=0)
            ogs.append(_rms_gate(o, gate[:, i * HEAD:(i + 1) * HEAD]))
        og_scr[hp] = jnp.concatenate(ogs, axis=1).astype(BF16)

    y_parts = []

    def out_proj(groups):
        y = _dot(og_scr[groups[0]], wo_ref[groups[0]])
        for hp in groups[1:]:
            yield
            y = y + _dot(og_scr[hp], wo_ref[hp])
        y_parts.append(y)

    for t in range(N_GROUPS + 2):
        stages = []
        if t < N_GROUPS:
            stages.append(stage_project(t))
        if t == N_GROUPS:
            stages.append(out_proj(list(range(0, N_GROUPS - 4))))
        if t == N_GROUPS + 1:
            stages.append(out_proj([N_GROUPS - 4, N_GROUPS - 3]))
        if 0 <= t - 1 < N_GROUPS:
            stages.append(stage_scores(t - 1))
        if 0 <= t - 2 < N_GROUPS:
            stages.append(stage_output(t - 2))
        _interleave(stages)
    _interleave([out_proj([N_GROUPS - 2, N_GROUPS - 1])])
    y = y_parts[0] + y_parts[1] + y_parts[2]
    h_scr[...] = ALPHA * x + y

    @pl.when(j == tiles_per_row - 1)
    def _():
        s_ref[0] = st_scr[...]


def _resident(shape):
    nd = len(shape)
    return pl.BlockSpec(shape, lambda *_: (0,) * nd, pipeline_mode=pl.Buffered(1))


def _weight_scratch(w_in, w_out):
    k, n = w_in.shape
    return [
        pltpu.VMEM((n // LANE_GROUP, k, LANE_GROUP), BF16),
        pltpu.VMEM((w_out.shape[0] // LANE_GROUP, LANE_GROUP, w_out.shape[1]), BF16),
        pltpu.VMEM((STAGE_SLOTS, STAGE_ROWS, LANE_GROUP), F32),
        pltpu.SemaphoreType.DMA((STAGE_SLOTS,)),
    ]


def _hgrn_prompt(x, w_in, lbl3, gn3, w_out, lng, lnb, layer):
    bsz, seq, _ = x.shape
    tiles_per_row = seq // HGRN_TILE
    n_tiles = bsz * tiles_per_row
    hbm = pl.BlockSpec(memory_space=pl.ANY)

    def tile_in(s):
        t = jnp.minimum(s, n_tiles - 1)
        return t // tiles_per_row, t % tiles_per_row

    def tile_out(s):
        t = jnp.maximum(s - 1, 0)
        return t // tiles_per_row, t % tiles_per_row

    return pl.pallas_call(
        functools.partial(_hgrn_prompt_kernel, layer=layer, tiles_per_row=tiles_per_row),
        grid=(n_tiles + 1,),
        in_specs=[
            pl.BlockSpec((1, HGRN_TILE, D_MODEL), lambda s: (*tile_in(s), 0)),
            hbm, _resident(lbl3.shape), _resident(gn3.shape), hbm,
            _resident(lng.shape), _resident(lnb.shape),
        ],
        out_specs=[
            pl.BlockSpec((1, HGRN_TILE, D_MODEL), lambda s: (*tile_out(s), 0)),
            pl.BlockSpec((1, N_HEADS, HEAD, HEAD), lambda s: (tile_in(s)[0], 0, 0, 0)),
        ],
        out_shape=[
            jax.ShapeDtypeStruct(x.shape, F32),
            jax.ShapeDtypeStruct((bsz, N_HEADS, HEAD, HEAD), F32),
        ],
        scratch_shapes=[
            pltpu.VMEM((N_HEADS, HEAD, HEAD), F32),
            pltpu.VMEM((N_GROUPS, HGRN_TILE, LANE_GROUP), BF16),
            pltpu.VMEM((HGRN_TILE, D_MODEL), F32),
            *_weight_scratch(w_in, w_out),
        ],
        compiler_params=pltpu.CompilerParams(
            dimension_semantics=("arbitrary",), vmem_limit_bytes=VMEM_LIMIT_BYTES),
        name="hgrn_prompt",
    )(x, w_in, lbl3, gn3, w_out, lng, lnb)


def _hgrn_sample_gates_kernel(x_ref, wq_ref, wf_ref, wi_ref, wg_ref, lbl_ref, gn_ref,
                              qd_ref, ki_ref, ke_ref, v_ref, gate_ref, dl_ref, *, layer, dec_seq):
    rows = x_ref.shape[0]
    xb = x_ref[...].astype(BF16)
    same, causal = _block_masks(ROW_TILE, dec_seq)
    tri = jnp.where(causal, 1.0, 0.0).astype(BF16)
    blk = jnp.where(same, 1.0, 0.0).astype(BF16)
    lb = _forget_lower_bound(lbl_ref[0], layer)

    qp = _dot(xb, wq_ref[...].astype(BF16))
    q = _silu(qp)
    z = _dot(xb, wf_ref[...].astype(BF16))
    c1 = 1.0 - lb
    cs = c1 / (1.0 + jnp.exp(-z))
    k = c1 - cs
    lf = jnp.log(lb + cs)
    for r in range(rows // ROW_TILE):
        rs = slice(r * ROW_TILE, (r + 1) * ROW_TILE)
        cum = _split_dot(tri, lf[rs])
        cum_last = _split_dot(blk, lf[rs])
        ki = k[rs] * jnp.exp(-cum)
        dl = jnp.exp(cum_last)
        qd_ref[0, rs] = q[rs] * jnp.exp(cum)
        ki_ref[0, rs] = ki
        ke_ref[0, rs] = ki * dl
        dl_ref[0, rs] = dl
    v_ref[0] = _dot(xb, wi_ref[...].astype(BF16))
    gate_ref[0] = _silu(_dot(xb, wg_ref[...].astype(BF16))) * gn_ref[0]


def _hgrn_sample_gates(x_rows, w3, lbl3, gn3, layer, dec_seq):
    rows = x_rows.shape[0]
    assert rows % ROW_TILE == 0 and SUBLANES % dec_seq == 0
    w_spec = lambda part: pl.BlockSpec((D_MODEL, LANE_GROUP), lambda g: (0, part * N_GROUPS + g))
    out_spec = pl.BlockSpec((1, rows, LANE_GROUP), lambda g: (g, 0, 0))
    out_shape = jax.ShapeDtypeStruct((N_GROUPS, rows, LANE_GROUP), F32)
    return pl.pallas_call(
        functools.partial(_hgrn_sample_gates_kernel, layer=layer, dec_seq=dec_seq),
        grid=(N_GROUPS,),
        in_specs=[
            pl.BlockSpec((rows, D_MODEL), lambda g: (0, 0)),
            w_spec(0), w_spec(1), w_spec(2), w_spec(3),
            pl.BlockSpec((1,) + lbl3.shape[1:], lambda g: (g, 0, 0)),
            pl.BlockSpec((1, 1, LANE_GROUP), lambda g: (g, 0, 0)),
        ],
        out_specs=[out_spec] * 6,
        out_shape=[out_shape] * 6,
        compiler_params=pltpu.CompilerParams(
            dimension_semantics=("arbitrary",), vmem_limit_bytes=VMEM_LIMIT_BYTES),
        name="hgrn_sample_gates",
    )(x_rows, w3, w3, w3, w3, lbl3, gn3)


def _decode_state_stages(qd_ref, ki_ref, ke_ref, v_ref, gate_ref, dl_ref, s_ref, og_ref, so_ref, dec_seq):
    seqs_per_tile = SUBLANES // dec_seq
    n_tiles = qd_ref.shape[1] // SUBLANES
    shift = dec_seq.bit_length() - 1
    trow = jnp.right_shift(lax.broadcasted_iota(jnp.int32, (SUBLANES, HEAD), 0), shift)
    in_seq = [trow == s for s in range(seqs_per_tile)]
    pad_rows = jnp.zeros((HEAD - N_HEADS, HEAD), F32)
    tiles = [slice(m * SUBLANES, (m + 1) * SUBLANES) for m in range(n_tiles)]
    dcols, pending = {}, {}
    lane_head = lax.broadcasted_iota(jnp.int32, (SUBLANES, LANE_GROUP), 1) // HEAD
    srow = lax.broadcasted_iota(jnp.int32, (SUBLANES, HEADS_PER_GROUP * SUBLANES), 0)
    scol = jnp.bitwise_and(lax.broadcasted_iota(jnp.int32, (SUBLANES, HEADS_PER_GROUP * SUBLANES), 1),
                           SUBLANES - 1)
    causal = jnp.logical_and(jnp.right_shift(srow, shift) == jnp.right_shift(scol, shift), scol <= srow)

    def per_head_rows(t):
        return jnp.concatenate([jnp.where(lane_head == i, t, 0.0) for i in range(HEADS_PER_GROUP)], axis=0)

    def per_seq_lanes(t):
        return jnp.concatenate([jnp.where(in_seq[s], t, 0.0) for s in range(seqs_per_tile)], axis=1)

    def decay_columns(m):
        dl = [dl_ref[hp, tiles[m], :] for hp in range(N_GROUPS)]
        for s in range(seqs_per_tile):
            r = s * dec_seq
            rows = [dl[h // HEADS_PER_GROUP][r:r + 1, (h % HEADS_PER_GROUP) * HEAD:(h % HEADS_PER_GROUP + 1) * HEAD]
                    for h in range(N_HEADS)]
            dcols[(m, s)] = jnp.concatenate(rows + [pad_rows], axis=0).T

    def start(hp):
        for m in range(n_tiles):
            if hp == 0:
                decay_columns(m)
            rs = tiles[m]
            qd = qd_ref[hp, rs, :]
            ke = ke_ref[hp, rs, :]
            v = v_ref[hp, rs, :]
            qdb = qd.astype(BF16)
            scores = _dot_nt(qdb, per_head_rows(ki_ref[hp, rs, :]).astype(BF16))
            o_inter = []
            for i in range(HEADS_PER_GROUP):
                ls = slice(i * HEAD, (i + 1) * HEAD)
                h = HEADS_PER_GROUP * hp + i
                sts = [s_ref[m * seqs_per_tile + s, h] for s in range(seqs_per_tile)]
                st_rows = jnp.concatenate([st.astype(BF16) for st in sts], axis=0)
                o_inter.append(_dot(per_seq_lanes(qd[:, ls]).astype(BF16), st_rows))
                u = _dot_tn(per_seq_lanes(ke[:, ls]).astype(BF16), v[:, ls].astype(BF16))
                for s in range(seqs_per_tile):
                    so_ref[m * seqs_per_tile + s, h] = (dcols[(m, s)][:, h:h + 1] * sts[s]
                                                       + u[s * HEAD:(s + 1) * HEAD])
            pending[(hp, m)] = (scores, jnp.concatenate(o_inter, axis=1), per_head_rows(v).astype(BF16))
            yield

    def finish(hp):
        for m in range(n_tiles):
            scores, o_inter, v_rows = pending.pop((hp, m))
            o = _dot(jnp.where(causal, scores, 0.0).astype(BF16), v_rows) + o_inter
            gate = gate_ref[hp, tiles[m], :]
            og_ref[hp, tiles[m], :] = jnp.concatenate(
                [_rms_gate(o[:, i * HEAD:(i + 1) * HEAD], gate[:, i * HEAD:(i + 1) * HEAD])
                 for i in range(HEADS_PER_GROUP)], axis=1)

    return start, finish


def _out_proj_ln_kernel(og_ref, x_ref, wo_ref, lng_ref, lnb_ref, y_ref):
    y = _dot(og_ref[0].astype(BF16), wo_ref[0].astype(BF16))
    for hp in range(1, N_GROUPS):
        y = y + _dot(og_ref[hp].astype(BF16), wo_ref[hp].astype(BF16))
    y_ref[...] = _layer_norm(ALPHA * x_ref[...] + y, lng_ref[...], lnb_ref[...])


def _out_proj_ln(og, x, wo3, lng, lnb):
    rows = x.shape[0]
    return pl.pallas_call(
        _out_proj_ln_kernel,
        grid=(rows // ROW_TILE,),
        in_specs=[
            pl.BlockSpec((N_GROUPS, ROW_TILE, LANE_GROUP), lambda r: (0, r, 0)),
            pl.BlockSpec((ROW_TILE, D_MODEL), lambda r: (r, 0)),
            _resident(wo3.shape), _resident(lng.shape), _resident(lnb.shape),
        ],
        out_specs=pl.BlockSpec((ROW_TILE, D_MODEL), lambda r: (r, 0)),
        out_shape=jax.ShapeDtypeStruct(x.shape, F32),
        compiler_params=pltpu.CompilerParams(
            dimension_semantics=("arbitrary",), vmem_limit_bytes=VMEM_LIMIT_BYTES),
        name="out_proj_ln",
    )(og, x, wo3, lng, lnb)


N_GMLP_INPUTS = 9
N_DECODE_INPUTS = 7


def _gmlp_kernel(*refs, emit_v, decode_seq, mix_block):
    refs = list(refs)
    gmlp_in = refs[:N_GMLP_INPUTS]
    del refs[:N_GMLP_INPUTS]
    decode_in = []
    if decode_seq:
        decode_in = refs[:N_DECODE_INPUTS]
        del refs[:N_DECODE_INPUTS]
    y_ref = refs.pop(0)
    vn_ref = refs.pop(0) if emit_v else None
    decode_out = [refs.pop(0), refs.pop(0)] if decode_seq else []
    v_scr, og_scr, h_scr, w_ref, wo_ref, sems = refs
    x_ref, w_hbm, vg_ref, vb_ref, ws_ref, bs_ref, wo_hbm, lng_ref, lnb_ref = gmlp_in
    s = pl.program_id(0)
    n_tiles = pl.num_programs(0) - 1

    @pl.when(s == 0)
    def _():
        _stage_projection_weights(w_hbm, wo_hbm, w_ref, wo_ref, v_scr, sems)
        h_scr[...] = jnp.zeros_like(h_scr)

    @pl.when(s < n_tiles)
    def _():
        decode = _decode_state_stages(*decode_in, *decode_out, decode_seq) if decode_seq else None
        _gmlp_tile_step(x_ref, w_ref, vg_ref, vb_ref, ws_ref, bs_ref, wo_ref, lng_ref, lnb_ref,
                        y_ref, vn_ref, v_scr, og_scr, h_scr, decode, mix_block)

    @pl.when(s == n_tiles)
    def _():
        y_ref[...] = _layer_norm(h_scr[...], lng_ref[...], lnb_ref[...])


def _gmlp_tile_step(x_ref, w_ref, vg_ref, vb_ref, ws_ref, bs_ref, wo_ref, lng_ref, lnb_ref,
                    y_ref, vn_ref, v_scr, og_scr, h_scr, decode, mix_block):
    y_ref[...] = _layer_norm(h_scr[...], lng_ref[...], lnb_ref[...])

    n_chunks = x_ref.shape[0] // CHUNK_B
    x = x_ref[...]
    xb = x.astype(BF16)

    half = x_ref.shape[0] // 2

    def stage_v(gp):
        for r in range(2):
            rows = slice(r * half, (r + 1) * half)
            v_scr[gp, rows] = _gelu_tanh(_dot(xb[rows], w_ref[N_GROUPS + gp]))
            yield

    for gp in range(N_GROUPS):
        _interleave([stage_v(gp)] + ([decode[0](gp)] if decode is not None else []))

    s1 = jnp.sum(v_scr[0], axis=-1, keepdims=True)
    for gp in range(1, N_GROUPS):
        s1 = s1 + jnp.sum(v_scr[gp], axis=-1, keepdims=True)
    mu = s1 * (1.0 / E)
    s2 = jnp.zeros_like(mu)
    for gp in range(N_GROUPS):
        d = v_scr[gp] - mu
        s2 = s2 + jnp.sum(d * d, axis=-1, keepdims=True)
    rstd = lax.rsqrt(s2 * (1.0 / E) + LN_EPS)

    _, causal = _block_masks(CHUNK_B, mix_block)
    ws_reps = CHUNK_B // ws_ref.shape[1]
    n_mix_groups = bs_ref.shape[0]
    bias_cols = jnp.concatenate(
        [bs_ref[...], jnp.zeros((CHUNK_B - n_mix_groups, CHUNK_B), F32)], axis=0).T

    def stage_gate(gp):
        return _dot(xb, w_ref[gp]), _dot(xb, w_ref[2 * N_GROUPS + gp])

    def stage_mix(gp, gate):
        u_pre, z = gate
        vn = (v_scr[gp] - mu) * rstd * vg_ref[gp] + vb_ref[gp]
        if vn_ref is not None:
            vn_ref[:, gp * LANE_GROUP:(gp + 1) * LANE_GROUP] = vn
        vnb = vn.astype(BF16)
        cols = []
        for i in range(HEADS_PER_GROUP):
            ls = slice(i * HEAD, (i + 1) * HEAD)
            g = HEADS_PER_GROUP * gp + i
            wg = jnp.concatenate([ws_ref[g]] * ws_reps, axis=0) if ws_reps > 1 else ws_ref[g]
            wc = jnp.where(causal, wg, 0.0).astype(BF16)
            bias = bias_cols[:, g:g + 1]
            side_by_side = jnp.concatenate(
                [vnb[c * CHUNK_B:(c + 1) * CHUNK_B, ls] for c in range(n_chunks)], axis=1)
            m = _dot(wc, side_by_side) + bias
            cols.append(jnp.concatenate([m[:, c * HEAD:(c + 1) * HEAD] for c in range(n_chunks)], axis=0))
        mixed = jnp.concatenate(cols, axis=1)
        og_scr[gp] = (_gelu_tanh(u_pre) * mixed * _silu(z)).astype(BF16)

    def out_proj(groups):
        y = _dot(og_scr[groups[0]], wo_ref[groups[0]])
        for gp in groups[1:]:
            y = y + _dot(og_scr[gp], wo_ref[gp])
        return y

    y_parts = []
    gates = {0: stage_gate(0)}
    for gp in range(N_GROUPS):
        if gp + 1 < N_GROUPS:
            gates[gp + 1] = stage_gate(gp + 1)
        stage_mix(gp, gates.pop(gp))
        if decode is not None:
            decode[1](gp)
        if gp == N_GROUPS - 2:
            y_parts.append(out_proj(list(range(0, N_GROUPS - 3))))
    y_parts.append(out_proj([N_GROUPS - 3, N_GROUPS - 2]))
    y_parts.append(out_proj([N_GROUPS - 1]))
    y = y_parts[0] + y_parts[1] + y_parts[2]
    h_scr[...] = ALPHA * x + y


def _gmlp(x, w_in, vg3, vb3, ws, bs_rows, w_out, lng, lnb, emit_v, mix_block=CHUNK_B, decode=None,
          decode_seq=0):
    rows = x.shape[0]
    tile = min(GMLP_TILE, rows)
    assert tile == STAGE_ROWS and N_GROUPS == STAGE_SLOTS
    n_tiles = rows // tile
    tile_in = lambda s: jnp.minimum(s, n_tiles - 1)
    tile_out = lambda s: jnp.maximum(s - 1, 0)
    hbm = pl.BlockSpec(memory_space=pl.ANY)
    in_specs = [
        pl.BlockSpec((tile, D_MODEL), lambda s: (tile_in(s), 0)),
        hbm, _resident(vg3.shape), _resident(vb3.shape), _resident(ws.shape),
        _resident(bs_rows.shape), hbm, _resident(lng.shape), _resident(lnb.shape),
    ]
    operands = [x, w_in, vg3, vb3, ws, bs_rows, w_out, lng, lnb]
    w_scr, wo_scr, _, sems = _weight_scratch(w_in, w_out)
    assert len(operands) == N_GMLP_INPUTS
    out_specs = [pl.BlockSpec((tile, D_MODEL), lambda s: (tile_out(s), 0))]
    out_shape = [jax.ShapeDtypeStruct(x.shape, F32)]
    if emit_v:
        out_specs.append(pl.BlockSpec((tile, E), lambda s: (tile_in(s), 0)))
        out_shape.append(jax.ShapeDtypeStruct((rows, E), F32))
    if decode is not None:
        assert len(decode) == N_DECODE_INPUTS
        state = decode[-1]
        n_seq = state.shape[0]
        seqs_per_step = n_seq // n_tiles
        rows_per_step = seqs_per_step * decode_seq
        assert n_seq % n_tiles == 0 and rows_per_step % SUBLANES == 0 and SUBLANES % decode_seq == 0
        tok_spec = pl.BlockSpec((N_GROUPS, rows_per_step, LANE_GROUP), lambda s: (0, tile_in(s), 0))
        st_spec = pl.BlockSpec((seqs_per_step, N_HEADS, HEAD, HEAD), lambda s: (tile_in(s), 0, 0, 0))
        in_specs += [tok_spec] * (N_DECODE_INPUTS - 1) + [st_spec]
        operands += list(decode)
        out_specs += [tok_spec, st_spec]
        out_shape += [jax.ShapeDtypeStruct(decode[0].shape, F32), jax.ShapeDtypeStruct(state.shape, F32)]
    return pl.pallas_call(
        functools.partial(_gmlp_kernel, emit_v=emit_v, decode_seq=decode_seq if decode is not None else 0,
                          mix_block=mix_block),
        grid=(n_tiles + 1,),
        in_specs=in_specs,
        out_specs=out_specs,
        out_shape=out_shape,
        scratch_shapes=[
            pltpu.VMEM((N_GROUPS, tile, LANE_GROUP), F32),
            pltpu.VMEM((N_GROUPS, tile, LANE_GROUP), BF16),
            pltpu.VMEM((tile, D_MODEL), F32),
            w_scr, wo_scr, sems,
        ],
        compiler_params=pltpu.CompilerParams(
            dimension_semantics=("arbitrary",), vmem_limit_bytes=VMEM_LIMIT_BYTES),
        name="gmlp_emit_v" if emit_v else "gmlp",
    )(*operands)


def _lane_groups_of_rows(w):
    return w.reshape(w.shape[0] // LANE_GROUP, LANE_GROUP, w.shape[1])


def _lane_groups_of_vector(v):
    return v.reshape(v.shape[0], N_GROUPS, LANE_GROUP).transpose(1, 0, 2)


def kernel(x_prompt, x_sample, state_hgrn, w_in_a, lb_logits_a, gnorm_a, w_out_a, w_in_b, lnv_g_b,
           lnv_b_b, w_s_b, b_s_b, w_out_b, ln_g, ln_b):
    bsz, seq, _ = x_prompt.shape
    n_seq, dec_seq, _ = x_sample.shape

    wa, woa = w_in_a[0], w_out_a[0]
    lbl3 = _lane_groups_of_vector(lb_logits_a)
    gn3 = _lane_groups_of_vector(gnorm_a[0:1])
    lng0, lnb0 = ln_g[0:1], ln_b[0:1]

    hp, sp = _hgrn_prompt(x_prompt, wa, lbl3, gn3, woa, lng0, lnb0, layer=0)

    xs = x_sample.reshape(n_seq * dec_seq, D_MODEL)
    decode_gates = _hgrn_sample_gates(xs, wa, lbl3, gn3, layer=0, dec_seq=dec_seq)

    w3b, wo3b = w_in_b[0], w_out_b[0]
    vg3 = _lane_groups_of_vector(lnv_g_b[0:1])
    vb3 = _lane_groups_of_vector(lnv_b_b[0:1])
    lng1, lnb1 = ln_g[1:2], ln_b[1:2]
    ws = w_s_b[0]
    bs = b_s_b[0]

    yp, og, ss = _gmlp(hp.reshape(bsz * seq, D_MODEL), w3b, vg3, vb3, ws, bs, wo3b, lng1, lnb1,
                       emit_v=False, decode=(*decode_gates, state_hgrn[0]), decode_seq=dec_seq)
    yp = yp.reshape(bsz, seq, D_MODEL)
    hs = _out_proj_ln(og, xs, _lane_groups_of_rows(woa), lng0, lnb0)

    reps = CHUNK_B // dec_seq
    ws_dec = jnp.tile(ws[:, :dec_seq, :dec_seq], (1, SUBLANES // dec_seq, reps))
    bs_dec = jnp.tile(bs[:, :dec_seq], (1, reps))
    ys, vn = _gmlp(hs, w3b, vg3, vb3, ws_dec, bs_dec, wo3b, lng1, lnb1, emit_v=True, mix_block=dec_seq)
    ys = ys.reshape(n_seq, dec_seq, D_MODEL)
    vs = vn.reshape(n_seq, dec_seq, E)

    return (yp, ys, sp[None], ss[None], vs[None])
```

```python
import functools
import math

import jax
import jax.numpy as jnp
from jax import lax
from jax.experimental import pallas as pl
from jax.experimental.pallas import tpu as pltpu

F32 = jnp.float32
BF16 = jnp.bfloat16

D_MODEL = 1024
E = 2048
HEAD = 128
N_HEADS = E // HEAD
LANE_GROUP = 256
N_GROUPS = E // LANE_GROUP
HEADS_PER_GROUP = LANE_GROUP // HEAD
CHUNK_A = 64
CHUNK_B = 128
ROW_TILE = 256
HGRN_TILE = 512
HGRN_BLOCK = 2 * CHUNK_A
GMLP_TILE = 512
SUBLANES = 8
DEPTH = 2
ALPHA = (2 * DEPTH) ** 0.25
LN_EPS = 1e-5
VMEM_LIMIT_BYTES = 60000 * 1024


def _dot(a, b):
    return jnp.dot(a, b, preferred_element_type=F32)


STAGE_ROWS = 512
STAGE_SLOTS = 8


def _stage_weight(w_hbm, stage, sems, store):
    n_row_blocks = w_hbm.shape[0] // STAGE_ROWS
    n_col_blocks = w_hbm.shape[1] // LANE_GROUP
    chunks = [(r, c) for c in range(n_col_blocks) for r in range(n_row_blocks)]

    def copy(i):
        r, c = chunks[i]
        slot = i % STAGE_SLOTS
        return pltpu.make_async_copy(
            w_hbm.at[pl.ds(r * STAGE_ROWS, STAGE_ROWS), pl.ds(c * LANE_GROUP, LANE_GROUP)],
            stage.at[slot], sems.at[slot])

    for i in range(min(STAGE_SLOTS, len(chunks))):
        copy(i).start()
    for i, (r, c) in enumerate(chunks):
        copy(i).wait()
        store(r, c, stage[i % STAGE_SLOTS].astype(BF16))
        if i + STAGE_SLOTS < len(chunks):
            copy(i + STAGE_SLOTS).start()


def _stage_projection_weights(w_in_hbm, w_out_hbm, w_scr, wo_scr, stage, sems):
    def store_in(r, c, chunk):
        w_scr[c, r * STAGE_ROWS:(r + 1) * STAGE_ROWS, :] = chunk

    def store_out(r, c, chunk):
        for k in range(STAGE_ROWS // LANE_GROUP):
            wo_scr[r * (STAGE_ROWS // LANE_GROUP) + k, :, c * LANE_GROUP:(c + 1) * LANE_GROUP] = (
                chunk[k * LANE_GROUP:(k + 1) * LANE_GROUP])

    _stage_weight(w_in_hbm, stage, sems, store_in)
    _stage_weight(w_out_hbm, stage, sems, store_out)


def _interleave(stages):
    stages = list(stages)
    while stages:
        for stage in list(stages):
            if next(stage, stages) is stages:
                stages.remove(stage)


def _dot_nt(a, b):
    return lax.dot_general(a, b, (((1,), (1,)), ((), ())), preferred_element_type=F32)


def _dot_tn(a, b):
    return lax.dot_general(a, b, (((0,), (0,)), ((), ())), preferred_element_type=F32)


def _silu(x):
    return x / (1.0 + jnp.exp(-x))


def _gelu_tanh(x):
    cdf = 0.5 * (1.0 + jnp.tanh(math.sqrt(2.0 / math.pi) * (x + 0.044715 * (x * x * x))))
    return x * cdf


def _layer_norm(x, g, b):
    mu = jnp.mean(x, axis=-1, keepdims=True)
    d = x - mu
    var = jnp.mean(d * d, axis=-1, keepdims=True)
    return d * lax.rsqrt(var + LN_EPS) * g + b


def _block_masks(n, block):
    shift = block.bit_length() - 1
    row = lax.broadcasted_iota(jnp.int32, (n, n), 0)
    col = lax.broadcasted_iota(jnp.int32, (n, n), 1)
    same = jnp.right_shift(row, shift) == jnp.right_shift(col, shift)
    causal = jnp.logical_and(same, col <= row)
    return same, causal


def _split_dot(m, x):
    hi = x.astype(BF16)
    lo = (x - hi.astype(F32)).astype(BF16)
    return _dot(jnp.concatenate([m, m], axis=1), jnp.concatenate([hi, lo], axis=0))


def _forget_lower_bound(logits, layer):
    m = jnp.max(logits, axis=0, keepdims=True)
    e = jnp.exp(logits - m)
    den = jnp.sum(e, axis=0, keepdims=True)
    num = jnp.sum(e[: layer + 1], axis=0, keepdims=True)
    return num / den


def _hgrn_gates(xb, w_q, w_f, lb, tri, valid):
    qp = _dot(xb, w_q)
    q = _silu(qp)
    z = _dot(xb, w_f)
    c1 = 1.0 - lb
    cs = c1 / (1.0 + jnp.exp(-z))
    f = lb + cs
    k = c1 - cs
    lf = jnp.log(f)
    if valid is not None:
        lf = jnp.where(valid, lf, 0.0)
        k = jnp.where(valid, k, 0.0)
    cum = _split_dot(tri, lf)
    return q * jnp.exp(cum), k * jnp.exp(-cum), cum, lf


def _rms_gate(o, gate):
    ms = jnp.mean(o * o, axis=-1, keepdims=True)
    return o * lax.rsqrt(ms + LN_EPS) * gate


def _hgrn_prompt_kernel(x_ref, w_hbm, lbl_ref, gn_ref, wo_hbm, lng_ref, lnb_ref,
                        y_ref, s_ref, st_scr, og_scr, h_scr, w_ref, wo_ref, stage, sems,
                        *, layer, tiles_per_row):
    s = pl.program_id(0)
    n_tiles = pl.num_programs(0) - 1

    @pl.when(s == 0)
    def _():
        _stage_projection_weights(w_hbm, wo_hbm, w_ref, wo_ref, stage, sems)
        h_scr[...] = jnp.zeros_like(h_scr)

    @pl.when(s < n_tiles)
    def _():
        _hgrn_tile_step(lax.rem(s, tiles_per_row), tiles_per_row, x_ref, w_ref, lbl_ref, gn_ref, wo_ref,
                        lng_ref, lnb_ref, y_ref, s_ref, st_scr, og_scr, h_scr, layer=layer)

    @pl.when(s == n_tiles)
    def _():
        y_ref[0] = _layer_norm(h_scr[...], lng_ref[...], lnb_ref[...])


def _hgrn_tile_step(j, tiles_per_row, x_ref, w_ref, lbl_ref, gn_ref, wo_ref, lng_ref, lnb_ref,
                    y_ref, s_ref, st_scr, og_scr, h_scr, *, layer):
    tile = x_ref.shape[1]
    n_blocks = tile // HGRN_BLOCK
    n_chunks = tile // CHUNK_A
    chunks_per_block = HGRN_BLOCK // CHUNK_A

    @pl.when(j == 0)
    def _():
        st_scr[...] = jnp.zeros_like(st_scr)

    y_ref[0] = _layer_norm(h_scr[...], lng_ref[...], lnb_ref[...])

    x = x_ref[0]
    xb = x.astype(BF16)
    _, causal = _block_masks(HGRN_BLOCK, CHUNK_A)
    tri = jnp.where(causal, 1.0, 0.0).astype(BF16)

    def block(p):
        return slice(p * HGRN_BLOCK, (p + 1) * HGRN_BLOCK)

    lane = lax.broadcasted_iota(jnp.int32, (HGRN_BLOCK, LANE_GROUP), 1)
    lane_head = lane // HEAD
    row = lax.broadcasted_iota(jnp.int32, (HGRN_BLOCK, LANE_GROUP), 0)
    col = jnp.bitwise_and(lane, HEAD - 1)
    causal_heads = jnp.logical_and(row // CHUNK_A == col // CHUNK_A, col <= row)

    def head_block_diag(tiles):
        return jnp.concatenate(
            [jnp.concatenate([t if j == i else jnp.zeros_like(t) for j in range(len(tiles))], axis=1)
             for i, t in enumerate(tiles)], axis=0)

    projected, scored = {}, {}

    def stage_project(hp):
        z = _dot(xb, w_ref[N_GROUPS + hp])
        yield
        qp = _dot(xb, w_ref[hp])
        yield
        v = _dot(xb, w_ref[2 * N_GROUPS + hp])
        yield
        g = _dot(xb, w_ref[3 * N_GROUPS + hp])
        yield
        lb = _forget_lower_bound(lbl_ref[hp], layer)
        c1 = 1.0 - lb
        cs = c1 / (1.0 + jnp.exp(-z))
        k = c1 - cs
        lf = jnp.log(lb + cs)
        cum = jnp.concatenate([_split_dot(tri, lf[block(p)]) for p in range(n_blocks)], axis=0)
        projected[hp] = (_silu(qp), k, cum, v.astype(BF16), _silu(g) * gn_ref[hp])

    def stage_scores(hp):
        q, k, cum, vb, gate = projected.pop(hp)
        qdb = (q * jnp.exp(cum)).astype(BF16)
        ki = k * jnp.exp(-cum)
        dl_rows = jnp.exp(jnp.concatenate(
            [cum[(c + 1) * CHUNK_A - 1:(c + 1) * CHUNK_A] for c in range(n_chunks)]
            + [jnp.zeros((HEAD - n_chunks, LANE_GROUP), F32)], axis=0))
        dl_cols = dl_rows.T
        kits = [ki[:, i * HEAD:(i + 1) * HEAD].T.astype(BF16) for i in range(HEADS_PER_GROUP)]
        scores = []
        for p in range(n_blocks):
            scores.append(_dot(qdb[block(p)], head_block_diag([kit[:, block(p)] for kit in kits])))
            if p % 2 == 1:
                yield
        dls, us = [], []
        for i in range(HEADS_PER_GROUP):
            ls = slice(i * HEAD, (i + 1) * HEAD)
            dls.append([dl_cols[ls, c:c + 1] for c in range(n_chunks)])
            us.append([])
            for c in range(n_chunks):
                lo, hi = c * CHUNK_A, (c + 1) * CHUNK_A
                ke = (ki[lo:hi, ls] * dl_rows[c:c + 1, ls]).astype(BF16)
                us[i].append(_dot_tn(ke, vb[lo:hi, ls]))
                if c % chunks_per_block == chunks_per_block - 1:
                    yield
        scored[hp] = (qdb, vb, scores, dls, us, gate)

    def stage_output(hp):
        qdb, vb, scores, dls, us, gate = scored.pop(hp)
        sts = [st_scr[HEADS_PER_GROUP * hp + i] for i in range(HEADS_PER_GROUP)]
        os = []
        for p in range(n_blocks):
            v_rows = jnp.concatenate(
                [jnp.where(lane_head == i, vb[block(p)], 0.0) for i in range(HEADS_PER_GROUP)], axis=0)
            o_intra = _dot(jnp.where(causal_heads, scores[p], 0.0).astype(BF16), v_rows)
            for cc in range(chunks_per_block):
                c = p * chunks_per_block + cc
                lo, hi = c * CHUNK_A, (c + 1) * CHUNK_A
                st_diag = head_block_diag([st.astype(BF16) for st in sts])
                os.append(o_intra[cc * CHUNK_A:(cc + 1) * CHUNK_A] + _dot(qdb[lo:hi], st_diag))
                sts = [st * dls[i][c] + us[i][c] for i, st in enumerate(sts)]
            yield
        for i, st in enumerate(sts):
            st_scr[HEADS_PER_GROUP * hp + i] = st
        o = jnp.concatenate(os, axis=0)
        og_scr[hp] = jnp.concatenate(
            [_rms_gate(o[:, i * HEAD:(i + 1) * HEAD], gate[:, i * HEAD:(i + 1) * HEAD])
             for i in range(HEADS_PER_GROUP)], axis=1).astype(BF16)

    y_parts = []

    def out_proj(groups):
        y = _dot(og_scr[groups[0]], wo_ref[groups[0]])
        for hp in groups[1:]:
            yield
            y = y + _dot(og_scr[hp], wo_ref[hp])
        y_parts.append(y)

    for t in range(N_GROUPS + 2):
        stages = []
        if t < N_GROUPS:
            stages.append(stage_project(t))
        if t == N_GROUPS:
            stages.append(out_proj(list(range(0, N_GROUPS - 4))))
        if t == N_GROUPS + 1:
            stages.append(out_proj([N_GROUPS - 4, N_GROUPS - 3]))
        if 0 <= t - 1 < N_GROUPS:
            stages.append(stage_scores(t - 1))
        if 0 <= t - 2 < N_GROUPS:
            stages.append(stage_output(t - 2))
        _interleave(stages)
    _interleave([out_proj([N_GROUPS - 2, N_GROUPS - 1])])
    y = y_parts[0] + y_parts[1] + y_parts[2]
    h_scr[...] = ALPHA * x + y

    @pl.when(j == tiles_per_row - 1)
    def _():
        s_ref[0] = st_scr[...]


def _resident(shape):
    nd = len(shape)
    return pl.BlockSpec(shape, lambda *_: (0,) * nd, pipeline_mode=pl.Buffered(1))


def _weight_scratch(w_in, w_out):
    k, n = w_in.shape
    return [
        pltpu.VMEM((n // LANE_GROUP, k, LANE_GROUP), BF16),
        pltpu.VMEM((w_out.shape[0] // LANE_GROUP, LANE_GROUP, w_out.shape[1]), BF16),
        pltpu.VMEM((STAGE_SLOTS, STAGE_ROWS, LANE_GROUP), F32),
        pltpu.SemaphoreType.DMA((STAGE_SLOTS,)),
    ]


def _hgrn_prompt(x, w_in, lbl3, gn3, w_out, lng, lnb, layer):
    bsz, seq, _ = x.shape
    tiles_per_row = seq // HGRN_TILE
    n_tiles = bsz * tiles_per_row
    hbm = pl.BlockSpec(memory_space=pl.ANY)

    def tile_in(s):
        t = jnp.minimum(s, n_tiles - 1)
        return t // tiles_per_row, t % tiles_per_row

    def tile_out(s):
        t = jnp.maximum(s - 1, 0)
        return t // tiles_per_row, t % tiles_per_row

    return pl.pallas_call(
        functools.partial(_hgrn_prompt_kernel, layer=layer, tiles_per_row=tiles_per_row),
        grid=(n_tiles + 1,),
        in_specs=[
            pl.BlockSpec((1, HGRN_TILE, D_MODEL), lambda s: (*tile_in(s), 0)),
            hbm, _resident(lbl3.shape), _resident(gn3.shape), hbm,
            _resident(lng.shape), _resident(lnb.shape),
        ],
        out_specs=[
            pl.BlockSpec((1, HGRN_TILE, D_MODEL), lambda s: (*tile_out(s), 0)),
            pl.BlockSpec((1, N_HEADS, HEAD, HEAD), lambda s: (tile_in(s)[0], 0, 0, 0)),
        ],
        out_shape=[
            jax.ShapeDtypeStruct(x.shape, F32),
            jax.ShapeDtypeStruct((bsz, N_HEADS, HEAD, HEAD), F32),
        ],
        scratch_shapes=[
            pltpu.VMEM((N_HEADS, HEAD, HEAD), F32),
            pltpu.VMEM((N_GROUPS, HGRN_TILE, LANE_GROUP), BF16),
            pltpu.VMEM((HGRN_TILE, D_MODEL), F32),
            *_weight_scratch(w_in, w_out),
        ],
        compiler_params=pltpu.CompilerParams(
            dimension_semantics=("arbitrary",), vmem_limit_bytes=VMEM_LIMIT_BYTES),
        name="hgrn_prompt",
    )(x, w_in, lbl3, gn3, w_out, lng, lnb)


def _hgrn_sample_gates_kernel(x_ref, wq_ref, wf_ref, wi_ref, wg_ref, lbl_ref, gn_ref,
                              qd_ref, ki_ref, ke_ref, v_ref, gate_ref, dl_ref, *, layer, dec_seq):
    rows = x_ref.shape[0]
    xb = x_ref[...].astype(BF16)
    same, causal = _block_masks(ROW_TILE, dec_seq)
    tri = jnp.where(causal, 1.0, 0.0).astype(BF16)
    blk = jnp.where(same, 1.0, 0.0).astype(BF16)
    lb = _forget_lower_bound(lbl_ref[0], layer)

    qp = _dot(xb, wq_ref[...].astype(BF16))
    q = _silu(qp)
    z = _dot(xb, wf_ref[...].astype(BF16))
    c1 = 1.0 - lb
    cs = c1 / (1.0 + jnp.exp(-z))
    k = c1 - cs
    lf = jnp.log(lb + cs)
    for r in range(rows // ROW_TILE):
        rs = slice(r * ROW_TILE, (r + 1) * ROW_TILE)
        cum = _split_dot(tri, lf[rs])
        cum_last = _split_dot(blk, lf[rs])
        ki = k[rs] * jnp.exp(-cum)
        dl = jnp.exp(cum_last)
        qd_ref[0, rs] = q[rs] * jnp.exp(cum)
        ki_ref[0, rs] = ki
        ke_ref[0, rs] = ki * dl
        dl_ref[0, rs] = dl
    v_ref[0] = _dot(xb, wi_ref[...].astype(BF16))
    gate_ref[0] = _silu(_dot(xb, wg_ref[...].astype(BF16))) * gn_ref[0]


def _hgrn_sample_gates(x_rows, w3, lbl3, gn3, layer, dec_seq):
    rows = x_rows.shape[0]
    assert rows % ROW_TILE == 0 and SUBLANES % dec_seq == 0
    w_spec = lambda part: pl.BlockSpec((D_MODEL, LANE_GROUP), lambda g: (0, part * N_GROUPS + g))
    out_spec = pl.BlockSpec((1, rows, LANE_GROUP), lambda g: (g, 0, 0))
    out_shape = jax.ShapeDtypeStruct((N_GROUPS, rows, LANE_GROUP), F32)
    return pl.pallas_call(
        functools.partial(_hgrn_sample_gates_kernel, layer=layer, dec_seq=dec_seq),
        grid=(N_GROUPS,),
        in_specs=[
            pl.BlockSpec((rows, D_MODEL), lambda g: (0, 0)),
            w_spec(0), w_spec(1), w_spec(2), w_spec(3),
            pl.BlockSpec((1,) + lbl3.shape[1:], lambda g: (g, 0, 0)),
            pl.BlockSpec((1, 1, LANE_GROUP), lambda g: (g, 0, 0)),
        ],
        out_specs=[out_spec] * 6,
        out_shape=[out_shape] * 6,
        compiler_params=pltpu.CompilerParams(
            dimension_semantics=("arbitrary",), vmem_limit_bytes=VMEM_LIMIT_BYTES),
        name="hgrn_sample_gates",
    )(x_rows, w3, w3, w3, w3, lbl3, gn3)


def _decode_state_stages(qd_ref, ki_ref, ke_ref, v_ref, gate_ref, dl_ref, s_ref, og_ref, so_ref, dec_seq):
    seqs_per_tile = SUBLANES // dec_seq
    n_tiles = qd_ref.shape[1] // SUBLANES
    shift = dec_seq.bit_length() - 1
    trow = jnp.right_shift(lax.broadcasted_iota(jnp.int32, (SUBLANES, HEAD), 0), shift)
    in_seq = [trow == s for s in range(seqs_per_tile)]
    pad_rows = jnp.zeros((HEAD - N_HEADS, HEAD), F32)
    tiles = [slice(m * SUBLANES, (m + 1) * SUBLANES) for m in range(n_tiles)]
    dcols, pending = {}, {}
    lane_head = lax.broadcasted_iota(jnp.int32, (SUBLANES, LANE_GROUP), 1) // HEAD
    srow = lax.broadcasted_iota(jnp.int32, (SUBLANES, HEADS_PER_GROUP * SUBLANES), 0)
    scol = jnp.bitwise_and(lax.broadcasted_iota(jnp.int32, (SUBLANES, HEADS_PER_GROUP * SUBLANES), 1),
                           SUBLANES - 1)
    causal = jnp.logical_and(jnp.right_shift(srow, shift) == jnp.right_shift(scol, shift), scol <= srow)

    def per_head_rows(t):
        return jnp.concatenate([jnp.where(lane_head == i, t, 0.0) for i in range(HEADS_PER_GROUP)], axis=0)

    def per_seq_lanes(t):
        return jnp.concatenate([jnp.where(in_seq[s], t, 0.0) for s in range(seqs_per_tile)], axis=1)

    def decay_columns(m):
        dl = [dl_ref[hp, tiles[m], :] for hp in range(N_GROUPS)]
        for s in range(seqs_per_tile):
            r = s * dec_seq
            rows = [dl[h // HEADS_PER_GROUP][r:r + 1, (h % HEADS_PER_GROUP) * HEAD:(h % HEADS_PER_GROUP + 1) * HEAD]
                    for h in range(N_HEADS)]
            dcols[(m, s)] = jnp.concatenate(rows + [pad_rows], axis=0).T

    def start(hp):
        for m in range(n_tiles):
            if hp == 0:
                decay_columns(m)
            rs = tiles[m]
            qd = qd_ref[hp, rs, :]
            ke = ke_ref[hp, rs, :]
            v = v_ref[hp, rs, :]
            qdb = qd.astype(BF16)
            scores = _dot_nt(qdb, per_head_rows(ki_ref[hp, rs, :]).astype(BF16))
            o_inter = []
            for i in range(HEADS_PER_GROUP):
                ls = slice(i * HEAD, (i + 1) * HEAD)
                h = HEADS_PER_GROUP * hp + i
                sts = [s_ref[m * seqs_per_tile + s, h] for s in range(seqs_per_tile)]
                st_rows = jnp.concatenate([st.astype(BF16) for st in sts], axis=0)
                o_inter.append(_dot(per_seq_lanes(qd[:, ls]).astype(BF16), st_rows))
                u = _dot_tn(per_seq_lanes(ke[:, ls]).astype(BF16), v[:, ls].astype(BF16))
                for s in range(seqs_per_tile):
                    so_ref[m * seqs_per_tile + s, h] = (dcols[(m, s)][:, h:h + 1] * sts[s]
                                                       + u[s * HEAD:(s + 1) * HEAD])
            pending[(hp, m)] = (scores, jnp.concatenate(o_inter, axis=1), per_head_rows(v).astype(BF16))
            yield

    def finish(hp):
        for m in range(n_tiles):
            scores, o_inter, v_rows = pending.pop((hp, m))
            o = _dot(jnp.where(causal, scores, 0.0).astype(BF16), v_rows) + o_inter
            gate = gate_ref[hp, tiles[m], :]
            og_ref[hp, tiles[m], :] = jnp.concatenate(
                [_rms_gate(o[:, i * HEAD:(i + 1) * HEAD], gate[:, i * HEAD:(i + 1) * HEAD])
                 for i in range(HEADS_PER_GROUP)], axis=1)

    return start, finish


def _out_proj_ln_kernel(og_ref, x_ref, wo_ref, lng_ref, lnb_ref, y_ref):
    y = _dot(og_ref[0].astype(BF16), wo_ref[0].astype(BF16))
    for hp in range(1, N_GROUPS):
        y = y + _dot(og_ref[hp].astype(BF16), wo_ref[hp].astype(BF16))
    y_ref[...] = _layer_norm(ALPHA * x_ref[...] + y, lng_ref[...], lnb_ref[...])


def _out_proj_ln(og, x, wo3, lng, lnb):
    rows = x.shape[0]
    return pl.pallas_call(
        _out_proj_ln_kernel,
        grid=(rows // ROW_TILE,),
        in_specs=[
            pl.BlockSpec((N_GROUPS, ROW_TILE, LANE_GROUP), lambda r: (0, r, 0)),
            pl.BlockSpec((ROW_TILE, D_MODEL), lambda r: (r, 0)),
            _resident(wo3.shape), _resident(lng.shape), _resident(lnb.shape),
        ],
        out_specs=pl.BlockSpec((ROW_TILE, D_MODEL), lambda r: (r, 0)),
        out_shape=jax.ShapeDtypeStruct(x.shape, F32),
        compiler_params=pltpu.CompilerParams(
            dimension_semantics=("arbitrary",), vmem_limit_bytes=VMEM_LIMIT_BYTES),
        name="out_proj_ln",
    )(og, x, wo3, lng, lnb)


N_GMLP_INPUTS = 9
N_DECODE_INPUTS = 7


def _gmlp_kernel(*refs, emit_v, decode_seq, mix_block):
    refs = list(refs)
    gmlp_in = refs[:N_GMLP_INPUTS]
    del refs[:N_GMLP_INPUTS]
    decode_in = []
    if decode_seq:
        decode_in = refs[:N_DECODE_INPUTS]
        del refs[:N_DECODE_INPUTS]
    y_ref = refs.pop(0)
    vn_ref = refs.pop(0) if emit_v else None
    decode_out = [refs.pop(0), refs.pop(0)] if decode_seq else []
    v_scr, og_scr, h_scr, w_ref, wo_ref, sems = refs
    x_ref, w_hbm, vg_ref, vb_ref, ws_ref, bs_ref, wo_hbm, lng_ref, lnb_ref = gmlp_in
    s = pl.program_id(0)
    n_tiles = pl.num_programs(0) - 1

    @pl.when(s == 0)
    def _():
        _stage_projection_weights(w_hbm, wo_hbm, w_ref, wo_ref, v_scr, sems)
        h_scr[...] = jnp.zeros_like(h_scr)

    @pl.when(s < n_tiles)
    def _():
        decode = _decode_state_stages(*decode_in, *decode_out, decode_seq) if decode_seq else None
        _gmlp_tile_step(x_ref, w_ref, vg_ref, vb_ref, ws_ref, bs_ref, wo_ref, lng_ref, lnb_ref,
                        y_ref, vn_ref, v_scr, og_scr, h_scr, decode, mix_block)

    @pl.when(s == n_tiles)
    def _():
        y_ref[...] = _layer_norm(h_scr[...], lng_ref[...], lnb_ref[...])


def _gmlp_tile_step(x_ref, w_ref, vg_ref, vb_ref, ws_ref, bs_ref, wo_ref, lng_ref, lnb_ref,
                    y_ref, vn_ref, v_scr, og_scr, h_scr, decode, mix_block):
    y_ref[...] = _layer_norm(h_scr[...], lng_ref[...], lnb_ref[...])

    n_chunks = x_ref.shape[0] // CHUNK_B
    x = x_ref[...]
    xb = x.astype(BF16)

    half = x_ref.shape[0] // 2

    def stage_v(gp):
        for r in range(2):
            rows = slice(r * half, (r + 1) * half)
            v_scr[gp, rows] = _gelu_tanh(_dot(xb[rows], w_ref[N_GROUPS + gp]))
            yield

    for gp in range(N_GROUPS):
        _interleave([stage_v(gp)] + ([decode[0](gp)] if decode is not None else []))

    s1 = jnp.sum(v_scr[0], axis=-1, keepdims=True)
    for gp in range(1, N_GROUPS):
        s1 = s1 + jnp.sum(v_scr[gp], axis=-1, keepdims=True)
    mu = s1 * (1.0 / E)
    s2 = jnp.zeros_like(mu)
    for gp in range(N_GROUPS):
        d = v_scr[gp] - mu
        s2 = s2 + jnp.sum(d * d, axis=-1, keepdims=True)
    rstd = lax.rsqrt(s2 * (1.0 / E) + LN_EPS)

    _, causal = _block_masks(CHUNK_B, mix_block)
    ws_reps = CHUNK_B // ws_ref.shape[1]
    n_mix_groups = bs_ref.shape[0]
    bias_cols = jnp.concatenate(
        [bs_ref[...], jnp.zeros((CHUNK_B - n_mix_groups, CHUNK_B), F32)], axis=0).T

    def stage_gate(gp):
        return _dot(xb, w_ref[gp]), _dot(xb, w_ref[2 * N_GROUPS + gp])

    def stage_mix(gp, gate):
        u_pre, z = gate
        vn = (v_scr[gp] - mu) * rstd * vg_ref[gp] + vb_ref[gp]
        if vn_ref is not None:
            vn_ref[:, gp * LANE_GROUP:(gp + 1) * LANE_GROUP] = vn
        vnb = vn.astype(BF16)
        cols = []
        for i in range(HEADS_PER_GROUP):
            ls = slice(i * HEAD, (i + 1) * HEAD)
            g = HEADS_PER_GROUP * gp + i
            wg = jnp.concatenate([ws_ref[g]] * ws_reps, axis=0) if ws_reps > 1 else ws_ref[g]
            wc = jnp.where(causal, wg, 0.0).astype(BF16)
            bias = bias_cols[:, g:g + 1]
            side_by_side = jnp.concatenate(
                [vnb[c * CHUNK_B:(c + 1) * CHUNK_B, ls] for c in range(n_chunks)], axis=1)
            m = _dot(wc, side_by_side) + bias
            cols.append(jnp.concatenate([m[:, c * HEAD:(c + 1) * HEAD] for c in range(n_chunks)], axis=0))
        mixed = jnp.concatenate(cols, axis=1)
        og_scr[gp] = (_gelu_tanh(u_pre) * mixed * _silu(z)).astype(BF16)

    def out_proj(groups):
        y = _dot(og_scr[groups[0]], wo_ref[groups[0]])
        for gp in groups[1:]:
            y = y + _dot(og_scr[gp], wo_ref[gp])
        return y

    y_parts = []
    gates = {0: stage_gate(0)}
    for gp in range(N_GROUPS):
        if gp + 1 < N_GROUPS:
            gates[gp + 1] = stage_gate(gp + 1)
        stage_mix(gp, gates.pop(gp))
        if decode is not None:
            decode[1](gp)
        if gp == N_GROUPS - 2:
            y_parts.append(out_proj(list(range(0, N_GROUPS - 3))))
    y_parts.append(out_proj([N_GROUPS - 3, N_GROUPS - 2]))
    y_parts.append(out_proj([N_GROUPS - 1]))
    y = y_parts[0] + y_parts[1] + y_parts[2]
    h_scr[...] = ALPHA * x + y


def _gmlp(x, w_in, vg3, vb3, ws, bs_rows, w_out, lng, lnb, emit_v, mix_block=CHUNK_B, decode=None,
          decode_seq=0):
    rows = x.shape[0]
    tile = min(GMLP_TILE, rows)
    assert tile == STAGE_ROWS and N_GROUPS == STAGE_SLOTS
    n_tiles = rows // tile
    tile_in = lambda s: jnp.minimum(s, n_tiles - 1)
    tile_out = lambda s: jnp.maximum(s - 1, 0)
    hbm = pl.BlockSpec(memory_space=pl.ANY)
    in_specs = [
        pl.BlockSpec((tile, D_MODEL), lambda s: (tile_in(s), 0)),
        hbm, _resident(vg3.shape), _resident(vb3.shape), _resident(ws.shape),
        _resident(bs_rows.shape), hbm, _resident(lng.shape), _resident(lnb.shape),
    ]
    operands = [x, w_in, vg3, vb3, ws, bs_rows, w_out, lng, lnb]
    w_scr, wo_scr, _, sems = _weight_scratch(w_in, w_out)
    assert len(operands) == N_GMLP_INPUTS
    out_specs = [pl.BlockSpec((tile, D_MODEL), lambda s: (tile_out(s), 0))]
    out_shape = [jax.ShapeDtypeStruct(x.shape, F32)]
    if emit_v:
        out_specs.append(pl.BlockSpec((tile, E), lambda s: (tile_in(s), 0)))
        out_shape.append(jax.ShapeDtypeStruct((rows, E), F32))
    if decode is not None:
        assert len(decode) == N_DECODE_INPUTS
        state = decode[-1]
        n_seq = state.shape[0]
        seqs_per_step = n_seq // n_tiles
        rows_per_step = seqs_per_step * decode_seq
        assert n_seq % n_tiles == 0 and rows_per_step % SUBLANES == 0 and SUBLANES % decode_seq == 0
        tok_spec = pl.BlockSpec((N_GROUPS, rows_per_step, LANE_GROUP), lambda s: (0, tile_in(s), 0))
        st_spec = pl.BlockSpec((seqs_per_step, N_HEADS, HEAD, HEAD), lambda s: (tile_in(s), 0, 0, 0))
        in_specs += [tok_spec] * (N_DECODE_INPUTS - 1) + [st_spec]
        operands += list(decode)
        out_specs += [tok_spec, st_spec]
        out_shape += [jax.ShapeDtypeStruct(decode[0].shape, F32), jax.ShapeDtypeStruct(state.shape, F32)]
    return pl.pallas_call(
        functools.partial(_gmlp_kernel, emit_v=emit_v, decode_seq=decode_seq if decode is not None else 0,
                          mix_block=mix_block),
        grid=(n_tiles + 1,),
        in_specs=in_specs,
        out_specs=out_specs,
        out_shape=out_shape,
        scratch_shapes=[
            pltpu.VMEM((N_GROUPS, tile, LANE_GROUP), F32),
            pltpu.VMEM((N_GROUPS, tile, LANE_GROUP), BF16),
            pltpu.VMEM((tile, D_MODEL), F32),
            w_scr, wo_scr, sems,
        ],
        compiler_params=pltpu.CompilerParams(
            dimension_semantics=("arbitrary",), vmem_limit_bytes=VMEM_LIMIT_BYTES),
        name="gmlp_emit_v" if emit_v else "gmlp",
    )(*operands)


def _lane_groups_of_rows(w):
    return w.reshape(w.shape[0] // LANE_GROUP, LANE_GROUP, w.shape[1])


def _lane_groups_of_vector(v):
    return v.reshape(v.shape[0], N_GROUPS, LANE_GROUP).transpose(1, 0, 2)


def kernel(x_prompt, x_sample, state_hgrn, w_in_a, lb_logits_a, gnorm_a, w_out_a, w_in_b, lnv_g_b,
           lnv_b_b, w_s_b, b_s_b, w_out_b, ln_g, ln_b):
    bsz, seq, _ = x_prompt.shape
    n_seq, dec_seq, _ = x_sample.shape

    wa, woa = w_in_a[0], w_out_a[0]
    lbl3 = _lane_groups_of_vector(lb_logits_a)
    gn3 = _lane_groups_of_vector(gnorm_a[0:1])
    lng0, lnb0 = ln_g[0:1], ln_b[0:1]

    hp, sp = _hgrn_prompt(x_prompt, wa, lbl3, gn3, woa, lng0, lnb0, layer=0)

    xs = x_sample.reshape(n_seq * dec_seq, D_MODEL)
    decode_gates = _hgrn_sample_gates(xs, wa, lbl3, gn3, layer=0, dec_seq=dec_seq)

    w3b, wo3b = w_in_b[0], w_out_b[0]
    vg3 = _lane_groups_of_vector(lnv_g_b[0:1])
    vb3 = _lane_groups_of_vector(lnv_b_b[0:1])
    lng1, lnb1 = ln_g[1:2], ln_b[1:2]
    ws = w_s_b[0]
    bs = b_s_b[0]

    yp, og, ss = _gmlp(hp.reshape(bsz * seq, D_MODEL), w3b, vg3, vb3, ws, bs, wo3b, lng1, lnb1,
                       emit_v=False, decode=(*decode_gates, state_hgrn[0]), decode_seq=dec_seq)
    yp = yp.reshape(bsz, seq, D_MODEL)
    hs = _out_proj_ln(og, xs, _lane_groups_of_rows(woa), lng0, lnb0)

    reps = CHUNK_B // dec_seq
    ws_dec = jnp.tile(ws[:, :dec_seq, :dec_seq], (1, SUBLANES // dec_seq, reps))
    bs_dec = jnp.tile(bs[:, :dec_seq], (1, reps))
    ys, vn = _gmlp(hs, w3b, vg3, vb3, ws_dec, bs_dec, wo3b, lng1, lnb1, emit_v=True, mix_block=dec_seq)
    ys = ys.reshape(n_seq, dec_seq, D_MODEL)
    vs = vn.reshape(n_seq, dec_seq, E)

    return (yp, ys, sp[None], ss[None], vs[None])
```

```python
import functools
import math

import jax
import jax.numpy as jnp
from jax import lax
from jax.experimental import pallas as pl
from jax.experimental.pallas import tpu as pltpu

F32 = jnp.float32
BF16 = jnp.bfloat16

D_MODEL = 1024
E = 2048
HEAD = 128
N_HEADS = E // HEAD
LANE_GROUP = 256
N_GROUPS = E // LANE_GROUP
HEADS_PER_GROUP = LANE_GROUP // HEAD
CHUNK_A = 64
CHUNK_B = 128
ROW_TILE = 256
HGRN_TILE = 512
HGRN_BLOCK = 2 * CHUNK_A
GMLP_TILE = 512
SUBLANES = 8
DEPTH = 2
ALPHA = (2 * DEPTH) ** 0.25
LN_EPS = 1e-5
VMEM_LIMIT_BYTES = 60000 * 1024


def _dot(a, b):
    return jnp.dot(a, b, preferred_element_type=F32)


STAGE_ROWS = 512
STAGE_SLOTS = 8


def _stage_weight(w_hbm, stage, sems, store):
    n_row_blocks = w_hbm.shape[0] // STAGE_ROWS
    n_col_blocks = w_hbm.shape[1] // LANE_GROUP
    chunks = [(r, c) for c in range(n_col_blocks) for r in range(n_row_blocks)]

    def copy(i):
        r, c = chunks[i]
        slot = i % STAGE_SLOTS
        return pltpu.make_async_copy(
            w_hbm.at[pl.ds(r * STAGE_ROWS, STAGE_ROWS), pl.ds(c * LANE_GROUP, LANE_GROUP)],
            stage.at[slot], sems.at[slot])

    for i in range(min(STAGE_SLOTS, len(chunks))):
        copy(i).start()
    for i, (r, c) in enumerate(chunks):
        copy(i).wait()
        store(r, c, stage[i % STAGE_SLOTS].astype(BF16))
        if i + STAGE_SLOTS < len(chunks):
            copy(i + STAGE_SLOTS).start()


def _stage_projection_weights(w_in_hbm, w_out_hbm, w_scr, wo_scr, stage, sems):
    def store_in(r, c, chunk):
        w_scr[c, r * STAGE_ROWS:(r + 1) * STAGE_ROWS, :] = chunk

    def store_out(r, c, chunk):
        for k in range(STAGE_ROWS // LANE_GROUP):
            wo_scr[r * (STAGE_ROWS // LANE_GROUP) + k, :, c * LANE_GROUP:(c + 1) * LANE_GROUP] = (
                chunk[k * LANE_GROUP:(k + 1) * LANE_GROUP])

    _stage_weight(w_in_hbm, stage, sems, store_in)
    _stage_weight(w_out_hbm, stage, sems, store_out)


def _interleave(stages):
    stages = list(stages)
    while stages:
        for stage in list(stages):
            if next(stage, stages) is stages:
                stages.remove(stage)


def _dot_nt(a, b):
    return lax.dot_general(a, b, (((1,), (1,)), ((), ())), preferred_element_type=F32)


def _dot_tn(a, b):
    return lax.dot_general(a, b, (((0,), (0,)), ((), ())), preferred_element_type=F32)


def _silu(x):
    return x / (1.0 + jnp.exp(-x))


def _gelu_tanh(x):
    cdf = 0.5 * (1.0 + jnp.tanh(math.sqrt(2.0 / math.pi) * (x + 0.044715 * (x * x * x))))
    return x * cdf


def _layer_norm(x, g, b):
    mu = jnp.mean(x, axis=-1, keepdims=True)
    d = x - mu
    var = jnp.mean(d * d, axis=-1, keepdims=True)
    return d * lax.rsqrt(var + LN_EPS) * g + b


def _block_masks(n, block):
    shift = block.bit_length() - 1
    row = lax.broadcasted_iota(jnp.int32, (n, n), 0)
    col = lax.broadcasted_iota(jnp.int32, (n, n), 1)
    same = jnp.right_shift(row, shift) == jnp.right_shift(col, shift)
    causal = jnp.logical_and(same, col <= row)
    return same, causal


def _split_dot(m, x):
    hi = x.astype(BF16)
    lo = (x - hi.astype(F32)).astype(BF16)
    return _dot(jnp.concatenate([m, m], axis=1), jnp.concatenate([hi, lo], axis=0))


def _forget_lower_bound(logits, layer):
    m = jnp.max(logits, axis=0, keepdims=True)
    e = jnp.exp(logits - m)
    den = jnp.sum(e, axis=0, keepdims=True)
    num = jnp.sum(e[: layer + 1], axis=0, keepdims=True)
    return num / den


def _hgrn_gates(xb, w_q, w_f, lb, tri, valid):
    qp = _dot(xb, w_q)
    q = _silu(qp)
    z = _dot(xb, w_f)
    c1 = 1.0 - lb
    cs = c1 / (1.0 + jnp.exp(-z))
    f = lb + cs
    k = c1 - cs
    lf = jnp.log(f)
    if valid is not None:
        lf = jnp.where(valid, lf, 0.0)
        k = jnp.where(valid, k, 0.0)
    cum = _split_dot(tri, lf)
    return q * jnp.exp(cum), k * jnp.exp(-cum), cum, lf


def _rms_gate(o, gate):
    ms = jnp.mean(o * o, axis=-1, keepdims=True)
    return o * lax.rsqrt(ms + LN_EPS) * gate


def _hgrn_prompt_kernel(x_ref, w_hbm, lbl_ref, gn_ref, wo_hbm, lng_ref, lnb_ref,
                        y_ref, s_ref, st_scr, og_scr, h_scr, w_ref, wo_ref, stage, sems,
                        *, layer, tiles_per_row):
    s = pl.program_id(0)
    n_tiles = pl.num_programs(0) - 1

    @pl.when(s == 0)
    def _():
        _stage_projection_weights(w_hbm, wo_hbm, w_ref, wo_ref, stage, sems)
        h_scr[...] = jnp.zeros_like(h_scr)

    @pl.when(s < n_tiles)
    def _():
        _hgrn_tile_step(lax.rem(s, tiles_per_row), tiles_per_row, x_ref, w_ref, lbl_ref, gn_ref, wo_ref,
                        lng_ref, lnb_ref, y_ref, s_ref, st_scr, og_scr, h_scr, layer=layer)

    @pl.when(s == n_tiles)
    def _():
        y_ref[0] = _layer_norm(h_scr[...], lng_ref[...], lnb_ref[...])


def _hgrn_tile_step(j, tiles_per_row, x_ref, w_ref, lbl_ref, gn_ref, wo_ref, lng_ref, lnb_ref,
                    y_ref, s_ref, st_scr, og_scr, h_scr, *, layer):
    tile = x_ref.shape[1]
    n_blocks = tile // HGRN_BLOCK
    n_chunks = tile // CHUNK_A
    chunks_per_block = HGRN_BLOCK // CHUNK_A

    @pl.when(j == 0)
    def _():
        st_scr[...] = jnp.zeros_like(st_scr)

    y_ref[0] = _layer_norm(h_scr[...], lng_ref[...], lnb_ref[...])

    x = x_ref[0]
    xb = x.astype(BF16)
    _, causal = _block_masks(HGRN_BLOCK, CHUNK_A)
    tri = jnp.where(causal, 1.0, 0.0).astype(BF16)

    def block(p):
        return slice(p * HGRN_BLOCK, (p + 1) * HGRN_BLOCK)

    projected, scored = {}, {}

    def stage_project(hp):
        z = _dot(xb, w_ref[N_GROUPS + hp])
        yield
        qp = _dot(xb, w_ref[hp])
        yield
        v = _dot(xb, w_ref[2 * N_GROUPS + hp])
        yield
        g = _dot(xb, w_ref[3 * N_GROUPS + hp])
        yield
        lb = _forget_lower_bound(lbl_ref[hp], layer)
        c1 = 1.0 - lb
        cs = c1 / (1.0 + jnp.exp(-z))
        k = c1 - cs
        lf = jnp.log(lb + cs)
        cum = jnp.concatenate([_split_dot(tri, lf[block(p)]) for p in range(n_blocks)], axis=0)
        projected[hp] = (_silu(qp), k, cum, v.astype(BF16), _silu(g) * gn_ref[hp])

    def stage_scores(hp):
        q, k, cum, vb, gate = projected.pop(hp)
        qdb = (q * jnp.exp(cum)).astype(BF16)
        ki = k * jnp.exp(-cum)
        dl_rows = jnp.exp(jnp.concatenate(
            [cum[(c + 1) * CHUNK_A - 1:(c + 1) * CHUNK_A] for c in range(n_chunks)]
            + [jnp.zeros((HEAD - n_chunks, LANE_GROUP), F32)], axis=0))
        dl_cols = dl_rows.T
        heads = []
        for i in range(HEADS_PER_GROUP):
            ls = slice(i * HEAD, (i + 1) * HEAD)
            kit = ki[:, ls].T.astype(BF16)
            scores = [_dot(qdb[block(p), ls], kit[:, block(p)]) for p in range(n_blocks)]
            yield
            dls, us = [], []
            for c in range(n_chunks):
                lo, hi = c * CHUNK_A, (c + 1) * CHUNK_A
                ke = (ki[lo:hi, ls] * dl_rows[c:c + 1, ls]).astype(BF16)
                dls.append(dl_cols[ls, c:c + 1])
                us.append(_dot_tn(ke, vb[lo:hi, ls]))
                if c % chunks_per_block == chunks_per_block - 1:
                    yield
            heads.append((qdb[:, ls], vb[:, ls], scores, dls, us))
        scored[hp] = (heads, gate)

    def stage_output(hp):
        heads, gate = scored.pop(hp)
        ogs = []
        for i, (qdb, vb, scores, dls, us) in enumerate(heads):
            st = st_scr[HEADS_PER_GROUP * hp + i]
            os = []
            for p in range(n_blocks):
                o_intra = _dot(jnp.where(causal, scores[p], 0.0).astype(BF16), vb[block(p)])
                for cc in range(chunks_per_block):
                    c = p * chunks_per_block + cc
                    lo, hi = c * CHUNK_A, (c + 1) * CHUNK_A
                    os.append(o_intra[cc * CHUNK_A:(cc + 1) * CHUNK_A] + _dot(qdb[lo:hi], st.astype(BF16)))
                    st = st * dls[c] + us[c]
                yield
            st_scr[HEADS_PER_GROUP * hp + i] = st
            o = jnp.concatenate(os, axis=0)
            ogs.append(_rms_gate(o, gate[:, i * HEAD:(i + 1) * HEAD]))
        og_scr[hp] = jnp.concatenate(ogs, axis=1).astype(BF16)

    y_parts = []

    def out_proj(groups):
        y = _dot(og_scr[groups[0]], wo_ref[groups[0]])
        for hp in groups[1:]:
            yield
            y = y + _dot(og_scr[hp], wo_ref[hp])
        y_parts.append(y)

    for t in range(N_GROUPS + 2):
        stages = []
        if t < N_GROUPS:
            stages.append(stage_project(t))
        if t == N_GROUPS:
            stages.append(out_proj(list(range(0, N_GROUPS - 4))))
        if t == N_GROUPS + 1:
            stages.append(out_proj([N_GROUPS - 4, N_GROUPS - 3]))
        if 0 <= t - 2 < N_GROUPS:
            stages.append(stage_output(t - 2))
        if 0 <= t - 1 < N_GROUPS:
            stages.append(stage_scores(t - 1))
        _interleave(stages)
    _interleave([out_proj([N_GROUPS - 2, N_GROUPS - 1])])
    y = y_parts[0] + y_parts[1] + y_parts[2]
    h_scr[...] = ALPHA * x + y

    @pl.when(j == tiles_per_row - 1)
    def _():
        s_ref[0] = st_scr[...]


def _resident(shape):
    nd = len(shape)
    return pl.BlockSpec(shape, lambda *_: (0,) * nd, pipeline_mode=pl.Buffered(1))


def _weight_scratch(w_in, w_out):
    k, n = w_in.shape
    return [
        pltpu.VMEM((n // LANE_GROUP, k, LANE_GROUP), BF16),
        pltpu.VMEM((w_out.shape[0] // LANE_GROUP, LANE_GROUP, w_out.shape[1]), BF16),
        pltpu.VMEM((STAGE_SLOTS, STAGE_ROWS, LANE_GROUP), F32),
        pltpu.SemaphoreType.DMA((STAGE_SLOTS,)),
    ]


def _hgrn_prompt(x, w_in, lbl3, gn3, w_out, lng, lnb, layer):
    bsz, seq, _ = x.shape
    tiles_per_row = seq // HGRN_TILE
    n_tiles = bsz * tiles_per_row
    hbm = pl.BlockSpec(memory_space=pl.ANY)

    def tile_in(s):
        t = jnp.minimum(s, n_tiles - 1)
        return t // tiles_per_row, t % tiles_per_row

    def tile_out(s):
        t = jnp.maximum(s - 1, 0)
        return t // tiles_per_row, t % tiles_per_row

    return pl.pallas_call(
        functools.partial(_hgrn_prompt_kernel, layer=layer, tiles_per_row=tiles_per_row),
        grid=(n_tiles + 1,),
        in_specs=[
            pl.BlockSpec((1, HGRN_TILE, D_MODEL), lambda s: (*tile_in(s), 0)),
            hbm, _resident(lbl3.shape), _resident(gn3.shape), hbm,
            _resident(lng.shape), _resident(lnb.shape),
        ],
        out_specs=[
            pl.BlockSpec((1, HGRN_TILE, D_MODEL), lambda s: (*tile_out(s), 0)),
            pl.BlockSpec((1, N_HEADS, HEAD, HEAD), lambda s: (tile_in(s)[0], 0, 0, 0)),
        ],
        out_shape=[
            jax.ShapeDtypeStruct(x.shape, F32),
            jax.ShapeDtypeStruct((bsz, N_HEADS, HEAD, HEAD), F32),
        ],
        scratch_shapes=[
            pltpu.VMEM((N_HEADS, HEAD, HEAD), F32),
            pltpu.VMEM((N_GROUPS, HGRN_TILE, LANE_GROUP), BF16),
            pltpu.VMEM((HGRN_TILE, D_MODEL), F32),
            *_weight_scratch(w_in, w_out),
        ],
        compiler_params=pltpu.CompilerParams(
            dimension_semantics=("arbitrary",), vmem_limit_bytes=VMEM_LIMIT_BYTES),
        name="hgrn_prompt",
    )(x, w_in, lbl3, gn3, w_out, lng, lnb)


def _hgrn_sample_gates_kernel(x_ref, wq_ref, wf_ref, wi_ref, wg_ref, lbl_ref, gn_ref,
                              qd_ref, ki_ref, ke_ref, v_ref, gate_ref, dl_ref, *, layer, dec_seq):
    rows = x_ref.shape[0]
    xb = x_ref[...].astype(BF16)
    same, causal = _block_masks(ROW_TILE, dec_seq)
    tri = jnp.where(causal, 1.0, 0.0).astype(BF16)
    blk = jnp.where(same, 1.0, 0.0).astype(BF16)
    lb = _forget_lower_bound(lbl_ref[0], layer)

    qp = _dot(xb, wq_ref[...].astype(BF16))
    q = _silu(qp)
    z = _dot(xb, wf_ref[...].astype(BF16))
    c1 = 1.0 - lb
    cs = c1 / (1.0 + jnp.exp(-z))
    k = c1 - cs
    lf = jnp.log(lb + cs)
    for r in range(rows // ROW_TILE):
        rs = slice(r * ROW_TILE, (r + 1) * ROW_TILE)
        cum = _split_dot(tri, lf[rs])
        cum_last = _split_dot(blk, lf[rs])
        ki = k[rs] * jnp.exp(-cum)
        dl = jnp.exp(cum_last)
        qd_ref[0, rs] = q[rs] * jnp.exp(cum)
        ki_ref[0, rs] = ki
        ke_ref[0, rs] = ki * dl
        dl_ref[0, rs] = dl
    v_ref[0] = _dot(xb, wi_ref[...].astype(BF16))
    gate_ref[0] = _silu(_dot(xb, wg_ref[...].astype(BF16))) * gn_ref[0]


def _hgrn_sample_gates(x_rows, w3, lbl3, gn3, layer, dec_seq):
    rows = x_rows.shape[0]
    assert rows % ROW_TILE == 0 and SUBLANES % dec_seq == 0
    w_spec = lambda part: pl.BlockSpec((D_MODEL, LANE_GROUP), lambda g: (0, part * N_GROUPS + g))
    out_spec = pl.BlockSpec((1, rows, LANE_GROUP), lambda g: (g, 0, 0))
    out_shape = jax.ShapeDtypeStruct((N_GROUPS, rows, LANE_GROUP), F32)
    return pl.pallas_call(
        functools.partial(_hgrn_sample_gates_kernel, layer=layer, dec_seq=dec_seq),
        grid=(N_GROUPS,),
        in_specs=[
            pl.BlockSpec((rows, D_MODEL), lambda g: (0, 0)),
            w_spec(0), w_spec(1), w_spec(2), w_spec(3),
            pl.BlockSpec((1,) + lbl3.shape[1:], lambda g: (g, 0, 0)),
            pl.BlockSpec((1, 1, LANE_GROUP), lambda g: (g, 0, 0)),
        ],
        out_specs=[out_spec] * 6,
        out_shape=[out_shape] * 6,
        compiler_params=pltpu.CompilerParams(
            dimension_semantics=("arbitrary",), vmem_limit_bytes=VMEM_LIMIT_BYTES),
        name="hgrn_sample_gates",
    )(x_rows, w3, w3, w3, w3, lbl3, gn3)


def _decode_state_stages(qd_ref, ki_ref, ke_ref, v_ref, gate_ref, dl_ref, s_ref, og_ref, so_ref, dec_seq):
    seqs_per_tile = SUBLANES // dec_seq
    n_tiles = qd_ref.shape[1] // SUBLANES
    shift = dec_seq.bit_length() - 1
    trow = jnp.right_shift(lax.broadcasted_iota(jnp.int32, (SUBLANES, HEAD), 0), shift)
    in_seq = [trow == s for s in range(seqs_per_tile)]
    pad_rows = jnp.zeros((HEAD - N_HEADS, HEAD), F32)
    tiles = [slice(m * SUBLANES, (m + 1) * SUBLANES) for m in range(n_tiles)]
    dcols, pending = {}, {}
    lane_head = lax.broadcasted_iota(jnp.int32, (SUBLANES, LANE_GROUP), 1) // HEAD
    srow = lax.broadcasted_iota(jnp.int32, (SUBLANES, HEADS_PER_GROUP * SUBLANES), 0)
    scol = jnp.bitwise_and(lax.broadcasted_iota(jnp.int32, (SUBLANES, HEADS_PER_GROUP * SUBLANES), 1),
                           SUBLANES - 1)
    causal = jnp.logical_and(jnp.right_shift(srow, shift) == jnp.right_shift(scol, shift), scol <= srow)

    def per_head_rows(t):
        return jnp.concatenate([jnp.where(lane_head == i, t, 0.0) for i in range(HEADS_PER_GROUP)], axis=0)

    def per_seq_lanes(t):
        return jnp.concatenate([jnp.where(in_seq[s], t, 0.0) for s in range(seqs_per_tile)], axis=1)

    def decay_columns(m):
        dl = [dl_ref[hp, tiles[m], :] for hp in range(N_GROUPS)]
        for s in range(seqs_per_tile):
            r = s * dec_seq
            rows = [dl[h // HEADS_PER_GROUP][r:r + 1, (h % HEADS_PER_GROUP) * HEAD:(h % HEADS_PER_GROUP + 1) * HEAD]
                    for h in range(N_HEADS)]
            dcols[(m, s)] = jnp.concatenate(rows + [pad_rows], axis=0).T

    def start(hp):
        for m in range(n_tiles):
            if hp == 0:
                decay_columns(m)
            rs = tiles[m]
            qd = qd_ref[hp, rs, :]
            ke = ke_ref[hp, rs, :]
            v = v_ref[hp, rs, :]
            qdb = qd.astype(BF16)
            scores = _dot_nt(qdb, per_head_rows(ki_ref[hp, rs, :]).astype(BF16))
            o_inter = []
            for i in range(HEADS_PER_GROUP):
                ls = slice(i * HEAD, (i + 1) * HEAD)
                h = HEADS_PER_GROUP * hp + i
                sts = [s_ref[m * seqs_per_tile + s, h] for s in range(seqs_per_tile)]
                st_rows = jnp.concatenate([st.astype(BF16) for st in sts], axis=0)
                o_inter.append(_dot(per_seq_lanes(qd[:, ls]).astype(BF16), st_rows))
                u = _dot_tn(per_seq_lanes(ke[:, ls]).astype(BF16), v[:, ls].astype(BF16))
                for s in range(seqs_per_tile):
                    so_ref[m * seqs_per_tile + s, h] = (dcols[(m, s)][:, h:h + 1] * sts[s]
                                                       + u[s * HEAD:(s + 1) * HEAD])
            pending[(hp, m)] = (scores, jnp.concatenate(o_inter, axis=1), per_head_rows(v).astype(BF16))
            yield

    def finish(hp):
        for m in range(n_tiles):
            scores, o_inter, v_rows = pending.pop((hp, m))
            o = _dot(jnp.where(causal, scores, 0.0).astype(BF16), v_rows) + o_inter
            gate = gate_ref[hp, tiles[m], :]
            og_ref[hp, tiles[m], :] = jnp.concatenate(
                [_rms_gate(o[:, i * HEAD:(i + 1) * HEAD], gate[:, i * HEAD:(i + 1) * HEAD])
                 for i in range(HEADS_PER_GROUP)], axis=1)

    return start, finish


def _out_proj_ln_kernel(og_ref, x_ref, wo_ref, lng_ref, lnb_ref, y_ref):
    y = _dot(og_ref[0].astype(BF16), wo_ref[0].astype(BF16))
    for hp in range(1, N_GROUPS):
        y = y + _dot(og_ref[hp].astype(BF16), wo_ref[hp].astype(BF16))
    y_ref[...] = _layer_norm(ALPHA * x_ref[...] + y, lng_ref[...], lnb_ref[...])


def _out_proj_ln(og, x, wo3, lng, lnb):
    rows = x.shape[0]
    return pl.pallas_call(
        _out_proj_ln_kernel,
        grid=(rows // ROW_TILE,),
        in_specs=[
            pl.BlockSpec((N_GROUPS, ROW_TILE, LANE_GROUP), lambda r: (0, r, 0)),
            pl.BlockSpec((ROW_TILE, D_MODEL), lambda r: (r, 0)),
            _resident(wo3.shape), _resident(lng.shape), _resident(lnb.shape),
        ],
        out_specs=pl.BlockSpec((ROW_TILE, D_MODEL), lambda r: (r, 0)),
        out_shape=jax.ShapeDtypeStruct(x.shape, F32),
        compiler_params=pltpu.CompilerParams(
            dimension_semantics=("arbitrary",), vmem_limit_bytes=VMEM_LIMIT_BYTES),
        name="out_proj_ln",
    )(og, x, wo3, lng, lnb)


N_GMLP_INPUTS = 9
N_DECODE_INPUTS = 7


def _gmlp_kernel(*refs, emit_v, decode_seq, mix_block):
    refs = list(refs)
    gmlp_in = refs[:N_GMLP_INPUTS]
    del refs[:N_GMLP_INPUTS]
    decode_in = []
    if decode_seq:
        decode_in = refs[:N_DECODE_INPUTS]
        del refs[:N_DECODE_INPUTS]
    y_ref = refs.pop(0)
    vn_ref = refs.pop(0) if emit_v else None
    decode_out = [refs.pop(0), refs.pop(0)] if decode_seq else []
    v_scr, og_scr, h_scr, w_ref, wo_ref, sems = refs
    x_ref, w_hbm, vg_ref, vb_ref, ws_ref, bs_ref, wo_hbm, lng_ref, lnb_ref = gmlp_in
    s = pl.program_id(0)
    n_tiles = pl.num_programs(0) - 1

    @pl.when(s == 0)
    def _():
        _stage_projection_weights(w_hbm, wo_hbm, w_ref, wo_ref, v_scr, sems)
        h_scr[...] = jnp.zeros_like(h_scr)

    @pl.when(s < n_tiles)
    def _():
        decode = _decode_state_stages(*decode_in, *decode_out, decode_seq) if decode_seq else None
        _gmlp_tile_step(x_ref, w_ref, vg_ref, vb_ref, ws_ref, bs_ref, wo_ref, lng_ref, lnb_ref,
                        y_ref, vn_ref, v_scr, og_scr, h_scr, decode, mix_block)

    @pl.when(s == n_tiles)
    def _():
        y_ref[...] = _layer_norm(h_scr[...], lng_ref[...], lnb_ref[...])


def _gmlp_tile_step(x_ref, w_ref, vg_ref, vb_ref, ws_ref, bs_ref, wo_ref, lng_ref, lnb_ref,
                    y_ref, vn_ref, v_scr, og_scr, h_scr, decode, mix_block):
    y_ref[...] = _layer_norm(h_scr[...], lng_ref[...], lnb_ref[...])

    n_chunks = x_ref.shape[0] // CHUNK_B
    x = x_ref[...]
    xb = x.astype(BF16)

    half = x_ref.shape[0] // 2

    def stage_v(gp):
        for r in range(2):
            rows = slice(r * half, (r + 1) * half)
            v_scr[gp, rows] = _gelu_tanh(_dot(xb[rows], w_ref[N_GROUPS + gp]))
            yield

    for gp in range(N_GROUPS):
        _interleave([stage_v(gp)] + ([decode[0](gp)] if decode is not None else []))

    s1 = jnp.sum(v_scr[0], axis=-1, keepdims=True)
    for gp in range(1, N_GROUPS):
        s1 = s1 + jnp.sum(v_scr[gp], axis=-1, keepdims=True)
    mu = s1 * (1.0 / E)
    s2 = jnp.zeros_like(mu)
    for gp in range(N_GROUPS):
        d = v_scr[gp] - mu
        s2 = s2 + jnp.sum(d * d, axis=-1, keepdims=True)
    rstd = lax.rsqrt(s2 * (1.0 / E) + LN_EPS)

    _, causal = _block_masks(CHUNK_B, mix_block)
    ws_reps = CHUNK_B // ws_ref.shape[1]
    n_mix_groups = bs_ref.shape[0]
    bias_cols = jnp.concatenate(
        [bs_ref[...], jnp.zeros((CHUNK_B - n_mix_groups, CHUNK_B), F32)], axis=0).T

    def stage_gate(gp):
        return _dot(xb, w_ref[gp]), _dot(xb, w_ref[2 * N_GROUPS + gp])

    def stage_mix(gp, gate):
        u_pre, z = gate
        vn = (v_scr[gp] - mu) * rstd * vg_ref[gp] + vb_ref[gp]
        if vn_ref is not None:
            vn_ref[:, gp * LANE_GROUP:(gp + 1) * LANE_GROUP] = vn
        vnb = vn.astype(BF16)
        cols = []
        for i in range(HEADS_PER_GROUP):
            ls = slice(i * HEAD, (i + 1) * HEAD)
            g = HEADS_PER_GROUP * gp + i
            wg = jnp.concatenate([ws_ref[g]] * ws_reps, axis=0) if ws_reps > 1 else ws_ref[g]
            wc = jnp.where(causal, wg, 0.0).astype(BF16)
            bias = bias_cols[:, g:g + 1]
            side_by_side = jnp.concatenate(
                [vnb[c * CHUNK_B:(c + 1) * CHUNK_B, ls] for c in range(n_chunks)], axis=1)
            m = _dot(wc, side_by_side) + bias
            cols.append(jnp.concatenate([m[:, c * HEAD:(c + 1) * HEAD] for c in range(n_chunks)], axis=0))
        mixed = jnp.concatenate(cols, axis=1)
        og_scr[gp] = (_gelu_tanh(u_pre) * mixed * _silu(z)).astype(BF16)

    def out_proj(groups):
        y = _dot(og_scr[groups[0]], wo_ref[groups[0]])
        for gp in groups[1:]:
            y = y + _dot(og_scr[gp], wo_ref[gp])
        return y

    y_parts = []
    gates = {0: stage_gate(0)}
    for gp in range(N_GROUPS):
        if gp + 1 < N_GROUPS:
            gates[gp + 1] = stage_gate(gp + 1)
        stage_mix(gp, gates.pop(gp))
        if decode is not None:
            decode[1](gp)
        if gp == N_GROUPS - 2:
            y_parts.append(out_proj(list(range(0, N_GROUPS - 3))))
    y_parts.append(out_proj([N_GROUPS - 3, N_GROUPS - 2]))
    y_parts.append(out_proj([N_GROUPS - 1]))
    y = y_parts[0] + y_parts[1] + y_parts[2]
    h_scr[...] = ALPHA * x + y


def _gmlp(x, w_in, vg3, vb3, ws, bs_rows, w_out, lng, lnb, emit_v, mix_block=CHUNK_B, decode=None,
          decode_seq=0):
    rows = x.shape[0]
    tile = min(GMLP_TILE, rows)
    assert tile == STAGE_ROWS and N_GROUPS == STAGE_SLOTS
    n_tiles = rows // tile
    tile_in = lambda s: jnp.minimum(s, n_tiles - 1)
    tile_out = lambda s: jnp.maximum(s - 1, 0)
    hbm = pl.BlockSpec(memory_space=pl.ANY)
    in_specs = [
        pl.BlockSpec((tile, D_MODEL), lambda s: (tile_in(s), 0)),
        hbm, _resident(vg3.shape), _resident(vb3.shape), _resident(ws.shape),
        _resident(bs_rows.shape), hbm, _resident(lng.shape), _resident(lnb.shape),
    ]
    operands = [x, w_in, vg3, vb3, ws, bs_rows, w_out, lng, lnb]
    w_scr, wo_scr, _, sems = _weight_scratch(w_in, w_out)
    assert len(operands) == N_GMLP_INPUTS
    out_specs = [pl.BlockSpec((tile, D_MODEL), lambda s: (tile_out(s), 0))]
    out_shape = [jax.ShapeDtypeStruct(x.shape, F32)]
    if emit_v:
        out_specs.append(pl.BlockSpec((tile, E), lambda s: (tile_in(s), 0)))
        out_shape.append(jax.ShapeDtypeStruct((rows, E), F32))
    if decode is not None:
        assert len(decode) == N_DECODE_INPUTS
        state = decode[-1]
        n_seq = state.shape[0]
        seqs_per_step = n_seq // n_tiles
        rows_per_step = seqs_per_step * decode_seq
        assert n_seq % n_tiles == 0 and rows_per_step % SUBLANES == 0 and SUBLANES % decode_seq == 0
        tok_spec = pl.BlockSpec((N_GROUPS, rows_per_step, LANE_GROUP), lambda s: (0, tile_in(s), 0))
        st_spec = pl.BlockSpec((seqs_per_step, N_HEADS, HEAD, HEAD), lambda s: (tile_in(s), 0, 0, 0))
        in_specs += [tok_spec] * (N_DECODE_INPUTS - 1) + [st_spec]
        operands += list(decode)
        out_specs += [tok_spec, st_spec]
        out_shape += [jax.ShapeDtypeStruct(decode[0].shape, F32), jax.ShapeDtypeStruct(state.shape, F32)]
    return pl.pallas_call(
        functools.partial(_gmlp_kernel, emit_v=emit_v, decode_seq=decode_seq if decode is not None else 0,
                          mix_block=mix_block),
        grid=(n_tiles + 1,),
        in_specs=in_specs,
        out_specs=out_specs,
        out_shape=out_shape,
        scratch_shapes=[
            pltpu.VMEM((N_GROUPS, tile, LANE_GROUP), F32),
            pltpu.VMEM((N_GROUPS, tile, LANE_GROUP), BF16),
            pltpu.VMEM((tile, D_MODEL), F32),
            w_scr, wo_scr, sems,
        ],
        compiler_params=pltpu.CompilerParams(
            dimension_semantics=("arbitrary",), vmem_limit_bytes=VMEM_LIMIT_BYTES),
        name="gmlp_emit_v" if emit_v else "gmlp",
    )(*operands)


def _lane_groups_of_rows(w):
    return w.reshape(w.shape[0] // LANE_GROUP, LANE_GROUP, w.shape[1])


def _lane_groups_of_vector(v):
    return v.reshape(v.shape[0], N_GROUPS, LANE_GROUP).transpose(1, 0, 2)


def kernel(x_prompt, x_sample, state_hgrn, w_in_a, lb_logits_a, gnorm_a, w_out_a, w_in_b, lnv_g_b,
           lnv_b_b, w_s_b, b_s_b, w_out_b, ln_g, ln_b):
    bsz, seq, _ = x_prompt.shape
    n_seq, dec_seq, _ = x_sample.shape

    wa, woa = w_in_a[0], w_out_a[0]
    lbl3 = _lane_groups_of_vector(lb_logits_a)
    gn3 = _lane_groups_of_vector(gnorm_a[0:1])
    lng0, lnb0 = ln_g[0:1], ln_b[0:1]

    hp, sp = _hgrn_prompt(x_prompt, wa, lbl3, gn3, woa, lng0, lnb0, layer=0)

    xs = x_sample.reshape(n_seq * dec_seq, D_MODEL)
    decode_gates = _hgrn_sample_gates(xs, wa, lbl3, gn3, layer=0, dec_seq=dec_seq)

    w3b, wo3b = w_in_b[0], w_out_b[0]
    vg3 = _lane_groups_of_vector(lnv_g_b[0:1])
    vb3 = _lane_groups_of_vector(lnv_b_b[0:1])
    lng1, lnb1 = ln_g[1:2], ln_b[1:2]
    ws = w_s_b[0]
    bs = b_s_b[0]

    yp, og, ss = _gmlp(hp.reshape(bsz * seq, D_MODEL), w3b, vg3, vb3, ws, bs, wo3b, lng1, lnb1,
                       emit_v=False, decode=(*decode_gates, state_hgrn[0]), decode_seq=dec_seq)
    yp = yp.reshape(bsz, seq, D_MODEL)
    hs = _out_proj_ln(og, xs, _lane_groups_of_rows(woa), lng0, lnb0)

    reps = CHUNK_B // dec_seq
    ws_dec = jnp.tile(ws[:, :dec_seq, :dec_seq], (1, SUBLANES // dec_seq, reps))
    bs_dec = jnp.tile(bs[:, :dec_seq], (1, reps))
    ys, vn = _gmlp(hs, w3b, vg3, vb3, ws_dec, bs_dec, wo3b, lng1, lnb1, emit_v=True, mix_block=dec_seq)
    ys = ys.reshape(n_seq, dec_seq, D_MODEL)
    vs = vn.reshape(n_seq, dec_seq, E)

    return (yp, ys, sp[None], ss[None], vs[None])
```

```python
import functools
import math

import jax
import jax.numpy as jnp
from jax import lax
from jax.experimental import pallas as pl
from jax.experimental.pallas import tpu as pltpu

F32 = jnp.float32
BF16 = jnp.bfloat16

D_MODEL = 1024
E = 2048
HEAD = 128
N_HEADS = E // HEAD
LANE_GROUP = 256
N_GROUPS = E // LANE_GROUP
HEADS_PER_GROUP = LANE_GROUP // HEAD
CHUNK_A = 64
CHUNK_B = 128
ROW_TILE = 256
HGRN_TILE = 512
HGRN_BLOCK = 2 * CHUNK_A
GMLP_TILE = 512
SUBLANES = 8
DEPTH = 2
ALPHA = (2 * DEPTH) ** 0.25
LN_EPS = 1e-5
VMEM_LIMIT_BYTES = 60000 * 1024


def _dot(a, b):
    return jnp.dot(a, b, preferred_element_type=F32)


STAGE_ROWS = 512
STAGE_SLOTS = 8


def _stage_weight(w_hbm, stage, sems, store):
    n_row_blocks = w_hbm.shape[0] // STAGE_ROWS
    n_col_blocks = w_hbm.shape[1] // LANE_GROUP
    chunks = [(r, c) for c in range(n_col_blocks) for r in range(n_row_blocks)]

    def copy(i):
        r, c = chunks[i]
        slot = i % STAGE_SLOTS
        return pltpu.make_async_copy(
            w_hbm.at[pl.ds(r * STAGE_ROWS, STAGE_ROWS), pl.ds(c * LANE_GROUP, LANE_GROUP)],
            stage.at[slot], sems.at[slot])

    for i in range(min(STAGE_SLOTS, len(chunks))):
        copy(i).start()
    for i, (r, c) in enumerate(chunks):
        copy(i).wait()
        store(r, c, stage[i % STAGE_SLOTS].astype(BF16))
        if i + STAGE_SLOTS < len(chunks):
            copy(i + STAGE_SLOTS).start()


def _stage_projection_weights(w_in_hbm, w_out_hbm, w_scr, wo_scr, stage, sems):
    def store_in(r, c, chunk):
        w_scr[c, r * STAGE_ROWS:(r + 1) * STAGE_ROWS, :] = chunk

    def store_out(r, c, chunk):
        for k in range(STAGE_ROWS // LANE_GROUP):
            wo_scr[r * (STAGE_ROWS // LANE_GROUP) + k, :, c * LANE_GROUP:(c + 1) * LANE_GROUP] = (
                chunk[k * LANE_GROUP:(k + 1) * LANE_GROUP])

    _stage_weight(w_in_hbm, stage, sems, store_in)
    _stage_weight(w_out_hbm, stage, sems, store_out)


def _interleave(stages):
    stages = list(stages)
    while stages:
        for stage in list(stages):
            if next(stage, stages) is stages:
                stages.remove(stage)


def _dot_nt(a, b):
    return lax.dot_general(a, b, (((1,), (1,)), ((), ())), preferred_element_type=F32)


def _dot_tn(a, b):
    return lax.dot_general(a, b, (((0,), (0,)), ((), ())), preferred_element_type=F32)


def _silu(x):
    return x / (1.0 + jnp.exp(-x))


def _gelu_tanh(x):
    cdf = 0.5 * (1.0 + jnp.tanh(math.sqrt(2.0 / math.pi) * (x + 0.044715 * (x * x * x))))
    return x * cdf


def _layer_norm(x, g, b):
    mu = jnp.mean(x, axis=-1, keepdims=True)
    d = x - mu
    var = jnp.mean(d * d, axis=-1, keepdims=True)
    return d * lax.rsqrt(var + LN_EPS) * g + b


def _block_masks(n, block):
    shift = block.bit_length() - 1
    row = lax.broadcasted_iota(jnp.int32, (n, n), 0)
    col = lax.broadcasted_iota(jnp.int32, (n, n), 1)
    same = jnp.right_shift(row, shift) == jnp.right_shift(col, shift)
    causal = jnp.logical_and(same, col <= row)
    return same, causal


def _split_dot(m, x):
    hi = x.astype(BF16)
    lo = (x - hi.astype(F32)).astype(BF16)
    return _dot(jnp.concatenate([m, m], axis=1), jnp.concatenate([hi, lo], axis=0))


def _forget_lower_bound(logits, layer):
    m = jnp.max(logits, axis=0, keepdims=True)
    e = jnp.exp(logits - m)
    den = jnp.sum(e, axis=0, keepdims=True)
    num = jnp.sum(e[: layer + 1], axis=0, keepdims=True)
    return num / den


def _rms_gate(o, gate):
    ms = jnp.mean(o * o, axis=-1, keepdims=True)
    return o * lax.rsqrt(ms + LN_EPS) * gate


def _hgrn_prompt_kernel(x_ref, w_hbm, lbl_ref, gn_ref, wo_hbm, lng_ref, lnb_ref,
                        y_ref, s_ref, st_scr, og_scr, h_scr, w_ref, wo_ref, stage, sems,
                        *, layer, tiles_per_row):
    s = pl.program_id(0)
    n_tiles = pl.num_programs(0) - 1

    @pl.when(s == 0)
    def _():
        _stage_projection_weights(w_hbm, wo_hbm, w_ref, wo_ref, stage, sems)
        h_scr[...] = jnp.zeros_like(h_scr)

    @pl.when(s < n_tiles)
    def _():
        _hgrn_tile_step(lax.rem(s, tiles_per_row), tiles_per_row, x_ref, w_ref, lbl_ref, gn_ref, wo_ref,
                        lng_ref, lnb_ref, y_ref, s_ref, st_scr, og_scr, h_scr, layer=layer)

    @pl.when(s == n_tiles)
    def _():
        y_ref[0] = _layer_norm(h_scr[...], lng_ref[...], lnb_ref[...])


def _hgrn_tile_step(j, tiles_per_row, x_ref, w_ref, lbl_ref, gn_ref, wo_ref, lng_ref, lnb_ref,
                    y_ref, s_ref, st_scr, og_scr, h_scr, *, layer):
    tile = x_ref.shape[1]
    n_blocks = tile // HGRN_BLOCK
    n_chunks = tile // CHUNK_A
    chunks_per_block = HGRN_BLOCK // CHUNK_A

    @pl.when(j == 0)
    def _():
        st_scr[...] = jnp.zeros_like(st_scr)

    y_ref[0] = _layer_norm(h_scr[...], lng_ref[...], lnb_ref[...])

    x = x_ref[0]
    xb = x.astype(BF16)
    _, causal = _block_masks(HGRN_BLOCK, CHUNK_A)
    tri = jnp.where(causal, 1.0, 0.0).astype(BF16)

    def block(p):
        return slice(p * HGRN_BLOCK, (p + 1) * HGRN_BLOCK)

    projected, scored = {}, {}

    def stage_project(hp):
        z = _dot(xb, w_ref[N_GROUPS + hp])
        yield
        qp = _dot(xb, w_ref[hp])
        yield
        v = _dot(xb, w_ref[2 * N_GROUPS + hp])
        yield
        g = _dot(xb, w_ref[3 * N_GROUPS + hp])
        yield
        lb = _forget_lower_bound(lbl_ref[hp], layer)
        c1 = 1.0 - lb
        cs = c1 / (1.0 + jnp.exp(-z))
        k = c1 - cs
        lf = jnp.log(lb + cs)
        cum = jnp.concatenate([_split_dot(tri, lf[block(p)]) for p in range(n_blocks)], axis=0)
        projected[hp] = (_silu(qp), k, cum, v.astype(BF16), _silu(g) * gn_ref[hp])

    def stage_scores(hp):
        q, k, cum, vb, gate = projected.pop(hp)
        qdb = (q * jnp.exp(cum)).astype(BF16)
        ki = k * jnp.exp(-cum)
        dl_rows = jnp.exp(jnp.concatenate(
            [cum[(c + 1) * CHUNK_A - 1:(c + 1) * CHUNK_A] for c in range(n_chunks)]
            + [jnp.zeros((HEAD - n_chunks, LANE_GROUP), F32)], axis=0))
        dl_cols = dl_rows.T
        heads = []
        for i in range(HEADS_PER_GROUP):
            ls = slice(i * HEAD, (i + 1) * HEAD)
            kit = ki[:, ls].T.astype(BF16)
            scores = [_dot(qdb[block(p), ls], kit[:, block(p)]) for p in range(n_blocks)]
            yield
            dls, us = [], []
            for c in range(n_chunks):
                lo, hi = c * CHUNK_A, (c + 1) * CHUNK_A
                ke = (ki[lo:hi, ls] * dl_rows[c:c + 1, ls]).astype(BF16)
                dls.append(dl_cols[ls, c:c + 1])
                us.append(_dot_tn(ke, vb[lo:hi, ls]))
                if c % chunks_per_block == chunks_per_block - 1:
                    yield
            heads.append((qdb[:, ls], vb[:, ls], scores, dls, us))
        scored[hp] = (heads, gate)

    def stage_output(hp):
        heads, gate = scored.pop(hp)
        ogs = []
        for i, (qdb, vb, scores, dls, us) in enumerate(heads):
            st = st_scr[HEADS_PER_GROUP * hp + i]
            os = []
            for p in range(n_blocks):
                o_intra = _dot(jnp.where(causal, scores[p], 0.0).astype(BF16), vb[block(p)])
                for cc in range(chunks_per_block):
                    c = p * chunks_per_block + cc
                    lo, hi = c * CHUNK_A, (c + 1) * CHUNK_A
                    os.append(o_intra[cc * CHUNK_A:(cc + 1) * CHUNK_A] + _dot(qdb[lo:hi], st.astype(BF16)))
                    st = st * dls[c] + us[c]
                yield
            st_scr[HEADS_PER_GROUP * hp + i] = st
            o = jnp.concatenate(os, axis=0)
            ogs.append(_rms_gate(o, gate[:, i * HEAD:(i + 1) * HEAD]))
        og_scr[hp] = jnp.concatenate(ogs, axis=1).astype(BF16)

    y_parts = []

    def out_proj(groups):
        y = _dot(og_scr[groups[0]], wo_ref[groups[0]])
        for hp in groups[1:]:
            yield
            y = y + _dot(og_scr[hp], wo_ref[hp])
        y_parts.append(y)

    for t in range(N_GROUPS + 2):
        stages = []
        if t < N_GROUPS:
            stages.append(stage_project(t))
        if t == N_GROUPS:
            stages.append(out_proj(list(range(0, N_GROUPS - 4))))
        if t == N_GROUPS + 1:
            stages.append(out_proj([N_GROUPS - 4, N_GROUPS - 3]))
        if 0 <= t - 2 < N_GROUPS:
            stages.append(stage_output(t - 2))
        if 0 <= t - 1 < N_GROUPS:
            stages.append(stage_scores(t - 1))
        _interleave(stages)
    _interleave([out_proj([N_GROUPS - 2, N_GROUPS - 1])])
    y = y_parts[0] + y_parts[1] + y_parts[2]
    h_scr[...] = ALPHA * x + y

    @pl.when(j == tiles_per_row - 1)
    def _():
        s_ref[0] = st_scr[...]


def _resident(shape):
    nd = len(shape)
    return pl.BlockSpec(shape, lambda *_: (0,) * nd, pipeline_mode=pl.Buffered(1))


def _weight_scratch(w_in, w_out):
    k, n = w_in.shape
    return [
        pltpu.VMEM((n // LANE_GROUP, k, LANE_GROUP), BF16),
        pltpu.VMEM((w_out.shape[0] // LANE_GROUP, LANE_GROUP, w_out.shape[1]), BF16),
        pltpu.VMEM((STAGE_SLOTS, STAGE_ROWS, LANE_GROUP), F32),
        pltpu.SemaphoreType.DMA((STAGE_SLOTS,)),
    ]


def _hgrn_prompt(x, w_in, lbl3, gn3, w_out, lng, lnb, layer):
    bsz, seq, _ = x.shape
    tiles_per_row = seq // HGRN_TILE
    n_tiles = bsz * tiles_per_row
    hbm = pl.BlockSpec(memory_space=pl.ANY)

    def tile_in(s):
        t = jnp.minimum(s, n_tiles - 1)
        return t // tiles_per_row, t % tiles_per_row

    def tile_out(s):
        t = jnp.maximum(s - 1, 0)
        return t // tiles_per_row, t % tiles_per_row

    return pl.pallas_call(
        functools.partial(_hgrn_prompt_kernel, layer=layer, tiles_per_row=tiles_per_row),
        grid=(n_tiles + 1,),
        in_specs=[
            pl.BlockSpec((1, HGRN_TILE, D_MODEL), lambda s: (*tile_in(s), 0)),
            hbm, _resident(lbl3.shape), _resident(gn3.shape), hbm,
            _resident(lng.shape), _resident(lnb.shape),
        ],
        out_specs=[
            pl.BlockSpec((1, HGRN_TILE, D_MODEL), lambda s: (*tile_out(s), 0)),
            pl.BlockSpec((1, N_HEADS, HEAD, HEAD), lambda s: (tile_in(s)[0], 0, 0, 0)),
        ],
        out_shape=[
            jax.ShapeDtypeStruct(x.shape, F32),
            jax.ShapeDtypeStruct((bsz, N_HEADS, HEAD, HEAD), F32),
        ],
        scratch_shapes=[
            pltpu.VMEM((N_HEADS, HEAD, HEAD), F32),
            pltpu.VMEM((N_GROUPS, HGRN_TILE, LANE_GROUP), BF16),
            pltpu.VMEM((HGRN_TILE, D_MODEL), F32),
            *_weight_scratch(w_in, w_out),
        ],
        compiler_params=pltpu.CompilerParams(
            dimension_semantics=("arbitrary",), vmem_limit_bytes=VMEM_LIMIT_BYTES),
        name="hgrn_prompt",
    )(x, w_in, lbl3, gn3, w_out, lng, lnb)


def _hgrn_sample_gates_kernel(x_ref, wq_ref, wf_ref, wi_ref, wg_ref, lbl_ref, gn_ref,
                              qd_ref, ki_ref, ke_ref, v_ref, gate_ref, dl_ref, *, layer, dec_seq):
    rows = x_ref.shape[0]
    xb = x_ref[...].astype(BF16)
    same, causal = _block_masks(ROW_TILE, dec_seq)
    tri = jnp.where(causal, 1.0, 0.0).astype(BF16)
    blk = jnp.where(same, 1.0, 0.0).astype(BF16)
    lb = _forget_lower_bound(lbl_ref[0], layer)

    qp = _dot(xb, wq_ref[...].astype(BF16))
    q = _silu(qp)
    z = _dot(xb, wf_ref[...].astype(BF16))
    c1 = 1.0 - lb
    cs = c1 / (1.0 + jnp.exp(-z))
    k = c1 - cs
    lf = jnp.log(lb + cs)
    for r in range(rows // ROW_TILE):
        rs = slice(r * ROW_TILE, (r + 1) * ROW_TILE)
        cum = _split_dot(tri, lf[rs])
        cum_last = _split_dot(blk, lf[rs])
        ki = k[rs] * jnp.exp(-cum)
        dl = jnp.exp(cum_last)
        qd_ref[0, rs] = q[rs] * jnp.exp(cum)
        ki_ref[0, rs] = ki
        ke_ref[0, rs] = ki * dl
        dl_ref[0, rs] = dl
    v_ref[0] = _dot(xb, wi_ref[...].astype(BF16))
    gate_ref[0] = _silu(_dot(xb, wg_ref[...].astype(BF16))) * gn_ref[0]


def _hgrn_sample_gates(x_rows, w3, lbl3, gn3, layer, dec_seq):
    rows = x_rows.shape[0]
    assert rows % ROW_TILE == 0 and SUBLANES % dec_seq == 0
    w_spec = lambda part: pl.BlockSpec((D_MODEL, LANE_GROUP), lambda g: (0, part * N_GROUPS + g))
    out_spec = pl.BlockSpec((1, rows, LANE_GROUP), lambda g: (g, 0, 0))
    out_shape = jax.ShapeDtypeStruct((N_GROUPS, rows, LANE_GROUP), F32)
    return pl.pallas_call(
        functools.partial(_hgrn_sample_gates_kernel, layer=layer, dec_seq=dec_seq),
        grid=(N_GROUPS,),
        in_specs=[
            pl.BlockSpec((rows, D_MODEL), lambda g: (0, 0)),
            w_spec(0), w_spec(1), w_spec(2), w_spec(3),
            pl.BlockSpec((1,) + lbl3.shape[1:], lambda g: (g, 0, 0)),
            pl.BlockSpec((1, 1, LANE_GROUP), lambda g: (g, 0, 0)),
        ],
        out_specs=[out_spec] * 6,
        out_shape=[out_shape] * 6,
        compiler_params=pltpu.CompilerParams(
            dimension_semantics=("arbitrary",), vmem_limit_bytes=VMEM_LIMIT_BYTES),
        name="hgrn_sample_gates",
    )(x_rows, w3, w3, w3, w3, lbl3, gn3)


def _decode_state_stages(qd_ref, ki_ref, ke_ref, v_ref, gate_ref, dl_ref, s_ref, og_ref, so_ref, dec_seq):
    seqs_per_tile = SUBLANES // dec_seq
    n_tiles = qd_ref.shape[1] // SUBLANES
    shift = dec_seq.bit_length() - 1
    trow = jnp.right_shift(lax.broadcasted_iota(jnp.int32, (SUBLANES, HEAD), 0), shift)
    in_seq = [trow == s for s in range(seqs_per_tile)]
    pad_rows = jnp.zeros((HEAD - N_HEADS, HEAD), F32)
    tiles = [slice(m * SUBLANES, (m + 1) * SUBLANES) for m in range(n_tiles)]
    dcols, pending = {}, {}
    lane_head = lax.broadcasted_iota(jnp.int32, (SUBLANES, LANE_GROUP), 1) // HEAD
    srow = lax.broadcasted_iota(jnp.int32, (SUBLANES, HEADS_PER_GROUP * SUBLANES), 0)
    scol = jnp.bitwise_and(lax.broadcasted_iota(jnp.int32, (SUBLANES, HEADS_PER_GROUP * SUBLANES), 1),
                           SUBLANES - 1)
    causal = jnp.logical_and(jnp.right_shift(srow, shift) == jnp.right_shift(scol, shift), scol <= srow)

    def per_head_rows(t):
        return jnp.concatenate([jnp.where(lane_head == i, t, 0.0) for i in range(HEADS_PER_GROUP)], axis=0)

    def per_seq_lanes(t):
        return jnp.concatenate([jnp.where(in_seq[s], t, 0.0) for s in range(seqs_per_tile)], axis=1)

    def decay_columns(m):
        dl = [dl_ref[hp, tiles[m], :] for hp in range(N_GROUPS)]
        for s in range(seqs_per_tile):
            r = s * dec_seq
            rows = [dl[h // HEADS_PER_GROUP][r:r + 1, (h % HEADS_PER_GROUP) * HEAD:(h % HEADS_PER_GROUP + 1) * HEAD]
                    for h in range(N_HEADS)]
            dcols[(m, s)] = jnp.concatenate(rows + [pad_rows], axis=0).T

    def start(hp):
        for m in range(n_tiles):
            if hp == 0:
                decay_columns(m)
            rs = tiles[m]
            qd = qd_ref[hp, rs, :]
            ke = ke_ref[hp, rs, :]
            v = v_ref[hp, rs, :]
            qdb = qd.astype(BF16)
            scores = _dot_nt(qdb, per_head_rows(ki_ref[hp, rs, :]).astype(BF16))
            o_inter = []
            for i in range(HEADS_PER_GROUP):
                ls = slice(i * HEAD, (i + 1) * HEAD)
                h = HEADS_PER_GROUP * hp + i
                sts = [s_ref[m * seqs_per_tile + s, h] for s in range(seqs_per_tile)]
                st_rows = jnp.concatenate([st.astype(BF16) for st in sts], axis=0)
                o_inter.append(_dot(per_seq_lanes(qd[:, ls]).astype(BF16), st_rows))
                u = _dot_tn(per_seq_lanes(ke[:, ls]).astype(BF16), v[:, ls].astype(BF16))
                for s in range(seqs_per_tile):
                    so_ref[m * seqs_per_tile + s, h] = (dcols[(m, s)][:, h:h + 1] * sts[s]
                                                       + u[s * HEAD:(s + 1) * HEAD])
            pending[(hp, m)] = (scores, jnp.concatenate(o_inter, axis=1), per_head_rows(v).astype(BF16))
            yield

    def finish(hp):
        for m in range(n_tiles):
            scores, o_inter, v_rows = pending.pop((hp, m))
            o = _dot(jnp.where(causal, scores, 0.0).astype(BF16), v_rows) + o_inter
            gate = gate_ref[hp, tiles[m], :]
            og_ref[hp, tiles[m], :] = jnp.concatenate(
                [_rms_gate(o[:, i * HEAD:(i + 1) * HEAD], gate[:, i * HEAD:(i + 1) * HEAD])
                 for i in range(HEADS_PER_GROUP)], axis=1)

    return start, finish


def _out_proj_ln_kernel(og_ref, x_ref, wo_ref, lng_ref, lnb_ref, y_ref):
    y = _dot(og_ref[0].astype(BF16), wo_ref[0].astype(BF16))
    for hp in range(1, N_GROUPS):
        y = y + _dot(og_ref[hp].astype(BF16), wo_ref[hp].astype(BF16))
    y_ref[...] = _layer_norm(ALPHA * x_ref[...] + y, lng_ref[...], lnb_ref[...])


def _out_proj_ln(og, x, wo3, lng, lnb):
    rows = x.shape[0]
    return pl.pallas_call(
        _out_proj_ln_kernel,
        grid=(rows // ROW_TILE,),
        in_specs=[
            pl.BlockSpec((N_GROUPS, ROW_TILE, LANE_GROUP), lambda r: (0, r, 0)),
            pl.BlockSpec((ROW_TILE, D_MODEL), lambda r: (r, 0)),
            _resident(wo3.shape), _resident(lng.shape), _resident(lnb.shape),
        ],
        out_specs=pl.BlockSpec((ROW_TILE, D_MODEL), lambda r: (r, 0)),
        out_shape=jax.ShapeDtypeStruct(x.shape, F32),
        compiler_params=pltpu.CompilerParams(
            dimension_semantics=("arbitrary",), vmem_limit_bytes=VMEM_LIMIT_BYTES),
        name="out_proj_ln",
    )(og, x, wo3, lng, lnb)


N_GMLP_INPUTS = 9
N_DECODE_INPUTS = 7


def _gmlp_kernel(*refs, emit_v, decode_seq, mix_block):
    refs = list(refs)
    gmlp_in = refs[:N_GMLP_INPUTS]
    del refs[:N_GMLP_INPUTS]
    decode_in = []
    if decode_seq:
        decode_in = refs[:N_DECODE_INPUTS]
        del refs[:N_DECODE_INPUTS]
    y_ref = refs.pop(0)
    vn_ref = refs.pop(0) if emit_v else None
    decode_out = [refs.pop(0), refs.pop(0)] if decode_seq else []
    v_scr, og_scr, h_scr, w_ref, wo_ref, sems = refs
    x_ref, w_hbm, vg_ref, vb_ref, ws_ref, bs_ref, wo_hbm, lng_ref, lnb_ref = gmlp_in
    s = pl.program_id(0)
    n_tiles = pl.num_programs(0) - 1

    @pl.when(s == 0)
    def _():
        _stage_projection_weights(w_hbm, wo_hbm, w_ref, wo_ref, v_scr, sems)
        h_scr[...] = jnp.zeros_like(h_scr)

    @pl.when(s < n_tiles)
    def _():
        decode = _decode_state_stages(*decode_in, *decode_out, decode_seq) if decode_seq else None
        _gmlp_tile_step(x_ref, w_ref, vg_ref, vb_ref, ws_ref, bs_ref, wo_ref, lng_ref, lnb_ref,
                        y_ref, vn_ref, v_scr, og_scr, h_scr, decode, mix_block)

    @pl.when(s == n_tiles)
    def _():
        y_ref[...] = _layer_norm(h_scr[...], lng_ref[...], lnb_ref[...])


def _gmlp_tile_step(x_ref, w_ref, vg_ref, vb_ref, ws_ref, bs_ref, wo_ref, lng_ref, lnb_ref,
                    y_ref, vn_ref, v_scr, og_scr, h_scr, decode, mix_block):
    y_ref[...] = _layer_norm(h_scr[...], lng_ref[...], lnb_ref[...])

    n_chunks = x_ref.shape[0] // CHUNK_B
    x = x_ref[...]
    xb = x.astype(BF16)

    half = x_ref.shape[0] // 2

    def stage_v(gp):
        for r in range(2):
            rows = slice(r * half, (r + 1) * half)
            v_scr[gp, rows] = _gelu_tanh(_dot(xb[rows], w_ref[N_GROUPS + gp]))
            yield

    for gp in range(N_GROUPS):
        _interleave([stage_v(gp)] + ([decode[0](gp)] if decode is not None else []))

    s1 = jnp.sum(v_scr[0], axis=-1, keepdims=True)
    for gp in range(1, N_GROUPS):
        s1 = s1 + jnp.sum(v_scr[gp], axis=-1, keepdims=True)
    mu = s1 * (1.0 / E)
    s2 = jnp.zeros_like(mu)
    for gp in range(N_GROUPS):
        d = v_scr[gp] - mu
        s2 = s2 + jnp.sum(d * d, axis=-1, keepdims=True)
    rstd = lax.rsqrt(s2 * (1.0 / E) + LN_EPS)

    _, causal = _block_masks(CHUNK_B, mix_block)
    ws_reps = CHUNK_B // ws_ref.shape[1]
    n_mix_groups = bs_ref.shape[0]
    bias_cols = jnp.concatenate(
        [bs_ref[...], jnp.zeros((CHUNK_B - n_mix_groups, CHUNK_B), F32)], axis=0).T

    def stage_gate(gp):
        return _dot(xb, w_ref[gp]), _dot(xb, w_ref[2 * N_GROUPS + gp])

    def stage_mix(gp, gate):
        u_pre, z = gate
        vn = (v_scr[gp] - mu) * rstd * vg_ref[gp] + vb_ref[gp]
        if vn_ref is not None:
            vn_ref[:, gp * LANE_GROUP:(gp + 1) * LANE_GROUP] = vn
        vnb = vn.astype(BF16)
        cols = []
        for i in range(HEADS_PER_GROUP):
            ls = slice(i * HEAD, (i + 1) * HEAD)
            g = HEADS_PER_GROUP * gp + i
            wg = jnp.concatenate([ws_ref[g]] * ws_reps, axis=0) if ws_reps > 1 else ws_ref[g]
            wc = jnp.where(causal, wg, 0.0).astype(BF16)
            bias = bias_cols[:, g:g + 1]
            side_by_side = jnp.concatenate(
                [vnb[c * CHUNK_B:(c + 1) * CHUNK_B, ls] for c in range(n_chunks)], axis=1)
            m = _dot(wc, side_by_side) + bias
            cols.append(jnp.concatenate([m[:, c * HEAD:(c + 1) * HEAD] for c in range(n_chunks)], axis=0))
        mixed = jnp.concatenate(cols, axis=1)
        og_scr[gp] = (_gelu_tanh(u_pre) * mixed * _silu(z)).astype(BF16)

    def out_proj(groups):
        y = _dot(og_scr[groups[0]], wo_ref[groups[0]])
        for gp in groups[1:]:
            y = y + _dot(og_scr[gp], wo_ref[gp])
        return y

    y_parts = []
    gates = {0: stage_gate(0)}
    for gp in range(N_GROUPS):
        if gp + 1 < N_GROUPS:
            gates[gp + 1] = stage_gate(gp + 1)
        stage_mix(gp, gates.pop(gp))
        if decode is not None:
            decode[1](gp)
        if gp == N_GROUPS - 2:
            y_parts.append(out_proj(list(range(0, N_GROUPS - 3))))
    y_parts.append(out_proj([N_GROUPS - 3, N_GROUPS - 2]))
    y_parts.append(out_proj([N_GROUPS - 1]))
    y = y_parts[0] + y_parts[1] + y_parts[2]
    h_scr[...] = ALPHA * x + y


def _gmlp(x, w_in, vg3, vb3, ws, bs_rows, w_out, lng, lnb, emit_v, mix_block=CHUNK_B, decode=None,
          decode_seq=0):
    rows = x.shape[0]
    tile = min(GMLP_TILE, rows)
    assert tile == STAGE_ROWS and N_GROUPS == STAGE_SLOTS
    n_tiles = rows // tile
    tile_in = lambda s: jnp.minimum(s, n_tiles - 1)
    tile_out = lambda s: jnp.maximum(s - 1, 0)
    hbm = pl.BlockSpec(memory_space=pl.ANY)
    in_specs = [
        pl.BlockSpec((tile, D_MODEL), lambda s: (tile_in(s), 0)),
        hbm, _resident(vg3.shape), _resident(vb3.shape), _resident(ws.shape),
        _resident(bs_rows.shape), hbm, _resident(lng.shape), _resident(lnb.shape),
    ]
    operands = [x, w_in, vg3, vb3, ws, bs_rows, w_out, lng, lnb]
    w_scr, wo_scr, _, sems = _weight_scratch(w_in, w_out)
    assert len(operands) == N_GMLP_INPUTS
    out_specs = [pl.BlockSpec((tile, D_MODEL), lambda s: (tile_out(s), 0))]
    out_shape = [jax.ShapeDtypeStruct(x.shape, F32)]
    if emit_v:
        out_specs.append(pl.BlockSpec((tile, E), lambda s: (tile_in(s), 0)))
        out_shape.append(jax.ShapeDtypeStruct((rows, E), F32))
    if decode is not None:
        assert len(decode) == N_DECODE_INPUTS
        state = decode[-1]
        n_seq = state.shape[0]
        seqs_per_step = n_seq // n_tiles
        rows_per_step = seqs_per_step * decode_seq
        assert n_seq % n_tiles == 0 and rows_per_step % SUBLANES == 0 and SUBLANES % decode_seq == 0
        tok_spec = pl.BlockSpec((N_GROUPS, rows_per_step, LANE_GROUP), lambda s: (0, tile_in(s), 0))
        st_spec = pl.BlockSpec((seqs_per_step, N_HEADS, HEAD, HEAD), lambda s: (tile_in(s), 0, 0, 0))
        in_specs += [tok_spec] * (N_DECODE_INPUTS - 1) + [st_spec]
        operands += list(decode)
        out_specs += [tok_spec, st_spec]
        out_shape += [jax.ShapeDtypeStruct(decode[0].shape, F32), jax.ShapeDtypeStruct(state.shape, F32)]
    return pl.pallas_call(
        functools.partial(_gmlp_kernel, emit_v=emit_v, decode_seq=decode_seq if decode is not None else 0,
                          mix_block=mix_block),
        grid=(n_tiles + 1,),
        in_specs=in_specs,
        out_specs=out_specs,
        out_shape=out_shape,
        scratch_shapes=[
            pltpu.VMEM((N_GROUPS, tile, LANE_GROUP), F32),
            pltpu.VMEM((N_GROUPS, tile, LANE_GROUP), BF16),
            pltpu.VMEM((tile, D_MODEL), F32),
            w_scr, wo_scr, sems,
        ],
        compiler_params=pltpu.CompilerParams(
            dimension_semantics=("arbitrary",), vmem_limit_bytes=VMEM_LIMIT_BYTES),
        name="gmlp_emit_v" if emit_v else "gmlp",
    )(*operands)


def _lane_groups_of_rows(w):
    return w.reshape(w.shape[0] // LANE_GROUP, LANE_GROUP, w.shape[1])


def _lane_groups_of_vector(v):
    return v.reshape(v.shape[0], N_GROUPS, LANE_GROUP).transpose(1, 0, 2)


def kernel(x_prompt, x_sample, state_hgrn, w_in_a, lb_logits_a, gnorm_a, w_out_a, w_in_b, lnv_g_b,
           lnv_b_b, w_s_b, b_s_b, w_out_b, ln_g, ln_b):
    bsz, seq, _ = x_prompt.shape
    n_seq, dec_seq, _ = x_sample.shape

    wa, woa = w_in_a[0], w_out_a[0]
    lbl3 = _lane_groups_of_vector(lb_logits_a)
    gn3 = _lane_groups_of_vector(gnorm_a[0:1])
    lng0, lnb0 = ln_g[0:1], ln_b[0:1]

    hp, sp = _hgrn_prompt(x_prompt, wa, lbl3, gn3, woa, lng0, lnb0, layer=0)

    xs = x_sample.reshape(n_seq * dec_seq, D_MODEL)
    decode_gates = _hgrn_sample_gates(xs, wa, lbl3, gn3, layer=0, dec_seq=dec_seq)

    w3b, wo3b = w_in_b[0], w_out_b[0]
    vg3 = _lane_groups_of_vector(lnv_g_b[0:1])
    vb3 = _lane_groups_of_vector(lnv_b_b[0:1])
    lng1, lnb1 = ln_g[1:2], ln_b[1:2]
    ws = w_s_b[0]
    bs = b_s_b[0]

    yp, og, ss = _gmlp(hp.reshape(bsz * seq, D_MODEL), w3b, vg3, vb3, ws, bs, wo3b, lng1, lnb1,
                       emit_v=False, decode=(*decode_gates, state_hgrn[0]), decode_seq=dec_seq)
    yp = yp.reshape(bsz, seq, D_MODEL)
    hs = _out_proj_ln(og, xs, _lane_groups_of_rows(woa), lng0, lnb0)

    reps = CHUNK_B // dec_seq
    ws_dec = jnp.tile(ws[:, :dec_seq, :dec_seq], (1, SUBLANES // dec_seq, reps))
    bs_dec = jnp.tile(bs[:, :dec_seq], (1, reps))
    ys, vn = _gmlp(hs, w3b, vg3, vb3, ws_dec, bs_dec, wo3b, lng1, lnb1, emit_v=True, mix_block=dec_seq)
    ys = ys.reshape(n_seq, dec_seq, D_MODEL)
    vs = vn.reshape(n_seq, dec_seq, E)

    return (yp, ys, sp[None], ss[None], vs[None])
```

```python
import functools
import math

import jax
import jax.numpy as jnp
from jax import lax
from jax.experimental import pallas as pl
from jax.experimental.pallas import tpu as pltpu

F32 = jnp.float32
BF16 = jnp.bfloat16

D_MODEL = 1024
E = 2048
HEAD = 128
N_HEADS = E // HEAD
LANE_GROUP = 256
N_GROUPS = E // LANE_GROUP
HEADS_PER_GROUP = LANE_GROUP // HEAD
CHUNK_A = 64
CHUNK_B = 128
ROW_TILE = 256
HGRN_TILE = 512
HGRN_BLOCK = 2 * CHUNK_A
GMLP_TILE = 512
SUBLANES = 8
DEPTH = 2
ALPHA = (2 * DEPTH) ** 0.25
LN_EPS = 1e-5
VMEM_LIMIT_BYTES = 60000 * 1024


def _dot(a, b):
    return jnp.dot(a, b, preferred_element_type=F32)


STAGE_ROWS = 512
STAGE_SLOTS = 8


def _stage_weight(w_hbm, stage, sems, store):
    n_row_blocks = w_hbm.shape[0] // STAGE_ROWS
    n_col_blocks = w_hbm.shape[1] // LANE_GROUP
    chunks = [(r, c) for c in range(n_col_blocks) for r in range(n_row_blocks)]

    def copy(i):
        r, c = chunks[i]
        slot = i % STAGE_SLOTS
        return pltpu.make_async_copy(
            w_hbm.at[pl.ds(r * STAGE_ROWS, STAGE_ROWS), pl.ds(c * LANE_GROUP, LANE_GROUP)],
            stage.at[slot], sems.at[slot])

    for i in range(min(STAGE_SLOTS, len(chunks))):
        copy(i).start()
    for i, (r, c) in enumerate(chunks):
        copy(i).wait()
        store(r, c, stage[i % STAGE_SLOTS].astype(BF16))
        if i + STAGE_SLOTS < len(chunks):
            copy(i + STAGE_SLOTS).start()


def _stage_projection_weights(w_in_hbm, w_out_hbm, w_scr, wo_scr, stage, sems):
    def store_in(r, c, chunk):
        w_scr[c, r * STAGE_ROWS:(r + 1) * STAGE_ROWS, :] = chunk

    def store_out(r, c, chunk):
        for k in range(STAGE_ROWS // LANE_GROUP):
            wo_scr[r * (STAGE_ROWS // LANE_GROUP) + k, :, c * LANE_GROUP:(c + 1) * LANE_GROUP] = (
                chunk[k * LANE_GROUP:(k + 1) * LANE_GROUP])

    _stage_weight(w_in_hbm, stage, sems, store_in)
    _stage_weight(w_out_hbm, stage, sems, store_out)


def _interleave(stages):
    stages = list(stages)
    while stages:
        for stage in list(stages):
            if next(stage, stages) is stages:
                stages.remove(stage)


def _dot_nt(a, b):
    return lax.dot_general(a, b, (((1,), (1,)), ((), ())), preferred_element_type=F32)


def _dot_tn(a, b):
    return lax.dot_general(a, b, (((0,), (0,)), ((), ())), preferred_element_type=F32)


def _silu(x):
    hx = 0.5 * x
    return hx + hx * jnp.tanh(hx)


def _gelu_tanh(x):
    c = math.sqrt(2.0 / math.pi)
    hx = 0.5 * x
    return hx + hx * jnp.tanh(x * (c + (c * 0.044715) * (x * x)))


def _layer_norm(x, g, b):
    mu = jnp.mean(x, axis=-1, keepdims=True)
    d = x - mu
    var = jnp.mean(d * d, axis=-1, keepdims=True)
    return d * lax.rsqrt(var + LN_EPS) * g + b


def _block_masks(n, block):
    shift = block.bit_length() - 1
    row = lax.broadcasted_iota(jnp.int32, (n, n), 0)
    col = lax.broadcasted_iota(jnp.int32, (n, n), 1)
    same = jnp.right_shift(row, shift) == jnp.right_shift(col, shift)
    causal = jnp.logical_and(same, col <= row)
    return same, causal


def _split_dot(m, x):
    hi = x.astype(BF16)
    lo = (x - hi.astype(F32)).astype(BF16)
    return _dot(jnp.concatenate([m, m], axis=1), jnp.concatenate([hi, lo], axis=0))


def _forget_lower_bound(logits, layer):
    m = jnp.max(logits, axis=0, keepdims=True)
    e = jnp.exp(logits - m)
    den = jnp.sum(e, axis=0, keepdims=True)
    num = jnp.sum(e[: layer + 1], axis=0, keepdims=True)
    return num / den


def _rms_gate(o, gate):
    ms = jnp.mean(o * o, axis=-1, keepdims=True)
    return o * lax.rsqrt(ms + LN_EPS) * gate


def _hgrn_prompt_kernel(x_ref, w_hbm, lbl_ref, gn_ref, wo_hbm, lng_ref, lnb_ref,
                        y_ref, s_ref, st_scr, og_scr, h_scr, w_ref, wo_ref, stage, sems,
                        *, layer, tiles_per_row):
    s = pl.program_id(0)
    n_tiles = pl.num_programs(0) - 1

    @pl.when(s == 0)
    def _():
        _stage_projection_weights(w_hbm, wo_hbm, w_ref, wo_ref, stage, sems)
        h_scr[...] = jnp.zeros_like(h_scr)

    @pl.when(s < n_tiles)
    def _():
        _hgrn_tile_step(lax.rem(s, tiles_per_row), tiles_per_row, x_ref, w_ref, lbl_ref, gn_ref, wo_ref,
                        lng_ref, lnb_ref, y_ref, s_ref, st_scr, og_scr, h_scr, layer=layer)

    @pl.when(s == n_tiles)
    def _():
        y_ref[0] = _layer_norm(h_scr[...], lng_ref[...], lnb_ref[...])


def _hgrn_tile_step(j, tiles_per_row, x_ref, w_ref, lbl_ref, gn_ref, wo_ref, lng_ref, lnb_ref,
                    y_ref, s_ref, st_scr, og_scr, h_scr, *, layer):
    tile = x_ref.shape[1]
    n_blocks = tile // HGRN_BLOCK
    n_chunks = tile // CHUNK_A
    chunks_per_block = HGRN_BLOCK // CHUNK_A

    @pl.when(j == 0)
    def _():
        st_scr[...] = jnp.zeros_like(st_scr)

    y_ref[0] = _layer_norm(h_scr[...], lng_ref[...], lnb_ref[...])

    x = x_ref[0]
    xb = x.astype(BF16)
    _, causal = _block_masks(HGRN_BLOCK, CHUNK_A)
    tri = jnp.where(causal, 1.0, 0.0).astype(BF16)

    def block(p):
        return slice(p * HGRN_BLOCK, (p + 1) * HGRN_BLOCK)

    projected, scored = {}, {}

    def stage_project(hp):
        z = _dot(xb, w_ref[N_GROUPS + hp])
        yield
        qp = _dot(xb, w_ref[hp])
        yield
        v = _dot(xb, w_ref[2 * N_GROUPS + hp])
        yield
        g = _dot(xb, w_ref[3 * N_GROUPS + hp])
        yield
        lb = _forget_lower_bound(lbl_ref[hp], layer)
        c1 = 1.0 - lb
        cs = c1 / (1.0 + jnp.exp(-z))
        k = c1 - cs
        lf = jnp.log(lb + cs)
        cum = jnp.concatenate([_split_dot(tri, lf[block(p)]) for p in range(n_blocks)], axis=0)
        projected[hp] = (_silu(qp), k, cum, v.astype(BF16), _silu(g) * gn_ref[hp])

    def stage_scores(hp):
        q, k, cum, vb, gate = projected.pop(hp)
        qdb = (q * jnp.exp(cum)).astype(BF16)
        ki = k * jnp.exp(-cum)
        dl_rows = jnp.exp(jnp.concatenate(
            [cum[(c + 1) * CHUNK_A - 1:(c + 1) * CHUNK_A] for c in range(n_chunks)]
            + [jnp.zeros((HEAD - n_chunks, LANE_GROUP), F32)], axis=0))
        dl_cols = dl_rows.T
        heads = []
        for i in range(HEADS_PER_GROUP):
            ls = slice(i * HEAD, (i + 1) * HEAD)
            kit = ki[:, ls].T.astype(BF16)
            scores = [_dot(qdb[block(p), ls], kit[:, block(p)]) for p in range(n_blocks)]
            yield
            dls, us = [], []
            for c in range(n_chunks):
                lo, hi = c * CHUNK_A, (c + 1) * CHUNK_A
                ke = (ki[lo:hi, ls] * dl_rows[c:c + 1, ls]).astype(BF16)
                dls.append(dl_cols[ls, c:c + 1])
                us.append(_dot_tn(ke, vb[lo:hi, ls]))
                if c % chunks_per_block == chunks_per_block - 1:
                    yield
            heads.append((qdb[:, ls], vb[:, ls], scores, dls, us))
        scored[hp] = (heads, gate)

    def stage_output(hp):
        heads, gate = scored.pop(hp)
        ogs = []
        for i, (qdb, vb, scores, dls, us) in enumerate(heads):
            st = st_scr[HEADS_PER_GROUP * hp + i]
            os = []
            for p in range(n_blocks):
                o_intra = _dot(jnp.where(causal, scores[p], 0.0).astype(BF16), vb[block(p)])
                for cc in range(chunks_per_block):
                    c = p * chunks_per_block + cc
                    lo, hi = c * CHUNK_A, (c + 1) * CHUNK_A
                    os.append(o_intra[cc * CHUNK_A:(cc + 1) * CHUNK_A] + _dot(qdb[lo:hi], st.astype(BF16)))
                    st = st * dls[c] + us[c]
                yield
            st_scr[HEADS_PER_GROUP * hp + i] = st
            o = jnp.concatenate(os, axis=0)
            ogs.append(_rms_gate(o, gate[:, i * HEAD:(i + 1) * HEAD]))
        og_scr[hp] = jnp.concatenate(ogs, axis=1).astype(BF16)

    y_parts = []

    def out_proj(groups):
        y = _dot(og_scr[groups[0]], wo_ref[groups[0]])
        for hp in groups[1:]:
            yield
            y = y + _dot(og_scr[hp], wo_ref[hp])
        y_parts.append(y)

    for t in range(N_GROUPS + 2):
        stages = []
        if t < N_GROUPS:
            stages.append(stage_project(t))
        if t == N_GROUPS:
            stages.append(out_proj(list(range(0, N_GROUPS - 4))))
        if t == N_GROUPS + 1:
            stages.append(out_proj([N_GROUPS - 4, N_GROUPS - 3]))
        if 0 <= t - 2 < N_GROUPS:
            stages.append(stage_output(t - 2))
        if 0 <= t - 1 < N_GROUPS:
            stages.append(stage_scores(t - 1))
        _interleave(stages)
    _interleave([out_proj([N_GROUPS - 2, N_GROUPS - 1])])
    y = y_parts[0] + y_parts[1] + y_parts[2]
    h_scr[...] = ALPHA * x + y

    @pl.when(j == tiles_per_row - 1)
    def _():
        s_ref[0] = st_scr[...]


def _resident(shape):
    nd = len(shape)
    return pl.BlockSpec(shape, lambda *_: (0,) * nd, pipeline_mode=pl.Buffered(1))


def _weight_scratch(w_in, w_out):
    k, n = w_in.shape
    return [
        pltpu.VMEM((n // LANE_GROUP, k, LANE_GROUP), BF16),
        pltpu.VMEM((w_out.shape[0] // LANE_GROUP, LANE_GROUP, w_out.shape[1]), BF16),
        pltpu.VMEM((STAGE_SLOTS, STAGE_ROWS, LANE_GROUP), F32),
        pltpu.SemaphoreType.DMA((STAGE_SLOTS,)),
    ]


def _hgrn_prompt(x, w_in, lbl3, gn3, w_out, lng, lnb, layer):
    bsz, seq, _ = x.shape
    tiles_per_row = seq // HGRN_TILE
    n_tiles = bsz * tiles_per_row
    hbm = pl.BlockSpec(memory_space=pl.ANY)

    def tile_in(s):
        t = jnp.minimum(s, n_tiles - 1)
        return t // tiles_per_row, t % tiles_per_row

    def tile_out(s):
        t = jnp.maximum(s - 1, 0)
        return t // tiles_per_row, t % tiles_per_row

    return pl.pallas_call(
        functools.partial(_hgrn_prompt_kernel, layer=layer, tiles_per_row=tiles_per_row),
        grid=(n_tiles + 1,),
        in_specs=[
            pl.BlockSpec((1, HGRN_TILE, D_MODEL), lambda s: (*tile_in(s), 0)),
            hbm, _resident(lbl3.shape), _resident(gn3.shape), hbm,
            _resident(lng.shape), _resident(lnb.shape),
        ],
        out_specs=[
            pl.BlockSpec((1, HGRN_TILE, D_MODEL), lambda s: (*tile_out(s), 0)),
            pl.BlockSpec((1, N_HEADS, HEAD, HEAD), lambda s: (tile_in(s)[0], 0, 0, 0)),
        ],
        out_shape=[
            jax.ShapeDtypeStruct(x.shape, F32),
            jax.ShapeDtypeStruct((bsz, N_HEADS, HEAD, HEAD), F32),
        ],
        scratch_shapes=[
            pltpu.VMEM((N_HEADS, HEAD, HEAD), F32),
            pltpu.VMEM((N_GROUPS, HGRN_TILE, LANE_GROUP), BF16),
            pltpu.VMEM((HGRN_TILE, D_MODEL), F32),
            *_weight_scratch(w_in, w_out),
        ],
        compiler_params=pltpu.CompilerParams(
            dimension_semantics=("arbitrary",), vmem_limit_bytes=VMEM_LIMIT_BYTES),
        name="hgrn_prompt",
    )(x, w_in, lbl3, gn3, w_out, lng, lnb)


def _hgrn_sample_gates_kernel(x_ref, wq_ref, wf_ref, wi_ref, wg_ref, lbl_ref, gn_ref,
                              qd_ref, ki_ref, ke_ref, v_ref, gate_ref, dl_ref, *, layer, dec_seq):
    rows = x_ref.shape[0]
    xb = x_ref[...].astype(BF16)
    same, causal = _block_masks(ROW_TILE, dec_seq)
    tri = jnp.where(causal, 1.0, 0.0).astype(BF16)
    blk = jnp.where(same, 1.0, 0.0).astype(BF16)
    lb = _forget_lower_bound(lbl_ref[0], layer)

    qp = _dot(xb, wq_ref[...].astype(BF16))
    q = _silu(qp)
    z = _dot(xb, wf_ref[...].astype(BF16))
    c1 = 1.0 - lb
    cs = c1 / (1.0 + jnp.exp(-z))
    k = c1 - cs
    lf = jnp.log(lb + cs)
    for r in range(rows // ROW_TILE):
        rs = slice(r * ROW_TILE, (r + 1) * ROW_TILE)
        cum = _split_dot(tri, lf[rs])
        cum_last = _split_dot(blk, lf[rs])
        ki = k[rs] * jnp.exp(-cum)
        dl = jnp.exp(cum_last)
        qd_ref[0, rs] = q[rs] * jnp.exp(cum)
        ki_ref[0, rs] = ki
        ke_ref[0, rs] = ki * dl
        dl_ref[0, rs] = dl
    v_ref[0] = _dot(xb, wi_ref[...].astype(BF16))
    gate_ref[0] = _silu(_dot(xb, wg_ref[...].astype(BF16))) * gn_ref[0]


def _hgrn_sample_gates(x_rows, w3, lbl3, gn3, layer, dec_seq):
    rows = x_rows.shape[0]
    assert rows % ROW_TILE == 0 and SUBLANES % dec_seq == 0
    w_spec = lambda part: pl.BlockSpec((D_MODEL, LANE_GROUP), lambda g: (0, part * N_GROUPS + g))
    out_spec = pl.BlockSpec((1, rows, LANE_GROUP), lambda g: (g, 0, 0))
    out_shape = jax.ShapeDtypeStruct((N_GROUPS, rows, LANE_GROUP), F32)
    return pl.pallas_call(
        functools.partial(_hgrn_sample_gates_kernel, layer=layer, dec_seq=dec_seq),
        grid=(N_GROUPS,),
        in_specs=[
            pl.BlockSpec((rows, D_MODEL), lambda g: (0, 0)),
            w_spec(0), w_spec(1), w_spec(2), w_spec(3),
            pl.BlockSpec((1,) + lbl3.shape[1:], lambda g: (g, 0, 0)),
            pl.BlockSpec((1, 1, LANE_GROUP), lambda g: (g, 0, 0)),
        ],
        out_specs=[out_spec] * 6,
        out_shape=[out_shape] * 6,
        compiler_params=pltpu.CompilerParams(
            dimension_semantics=("arbitrary",), vmem_limit_bytes=VMEM_LIMIT_BYTES),
        name="hgrn_sample_gates",
    )(x_rows, w3, w3, w3, w3, lbl3, gn3)


def _decode_state_stages(qd_ref, ki_ref, ke_ref, v_ref, gate_ref, dl_ref, s_ref, og_ref, so_ref, dec_seq):
    seqs_per_tile = SUBLANES // dec_seq
    n_tiles = qd_ref.shape[1] // SUBLANES
    shift = dec_seq.bit_length() - 1
    trow = jnp.right_shift(lax.broadcasted_iota(jnp.int32, (SUBLANES, HEAD), 0), shift)
    in_seq = [trow == s for s in range(seqs_per_tile)]
    pad_rows = jnp.zeros((HEAD - N_HEADS, HEAD), F32)
    tiles = [slice(m * SUBLANES, (m + 1) * SUBLANES) for m in range(n_tiles)]
    dcols, pending = {}, {}
    lane_head = lax.broadcasted_iota(jnp.int32, (SUBLANES, LANE_GROUP), 1) // HEAD
    srow = lax.broadcasted_iota(jnp.int32, (SUBLANES, HEADS_PER_GROUP * SUBLANES), 0)
    scol = jnp.bitwise_and(lax.broadcasted_iota(jnp.int32, (SUBLANES, HEADS_PER_GROUP * SUBLANES), 1),
                           SUBLANES - 1)
    causal = jnp.logical_and(jnp.right_shift(srow, shift) == jnp.right_shift(scol, shift), scol <= srow)

    def per_head_rows(t):
        return jnp.concatenate([jnp.where(lane_head == i, t, 0.0) for i in range(HEADS_PER_GROUP)], axis=0)

    def per_seq_lanes(t):
        return jnp.concatenate([jnp.where(in_seq[s], t, 0.0) for s in range(seqs_per_tile)], axis=1)

    def decay_columns(m):
        dl = [dl_ref[hp, tiles[m], :] for hp in range(N_GROUPS)]
        for s in range(seqs_per_tile):
            r = s * dec_seq
            rows = [dl[h // HEADS_PER_GROUP][r:r + 1, (h % HEADS_PER_GROUP) * HEAD:(h % HEADS_PER_GROUP + 1) * HEAD]
                    for h in range(N_HEADS)]
            dcols[(m, s)] = jnp.concatenate(rows + [pad_rows], axis=0).T

    def start(hp):
        for m in range(n_tiles):
            if hp == 0:
                decay_columns(m)
            rs = tiles[m]
            qd = qd_ref[hp, rs, :]
            ke = ke_ref[hp, rs, :]
            v = v_ref[hp, rs, :]
            qdb = qd.astype(BF16)
            scores = _dot_nt(qdb, per_head_rows(ki_ref[hp, rs, :]).astype(BF16))
            o_inter = []
            for i in range(HEADS_PER_GROUP):
                ls = slice(i * HEAD, (i + 1) * HEAD)
                h = HEADS_PER_GROUP * hp + i
                sts = [s_ref[m * seqs_per_tile + s, h] for s in range(seqs_per_tile)]
                st_rows = jnp.concatenate([st.astype(BF16) for st in sts], axis=0)
                o_inter.append(_dot(per_seq_lanes(qd[:, ls]).astype(BF16), st_rows))
                u = _dot_tn(per_seq_lanes(ke[:, ls]).astype(BF16), v[:, ls].astype(BF16))
                for s in range(seqs_per_tile):
                    so_ref[m * seqs_per_tile + s, h] = (dcols[(m, s)][:, h:h + 1] * sts[s]
                                                       + u[s * HEAD:(s + 1) * HEAD])
            pending[(hp, m)] = (scores, jnp.concatenate(o_inter, axis=1), per_head_rows(v).astype(BF16))
            yield

    def finish(hp):
        for m in range(n_tiles):
            scores, o_inter, v_rows = pending.pop((hp, m))
            o = _dot(jnp.where(causal, scores, 0.0).astype(BF16), v_rows) + o_inter
            gate = gate_ref[hp, tiles[m], :]
            og_ref[hp, tiles[m], :] = jnp.concatenate(
                [_rms_gate(o[:, i * HEAD:(i + 1) * HEAD], gate[:, i * HEAD:(i + 1) * HEAD])
                 for i in range(HEADS_PER_GROUP)], axis=1)

    return start, finish


def _out_proj_ln_kernel(og_ref, x_ref, wo_ref, lng_ref, lnb_ref, y_ref):
    y = _dot(og_ref[0].astype(BF16), wo_ref[0].astype(BF16))
    for hp in range(1, N_GROUPS):
        y = y + _dot(og_ref[hp].astype(BF16), wo_ref[hp].astype(BF16))
    y_ref[...] = _layer_norm(ALPHA * x_ref[...] + y, lng_ref[...], lnb_ref[...])


def _out_proj_ln(og, x, wo3, lng, lnb):
    rows = x.shape[0]
    return pl.pallas_call(
        _out_proj_ln_kernel,
        grid=(rows // ROW_TILE,),
        in_specs=[
            pl.BlockSpec((N_GROUPS, ROW_TILE, LANE_GROUP), lambda r: (0, r, 0)),
            pl.BlockSpec((ROW_TILE, D_MODEL), lambda r: (r, 0)),
            _resident(wo3.shape), _resident(lng.shape), _resident(lnb.shape),
        ],
        out_specs=pl.BlockSpec((ROW_TILE, D_MODEL), lambda r: (r, 0)),
        out_shape=jax.ShapeDtypeStruct(x.shape, F32),
        compiler_params=pltpu.CompilerParams(
            dimension_semantics=("arbitrary",), vmem_limit_bytes=VMEM_LIMIT_BYTES),
        name="out_proj_ln",
    )(og, x, wo3, lng, lnb)


N_GMLP_INPUTS = 9
N_DECODE_INPUTS = 7


def _gmlp_kernel(*refs, emit_v, decode_seq, mix_block):
    refs = list(refs)
    gmlp_in = refs[:N_GMLP_INPUTS]
    del refs[:N_GMLP_INPUTS]
    decode_in = []
    if decode_seq:
        decode_in = refs[:N_DECODE_INPUTS]
        del refs[:N_DECODE_INPUTS]
    y_ref = refs.pop(0)
    vn_ref = refs.pop(0) if emit_v else None
    decode_out = [refs.pop(0), refs.pop(0)] if decode_seq else []
    v_scr, og_scr, h_scr, w_ref, wo_ref, sems = refs
    x_ref, w_hbm, vg_ref, vb_ref, ws_ref, bs_ref, wo_hbm, lng_ref, lnb_ref = gmlp_in
    s = pl.program_id(0)
    n_tiles = pl.num_programs(0) - 1

    @pl.when(s == 0)
    def _():
        _stage_projection_weights(w_hbm, wo_hbm, w_ref, wo_ref, v_scr, sems)
        h_scr[...] = jnp.zeros_like(h_scr)

    @pl.when(s < n_tiles)
    def _():
        decode = _decode_state_stages(*decode_in, *decode_out, decode_seq) if decode_seq else None
        _gmlp_tile_step(x_ref, w_ref, vg_ref, vb_ref, ws_ref, bs_ref, wo_ref, lng_ref, lnb_ref,
                        y_ref, vn_ref, v_scr, og_scr, h_scr, decode, mix_block)

    @pl.when(s == n_tiles)
    def _():
        y_ref[...] = _layer_norm(h_scr[...], lng_ref[...], lnb_ref[...])


def _gmlp_tile_step(x_ref, w_ref, vg_ref, vb_ref, ws_ref, bs_ref, wo_ref, lng_ref, lnb_ref,
                    y_ref, vn_ref, v_scr, og_scr, h_scr, decode, mix_block):
    y_ref[...] = _layer_norm(h_scr[...], lng_ref[...], lnb_ref[...])

    n_chunks = x_ref.shape[0] // CHUNK_B
    x = x_ref[...]
    xb = x.astype(BF16)

    half = x_ref.shape[0] // 2

    def stage_v(gp):
        for r in range(2):
            rows = slice(r * half, (r + 1) * half)
            v_scr[gp, rows] = _gelu_tanh(_dot(xb[rows], w_ref[N_GROUPS + gp]))
            yield

    for gp in range(N_GROUPS):
        _interleave([stage_v(gp)] + ([decode[0](gp)] if decode is not None else []))

    s1 = jnp.sum(v_scr[0], axis=-1, keepdims=True)
    for gp in range(1, N_GROUPS):
        s1 = s1 + jnp.sum(v_scr[gp], axis=-1, keepdims=True)
    mu = s1 * (1.0 / E)
    s2 = jnp.zeros_like(mu)
    for gp in range(N_GROUPS):
        d = v_scr[gp] - mu
        s2 = s2 + jnp.sum(d * d, axis=-1, keepdims=True)
    rstd = lax.rsqrt(s2 * (1.0 / E) + LN_EPS)

    _, causal = _block_masks(CHUNK_B, mix_block)
    ws_reps = CHUNK_B // ws_ref.shape[1]
    n_mix_groups = bs_ref.shape[0]
    bias_cols = jnp.concatenate(
        [bs_ref[...], jnp.zeros((CHUNK_B - n_mix_groups, CHUNK_B), F32)], axis=0).T

    def stage_gate(gp):
        return _dot(xb, w_ref[gp]), _dot(xb, w_ref[2 * N_GROUPS + gp])

    def stage_mix(gp, gate):
        u_pre, z = gate
        vn = (v_scr[gp] - mu) * rstd * vg_ref[gp] + vb_ref[gp]
        if vn_ref is not None:
            vn_ref[:, gp * LANE_GROUP:(gp + 1) * LANE_GROUP] = vn
        vnb = vn.astype(BF16)
        cols = []
        for i in range(HEADS_PER_GROUP):
            ls = slice(i * HEAD, (i + 1) * HEAD)
            g = HEADS_PER_GROUP * gp + i
            wg = jnp.concatenate([ws_ref[g]] * ws_reps, axis=0) if ws_reps > 1 else ws_ref[g]
            wc = jnp.where(causal, wg, 0.0).astype(BF16)
            bias = bias_cols[:, g:g + 1]
            side_by_side = jnp.concatenate(
                [vnb[c * CHUNK_B:(c + 1) * CHUNK_B, ls] for c in range(n_chunks)], axis=1)
            m = _dot(wc, side_by_side) + bias
            cols.append(jnp.concatenate([m[:, c * HEAD:(c + 1) * HEAD] for c in range(n_chunks)], axis=0))
        mixed = jnp.concatenate(cols, axis=1)
        og_scr[gp] = (_gelu_tanh(u_pre) * mixed * _silu(z)).astype(BF16)

    def out_proj(groups):
        y = _dot(og_scr[groups[0]], wo_ref[groups[0]])
        for gp in groups[1:]:
            y = y + _dot(og_scr[gp], wo_ref[gp])
        return y

    y_parts = []
    gates = {0: stage_gate(0)}
    for gp in range(N_GROUPS):
        if gp + 1 < N_GROUPS:
            gates[gp + 1] = stage_gate(gp + 1)
        stage_mix(gp, gates.pop(gp))
        if decode is not None:
            decode[1](gp)
        if gp == N_GROUPS - 2:
            y_parts.append(out_proj(list(range(0, N_GROUPS - 3))))
    y_parts.append(out_proj([N_GROUPS - 3, N_GROUPS - 2]))
    y_parts.append(out_proj([N_GROUPS - 1]))
    y = y_parts[0] + y_parts[1] + y_parts[2]
    h_scr[...] = ALPHA * x + y


def _gmlp(x, w_in, vg3, vb3, ws, bs_rows, w_out, lng, lnb, emit_v, mix_block=CHUNK_B, decode=None,
          decode_seq=0):
    rows = x.shape[0]
    tile = min(GMLP_TILE, rows)
    assert tile == STAGE_ROWS and N_GROUPS == STAGE_SLOTS
    n_tiles = rows // tile
    tile_in = lambda s: jnp.minimum(s, n_tiles - 1)
    tile_out = lambda s: jnp.maximum(s - 1, 0)
    hbm = pl.BlockSpec(memory_space=pl.ANY)
    in_specs = [
        pl.BlockSpec((tile, D_MODEL), lambda s: (tile_in(s), 0)),
        hbm, _resident(vg3.shape), _resident(vb3.shape), _resident(ws.shape),
        _resident(bs_rows.shape), hbm, _resident(lng.shape), _resident(lnb.shape),
    ]
    operands = [x, w_in, vg3, vb3, ws, bs_rows, w_out, lng, lnb]
    w_scr, wo_scr, _, sems = _weight_scratch(w_in, w_out)
    assert len(operands) == N_GMLP_INPUTS
    out_specs = [pl.BlockSpec((tile, D_MODEL), lambda s: (tile_out(s), 0))]
    out_shape = [jax.ShapeDtypeStruct(x.shape, F32)]
    if emit_v:
        out_specs.append(pl.BlockSpec((tile, E), lambda s: (tile_in(s), 0)))
        out_shape.append(jax.ShapeDtypeStruct((rows, E), F32))
    if decode is not None:
        assert len(decode) == N_DECODE_INPUTS
        state = decode[-1]
        n_seq = state.shape[0]
        seqs_per_step = n_seq // n_tiles
        rows_per_step = seqs_per_step * decode_seq
        assert n_seq % n_tiles == 0 and rows_per_step % SUBLANES == 0 and SUBLANES % decode_seq == 0
        tok_spec = pl.BlockSpec((N_GROUPS, rows_per_step, LANE_GROUP), lambda s: (0, tile_in(s), 0))
        st_spec = pl.BlockSpec((seqs_per_step, N_HEADS, HEAD, HEAD), lambda s: (tile_in(s), 0, 0, 0))
        in_specs += [tok_spec] * (N_DECODE_INPUTS - 1) + [st_spec]
        operands += list(decode)
        out_specs += [tok_spec, st_spec]
        out_shape += [jax.ShapeDtypeStruct(decode[0].shape, F32), jax.ShapeDtypeStruct(state.shape, F32)]
    return pl.pallas_call(
        functools.partial(_gmlp_kernel, emit_v=emit_v, decode_seq=decode_seq if decode is not None else 0,
                          mix_block=mix_block),
        grid=(n_tiles + 1,),
        in_specs=in_specs,
        out_specs=out_specs,
        out_shape=out_shape,
        scratch_shapes=[
            pltpu.VMEM((N_GROUPS, tile, LANE_GROUP), F32),
            pltpu.VMEM((N_GROUPS, tile, LANE_GROUP), BF16),
            pltpu.VMEM((tile, D_MODEL), F32),
            w_scr, wo_scr, sems,
        ],
        compiler_params=pltpu.CompilerParams(
            dimension_semantics=("arbitrary",), vmem_limit_bytes=VMEM_LIMIT_BYTES),
        name="gmlp_emit_v" if emit_v else "gmlp",
    )(*operands)


def _lane_groups_of_rows(w):
    return w.reshape(w.shape[0] // LANE_GROUP, LANE_GROUP, w.shape[1])


def _lane_groups_of_vector(v):
    return v.reshape(v.shape[0], N_GROUPS, LANE_GROUP).transpose(1, 0, 2)


def kernel(x_prompt, x_sample, state_hgrn, w_in_a, lb_logits_a, gnorm_a, w_out_a, w_in_b, lnv_g_b,
           lnv_b_b, w_s_b, b_s_b, w_out_b, ln_g, ln_b):
    bsz, seq, _ = x_prompt.shape
    n_seq, dec_seq, _ = x_sample.shape

    wa, woa = w_in_a[0], w_out_a[0]
    lbl3 = _lane_groups_of_vector(lb_logits_a)
    gn3 = _lane_groups_of_vector(gnorm_a[0:1])
    lng0, lnb0 = ln_g[0:1], ln_b[0:1]

    hp, sp = _hgrn_prompt(x_prompt, wa, lbl3, gn3, woa, lng0, lnb0, layer=0)

    xs = x_sample.reshape(n_seq * dec_seq, D_MODEL)
    decode_gates = _hgrn_sample_gates(xs, wa, lbl3, gn3, layer=0, dec_seq=dec_seq)

    w3b, wo3b = w_in_b[0], w_out_b[0]
    vg3 = _lane_groups_of_vector(lnv_g_b[0:1])
    vb3 = _lane_groups_of_vector(lnv_b_b[0:1])
    lng1, lnb1 = ln_g[1:2], ln_b[1:2]
    ws = w_s_b[0]
    bs = b_s_b[0]

    yp, og, ss = _gmlp(hp.reshape(bsz * seq, D_MODEL), w3b, vg3, vb3, ws, bs, wo3b, lng1, lnb1,
                       emit_v=False, decode=(*decode_gates, state_hgrn[0]), decode_seq=dec_seq)
    yp = yp.reshape(bsz, seq, D_MODEL)
    hs = _out_proj_ln(og, xs, _lane_groups_of_rows(woa), lng0, lnb0)

    reps = CHUNK_B // dec_seq
    ws_dec = jnp.tile(ws[:, :dec_seq, :dec_seq], (1, SUBLANES // dec_seq, reps))
    bs_dec = jnp.tile(bs[:, :dec_seq], (1, reps))
    ys, vn = _gmlp(hs, w3b, vg3, vb3, ws_dec, bs_dec, wo3b, lng1, lnb1, emit_v=True, mix_block=dec_seq)
    ys = ys.reshape(n_seq, dec_seq, D_MODEL)
    vs = vn.reshape(n_seq, dec_seq, E)

    return (yp, ys, sp[None], ss[None], vs[None])
```

```python
import functools
import math

import jax
import jax.numpy as jnp
from jax import lax
from jax.experimental import pallas as pl
from jax.experimental.pallas import tpu as pltpu

F32 = jnp.float32
BF16 = jnp.bfloat16

D_MODEL = 1024
E = 2048
HEAD = 128
N_HEADS = E // HEAD
LANE_GROUP = 256
N_GROUPS = E // LANE_GROUP
HEADS_PER_GROUP = LANE_GROUP // HEAD
CHUNK_A = 64
CHUNK_B = 128
ROW_TILE = 256
HGRN_TILE = 512
HGRN_BLOCK = 2 * CHUNK_A
GMLP_TILE = 512
SUBLANES = 8
DEPTH = 2
ALPHA = (2 * DEPTH) ** 0.25
LN_EPS = 1e-5
VMEM_LIMIT_BYTES = 60000 * 1024


def _dot(a, b):
    return jnp.dot(a, b, preferred_element_type=F32)


STAGE_ROWS = 512
STAGE_SLOTS = 8


def _stage_weight(w_hbm, stage, sems, store):
    n_row_blocks = w_hbm.shape[0] // STAGE_ROWS
    n_col_blocks = w_hbm.shape[1] // LANE_GROUP
    chunks = [(r, c) for c in range(n_col_blocks) for r in range(n_row_blocks)]

    def copy(i):
        r, c = chunks[i]
        slot = i % STAGE_SLOTS
        return pltpu.make_async_copy(
            w_hbm.at[pl.ds(r * STAGE_ROWS, STAGE_ROWS), pl.ds(c * LANE_GROUP, LANE_GROUP)],
            stage.at[slot], sems.at[slot])

    for i in range(min(STAGE_SLOTS, len(chunks))):
        copy(i).start()
    for i, (r, c) in enumerate(chunks):
        copy(i).wait()
        store(r, c, stage[i % STAGE_SLOTS].astype(BF16))
        if i + STAGE_SLOTS < len(chunks):
            copy(i + STAGE_SLOTS).start()


def _stage_projection_weights(w_in_hbm, w_out_hbm, w_scr, wo_scr, stage, sems):
    def store_in(r, c, chunk):
        w_scr[c, r * STAGE_ROWS:(r + 1) * STAGE_ROWS, :] = chunk

    def store_out(r, c, chunk):
        for k in range(STAGE_ROWS // LANE_GROUP):
            wo_scr[r * (STAGE_ROWS // LANE_GROUP) + k, :, c * LANE_GROUP:(c + 1) * LANE_GROUP] = (
                chunk[k * LANE_GROUP:(k + 1) * LANE_GROUP])

    _stage_weight(w_in_hbm, stage, sems, store_in)
    _stage_weight(w_out_hbm, stage, sems, store_out)


def _interleave(stages):
    stages = list(stages)
    while stages:
        for stage in list(stages):
            if next(stage, stages) is stages:
                stages.remove(stage)


def _dot_nt(a, b):
    return lax.dot_general(a, b, (((1,), (1,)), ((), ())), preferred_element_type=F32)


def _dot_tn(a, b):
    return lax.dot_general(a, b, (((0,), (0,)), ((), ())), preferred_element_type=F32)


def _silu(x):
    hx = 0.5 * x
    return hx + hx * jnp.tanh(hx)


def _recip(d):
    r = pl.reciprocal(d, approx=True)
    return r * (2.0 - d * r)


def _gelu_tanh(x):
    c = math.sqrt(2.0 / math.pi)
    hx = 0.5 * x
    return hx + hx * jnp.tanh(x * (c + (c * 0.044715) * (x * x)))


def _layer_norm(x, g, b):
    mu = jnp.mean(x, axis=-1, keepdims=True)
    d = x - mu
    var = jnp.mean(d * d, axis=-1, keepdims=True)
    return d * lax.rsqrt(var + LN_EPS) * g + b


def _block_masks(n, block):
    shift = block.bit_length() - 1
    row = lax.broadcasted_iota(jnp.int32, (n, n), 0)
    col = lax.broadcasted_iota(jnp.int32, (n, n), 1)
    same = jnp.right_shift(row, shift) == jnp.right_shift(col, shift)
    causal = jnp.logical_and(same, col <= row)
    return same, causal


def _split_dot(m, x):
    hi = x.astype(BF16)
    lo = (x - hi.astype(F32)).astype(BF16)
    return _dot(jnp.concatenate([m, m], axis=1), jnp.concatenate([hi, lo], axis=0))


def _forget_lower_bound(logits, layer):
    m = jnp.max(logits, axis=0, keepdims=True)
    e = jnp.exp(logits - m)
    den = jnp.sum(e, axis=0, keepdims=True)
    num = jnp.sum(e[: layer + 1], axis=0, keepdims=True)
    return num / den


def _rms_gate(o, gate):
    ms = jnp.mean(o * o, axis=-1, keepdims=True)
    return o * lax.rsqrt(ms + LN_EPS) * gate


def _hgrn_prompt_kernel(x_ref, w_hbm, lbl_ref, gn_ref, wo_hbm, lng_ref, lnb_ref,
                        y_ref, s_ref, st_scr, og_scr, h_scr, w_ref, wo_ref, stage, sems,
                        *, layer, tiles_per_row):
    s = pl.program_id(0)
    n_tiles = pl.num_programs(0) - 1

    @pl.when(s == 0)
    def _():
        _stage_projection_weights(w_hbm, wo_hbm, w_ref, wo_ref, stage, sems)
        h_scr[...] = jnp.zeros_like(h_scr)

    @pl.when(s < n_tiles)
    def _():
        _hgrn_tile_step(lax.rem(s, tiles_per_row), tiles_per_row, x_ref, w_ref, lbl_ref, gn_ref, wo_ref,
                        lng_ref, lnb_ref, y_ref, s_ref, st_scr, og_scr, h_scr, layer=layer)

    @pl.when(s == n_tiles)
    def _():
        y_ref[0] = _layer_norm(h_scr[...], lng_ref[...], lnb_ref[...])


def _hgrn_tile_step(j, tiles_per_row, x_ref, w_ref, lbl_ref, gn_ref, wo_ref, lng_ref, lnb_ref,
                    y_ref, s_ref, st_scr, og_scr, h_scr, *, layer):
    tile = x_ref.shape[1]
    n_blocks = tile // HGRN_BLOCK
    n_chunks = tile // CHUNK_A
    chunks_per_block = HGRN_BLOCK // CHUNK_A

    @pl.when(j == 0)
    def _():
        st_scr[...] = jnp.zeros_like(st_scr)

    y_ref[0] = _layer_norm(h_scr[...], lng_ref[...], lnb_ref[...])

    x = x_ref[0]
    xb = x.astype(BF16)
    _, causal = _block_masks(HGRN_BLOCK, CHUNK_A)
    tri = jnp.where(causal, 1.0, 0.0).astype(BF16)

    def block(p):
        return slice(p * HGRN_BLOCK, (p + 1) * HGRN_BLOCK)

    projected, scored = {}, {}

    def stage_project(hp):
        z = _dot(xb, w_ref[N_GROUPS + hp])
        yield
        qp = _dot(xb, w_ref[hp])
        yield
        v = _dot(xb, w_ref[2 * N_GROUPS + hp])
        yield
        g = _dot(xb, w_ref[3 * N_GROUPS + hp])
        yield
        lb = _forget_lower_bound(lbl_ref[hp], layer)
        c1 = 1.0 - lb
        cs = c1 * _recip(1.0 + jnp.exp(-z))
        k = c1 - cs
        lf = jnp.log(lb + cs)
        cum = jnp.concatenate([_split_dot(tri, lf[block(p)]) for p in range(n_blocks)], axis=0)
        projected[hp] = (_silu(qp), k, cum, v.astype(BF16), _silu(g) * gn_ref[hp])

    def stage_scores(hp):
        q, k, cum, vb, gate = projected.pop(hp)
        qdb = (q * jnp.exp(cum)).astype(BF16)
        ki = k * jnp.exp(-cum)
        dl_rows = jnp.exp(jnp.concatenate(
            [cum[(c + 1) * CHUNK_A - 1:(c + 1) * CHUNK_A] for c in range(n_chunks)]
            + [jnp.zeros((HEAD - n_chunks, LANE_GROUP), F32)], axis=0))
        dl_cols = dl_rows.T
        heads = []
        for i in range(HEADS_PER_GROUP):
            ls = slice(i * HEAD, (i + 1) * HEAD)
            kit = ki[:, ls].T.astype(BF16)
            scores = [_dot(qdb[block(p), ls], kit[:, block(p)]) for p in range(n_blocks)]
            yield
            dls, us = [], []
            for c in range(n_chunks):
                lo, hi = c * CHUNK_A, (c + 1) * CHUNK_A
                ke = (ki[lo:hi, ls] * dl_rows[c:c + 1, ls]).astype(BF16)
                dls.append(dl_cols[ls, c:c + 1])
                us.append(_dot_tn(ke, vb[lo:hi, ls]))
                if c % chunks_per_block == chunks_per_block - 1:
                    yield
            heads.append((qdb[:, ls], vb[:, ls], scores, dls, us))
        scored[hp] = (heads, gate)

    def stage_output(hp):
        heads, gate = scored.pop(hp)
        ogs = []
        for i, (qdb, vb, scores, dls, us) in enumerate(heads):
            st = st_scr[HEADS_PER_GROUP * hp + i]
            os = []
            for p in range(n_blocks):
                o_intra = _dot(jnp.where(causal, scores[p], 0.0).astype(BF16), vb[block(p)])
                for cc in range(chunks_per_block):
                    c = p * chunks_per_block + cc
                    lo, hi = c * CHUNK_A, (c + 1) * CHUNK_A
                    os.append(o_intra[cc * CHUNK_A:(cc + 1) * CHUNK_A] + _dot(qdb[lo:hi], st.astype(BF16)))
                    st = st * dls[c] + us[c]
                yield
            st_scr[HEADS_PER_GROUP * hp + i] = st
            o = jnp.concatenate(os, axis=0)
            ogs.append(_rms_gate(o, gate[:, i * HEAD:(i + 1) * HEAD]))
        og_scr[hp] = jnp.concatenate(ogs, axis=1).astype(BF16)

    y_parts = []

    def out_proj(groups):
        y = _dot(og_scr[groups[0]], wo_ref[groups[0]])
        for hp in groups[1:]:
            yield
            y = y + _dot(og_scr[hp], wo_ref[hp])
        y_parts.append(y)

    for t in range(N_GROUPS + 2):
        stages = []
        if t < N_GROUPS:
            stages.append(stage_project(t))
        if t == N_GROUPS:
            stages.append(out_proj(list(range(0, N_GROUPS - 4))))
        if t == N_GROUPS + 1:
            stages.append(out_proj([N_GROUPS - 4, N_GROUPS - 3]))
        if 0 <= t - 2 < N_GROUPS:
            stages.append(stage_output(t - 2))
        if 0 <= t - 1 < N_GROUPS:
            stages.append(stage_scores(t - 1))
        _interleave(stages)
    _interleave([out_proj([N_GROUPS - 2, N_GROUPS - 1])])
    y = y_parts[0] + y_parts[1] + y_parts[2]
    h_scr[...] = ALPHA * x + y

    @pl.when(j == tiles_per_row - 1)
    def _():
        s_ref[0] = st_scr[...]


def _resident(shape):
    nd = len(shape)
    return pl.BlockSpec(shape, lambda *_: (0,) * nd, pipeline_mode=pl.Buffered(1))


def _weight_scratch(w_in, w_out):
    k, n = w_in.shape
    return [
        pltpu.VMEM((n // LANE_GROUP, k, LANE_GROUP), BF16),
        pltpu.VMEM((w_out.shape[0] // LANE_GROUP, LANE_GROUP, w_out.shape[1]), BF16),
        pltpu.VMEM((STAGE_SLOTS, STAGE_ROWS, LANE_GROUP), F32),
        pltpu.SemaphoreType.DMA((STAGE_SLOTS,)),
    ]


def _hgrn_prompt(x, w_in, lbl3, gn3, w_out, lng, lnb, layer):
    bsz, seq, _ = x.shape
    tiles_per_row = seq // HGRN_TILE
    n_tiles = bsz * tiles_per_row
    hbm = pl.BlockSpec(memory_space=pl.ANY)

    def tile_in(s):
        t = jnp.minimum(s, n_tiles - 1)
        return t // tiles_per_row, t % tiles_per_row

    def tile_out(s):
        t = jnp.maximum(s - 1, 0)
        return t // tiles_per_row, t % tiles_per_row

    return pl.pallas_call(
        functools.partial(_hgrn_prompt_kernel, layer=layer, tiles_per_row=tiles_per_row),
        grid=(n_tiles + 1,),
        in_specs=[
            pl.BlockSpec((1, HGRN_TILE, D_MODEL), lambda s: (*tile_in(s), 0)),
            hbm, _resident(lbl3.shape), _resident(gn3.shape), hbm,
            _resident(lng.shape), _resident(lnb.shape),
        ],
        out_specs=[
            pl.BlockSpec((1, HGRN_TILE, D_MODEL), lambda s: (*tile_out(s), 0)),
            pl.BlockSpec((1, N_HEADS, HEAD, HEAD), lambda s: (tile_in(s)[0], 0, 0, 0)),
        ],
        out_shape=[
            jax.ShapeDtypeStruct(x.shape, F32),
            jax.ShapeDtypeStruct((bsz, N_HEADS, HEAD, HEAD), F32),
        ],
        scratch_shapes=[
            pltpu.VMEM((N_HEADS, HEAD, HEAD), F32),
            pltpu.VMEM((N_GROUPS, HGRN_TILE, LANE_GROUP), BF16),
            pltpu.VMEM((HGRN_TILE, D_MODEL), F32),
            *_weight_scratch(w_in, w_out),
        ],
        compiler_params=pltpu.CompilerParams(
            dimension_semantics=("arbitrary",), vmem_limit_bytes=VMEM_LIMIT_BYTES),
        name="hgrn_prompt",
    )(x, w_in, lbl3, gn3, w_out, lng, lnb)


def _hgrn_sample_gates_kernel(x_ref, wq_ref, wf_ref, wi_ref, wg_ref, lbl_ref, gn_ref,
                              qd_ref, ki_ref, ke_ref, v_ref, gate_ref, dl_ref, *, layer, dec_seq):
    rows = x_ref.shape[0]
    xb = x_ref[...].astype(BF16)
    same, causal = _block_masks(ROW_TILE, dec_seq)
    tri = jnp.where(causal, 1.0, 0.0).astype(BF16)
    blk = jnp.where(same, 1.0, 0.0).astype(BF16)
    lb = _forget_lower_bound(lbl_ref[0], layer)

    qp = _dot(xb, wq_ref[...].astype(BF16))
    q = _silu(qp)
    z = _dot(xb, wf_ref[...].astype(BF16))
    c1 = 1.0 - lb
    cs = c1 * _recip(1.0 + jnp.exp(-z))
    k = c1 - cs
    lf = jnp.log(lb + cs)
    for r in range(rows // ROW_TILE):
        rs = slice(r * ROW_TILE, (r + 1) * ROW_TILE)
        cum = _split_dot(tri, lf[rs])
        cum_last = _split_dot(blk, lf[rs])
        ki = k[rs] * jnp.exp(-cum)
        dl = jnp.exp(cum_last)
        qd_ref[0, rs] = q[rs] * jnp.exp(cum)
        ki_ref[0, rs] = ki
        ke_ref[0, rs] = ki * dl
        dl_ref[0, rs] = dl
    v_ref[0] = _dot(xb, wi_ref[...].astype(BF16))
    gate_ref[0] = _silu(_dot(xb, wg_ref[...].astype(BF16))) * gn_ref[0]


def _hgrn_sample_gates(x_rows, w3, lbl3, gn3, layer, dec_seq):
    rows = x_rows.shape[0]
    assert rows % ROW_TILE == 0 and SUBLANES % dec_seq == 0
    w_spec = lambda part: pl.BlockSpec((D_MODEL, LANE_GROUP), lambda g: (0, part * N_GROUPS + g))
    out_spec = pl.BlockSpec((1, rows, LANE_GROUP), lambda g: (g, 0, 0))
    out_shape = jax.ShapeDtypeStruct((N_GROUPS, rows, LANE_GROUP), F32)
    return pl.pallas_call(
        functools.partial(_hgrn_sample_gates_kernel, layer=layer, dec_seq=dec_seq),
        grid=(N_GROUPS,),
        in_specs=[
            pl.BlockSpec((rows, D_MODEL), lambda g: (0, 0)),
            w_spec(0), w_spec(1), w_spec(2), w_spec(3),
            pl.BlockSpec((1,) + lbl3.shape[1:], lambda g: (g, 0, 0)),
            pl.BlockSpec((1, 1, LANE_GROUP), lambda g: (g, 0, 0)),
        ],
        out_specs=[out_spec] * 6,
        out_shape=[out_shape] * 6,
        compiler_params=pltpu.CompilerParams(
            dimension_semantics=("arbitrary",), vmem_limit_bytes=VMEM_LIMIT_BYTES),
        name="hgrn_sample_gates",
    )(x_rows, w3, w3, w3, w3, lbl3, gn3)


def _decode_state_stages(qd_ref, ki_ref, ke_ref, v_ref, gate_ref, dl_ref, s_ref, og_ref, so_ref, dec_seq):
    seqs_per_tile = SUBLANES // dec_seq
    n_tiles = qd_ref.shape[1] // SUBLANES
    shift = dec_seq.bit_length() - 1
    trow = jnp.right_shift(lax.broadcasted_iota(jnp.int32, (SUBLANES, HEAD), 0), shift)
    in_seq = [trow == s for s in range(seqs_per_tile)]
    pad_rows = jnp.zeros((HEAD - N_HEADS, HEAD), F32)
    tiles = [slice(m * SUBLANES, (m + 1) * SUBLANES) for m in range(n_tiles)]
    dcols, pending = {}, {}
    lane_head = lax.broadcasted_iota(jnp.int32, (SUBLANES, LANE_GROUP), 1) // HEAD
    srow = lax.broadcasted_iota(jnp.int32, (SUBLANES, HEADS_PER_GROUP * SUBLANES), 0)
    scol = jnp.bitwise_and(lax.broadcasted_iota(jnp.int32, (SUBLANES, HEADS_PER_GROUP * SUBLANES), 1),
                           SUBLANES - 1)
    causal = jnp.logical_and(jnp.right_shift(srow, shift) == jnp.right_shift(scol, shift), scol <= srow)

    def per_head_rows(t):
        return jnp.concatenate([jnp.where(lane_head == i, t, 0.0) for i in range(HEADS_PER_GROUP)], axis=0)

    def per_seq_lanes(t):
        return jnp.concatenate([jnp.where(in_seq[s], t, 0.0) for s in range(seqs_per_tile)], axis=1)

    def decay_columns(m):
        dl = [dl_ref[hp, tiles[m], :] for hp in range(N_GROUPS)]
        for s in range(seqs_per_tile):
            r = s * dec_seq
            rows = [dl[h // HEADS_PER_GROUP][r:r + 1, (h % HEADS_PER_GROUP) * HEAD:(h % HEADS_PER_GROUP + 1) * HEAD]
                    for h in range(N_HEADS)]
            dcols[(m, s)] = jnp.concatenate(rows + [pad_rows], axis=0).T

    def start(hp):
        for m in range(n_tiles):
            if hp == 0:
                decay_columns(m)
            rs = tiles[m]
            qd = qd_ref[hp, rs, :]
            ke = ke_ref[hp, rs, :]
            v = v_ref[hp, rs, :]
            qdb = qd.astype(BF16)
            scores = _dot_nt(qdb, per_head_rows(ki_ref[hp, rs, :]).astype(BF16))
            o_inter = []
            for i in range(HEADS_PER_GROUP):
                ls = slice(i * HEAD, (i + 1) * HEAD)
                h = HEADS_PER_GROUP * hp + i
                sts = [s_ref[m * seqs_per_tile + s, h] for s in range(seqs_per_tile)]
                st_rows = jnp.concatenate([st.astype(BF16) for st in sts], axis=0)
                o_inter.append(_dot(per_seq_lanes(qd[:, ls]).astype(BF16), st_rows))
                u = _dot_tn(per_seq_lanes(ke[:, ls]).astype(BF16), v[:, ls].astype(BF16))
                for s in range(seqs_per_tile):
                    so_ref[m * seqs_per_tile + s, h] = (dcols[(m, s)][:, h:h + 1] * sts[s]
                                                       + u[s * HEAD:(s + 1) * HEAD])
            pending[(hp, m)] = (scores, jnp.concatenate(o_inter, axis=1), per_head_rows(v).astype(BF16))
            yield

    def finish(hp):
        for m in range(n_tiles):
            scores, o_inter, v_rows = pending.pop((hp, m))
            o = _dot(jnp.where(causal, scores, 0.0).astype(BF16), v_rows) + o_inter
            gate = gate_ref[hp, tiles[m], :]
            og_ref[hp, tiles[m], :] = jnp.concatenate(
                [_rms_gate(o[:, i * HEAD:(i + 1) * HEAD], gate[:, i * HEAD:(i + 1) * HEAD])
                 for i in range(HEADS_PER_GROUP)], axis=1)

    return start, finish


def _out_proj_ln_kernel(og_ref, x_ref, wo_ref, lng_ref, lnb_ref, y_ref):
    y = _dot(og_ref[0].astype(BF16), wo_ref[0].astype(BF16))
    for hp in range(1, N_GROUPS):
        y = y + _dot(og_ref[hp].astype(BF16), wo_ref[hp].astype(BF16))
    y_ref[...] = _layer_norm(ALPHA * x_ref[...] + y, lng_ref[...], lnb_ref[...])


def _out_proj_ln(og, x, wo3, lng, lnb):
    rows = x.shape[0]
    return pl.pallas_call(
        _out_proj_ln_kernel,
        grid=(rows // ROW_TILE,),
        in_specs=[
            pl.BlockSpec((N_GROUPS, ROW_TILE, LANE_GROUP), lambda r: (0, r, 0)),
            pl.BlockSpec((ROW_TILE, D_MODEL), lambda r: (r, 0)),
            _resident(wo3.shape), _resident(lng.shape), _resident(lnb.shape),
        ],
        out_specs=pl.BlockSpec((ROW_TILE, D_MODEL), lambda r: (r, 0)),
        out_shape=jax.ShapeDtypeStruct(x.shape, F32),
        compiler_params=pltpu.CompilerParams(
            dimension_semantics=("arbitrary",), vmem_limit_bytes=VMEM_LIMIT_BYTES),
        name="out_proj_ln",
    )(og, x, wo3, lng, lnb)


N_GMLP_INPUTS = 9
N_DECODE_INPUTS = 7


def _gmlp_kernel(*refs, emit_v, decode_seq, mix_block):
    refs = list(refs)
    gmlp_in = refs[:N_GMLP_INPUTS]
    del refs[:N_GMLP_INPUTS]
    decode_in = []
    if decode_seq:
        decode_in = refs[:N_DECODE_INPUTS]
        del refs[:N_DECODE_INPUTS]
    y_ref = refs.pop(0)
    vn_ref = refs.pop(0) if emit_v else None
    decode_out = [refs.pop(0), refs.pop(0)] if decode_seq else []
    v_scr, og_scr, h_scr, w_ref, wo_ref, sems = refs
    x_ref, w_hbm, vg_ref, vb_ref, ws_ref, bs_ref, wo_hbm, lng_ref, lnb_ref = gmlp_in
    s = pl.program_id(0)
    n_tiles = pl.num_programs(0) - 1

    @pl.when(s == 0)
    def _():
        _stage_projection_weights(w_hbm, wo_hbm, w_ref, wo_ref, v_scr, sems)
        h_scr[...] = jnp.zeros_like(h_scr)

    @pl.when(s < n_tiles)
    def _():
        decode = _decode_state_stages(*decode_in, *decode_out, decode_seq) if decode_seq else None
        _gmlp_tile_step(x_ref, w_ref, vg_ref, vb_ref, ws_ref, bs_ref, wo_ref, lng_ref, lnb_ref,
                        y_ref, vn_ref, v_scr, og_scr, h_scr, decode, mix_block)

    @pl.when(s == n_tiles)
    def _():
        y_ref[...] = _layer_norm(h_scr[...], lng_ref[...], lnb_ref[...])


def _gmlp_tile_step(x_ref, w_ref, vg_ref, vb_ref, ws_ref, bs_ref, wo_ref, lng_ref, lnb_ref,
                    y_ref, vn_ref, v_scr, og_scr, h_scr, decode, mix_block):
    y_ref[...] = _layer_norm(h_scr[...], lng_ref[...], lnb_ref[...])

    n_chunks = x_ref.shape[0] // CHUNK_B
    x = x_ref[...]
    xb = x.astype(BF16)

    half = x_ref.shape[0] // 2

    def stage_v(gp):
        for r in range(2):
            rows = slice(r * half, (r + 1) * half)
            v_scr[gp, rows] = _gelu_tanh(_dot(xb[rows], w_ref[N_GROUPS + gp]))
            yield

    for gp in range(N_GROUPS):
        _interleave([stage_v(gp)] + ([decode[0](gp)] if decode is not None else []))

    s1 = jnp.sum(v_scr[0], axis=-1, keepdims=True)
    for gp in range(1, N_GROUPS):
        s1 = s1 + jnp.sum(v_scr[gp], axis=-1, keepdims=True)
    mu = s1 * (1.0 / E)
    s2 = jnp.zeros_like(mu)
    for gp in range(N_GROUPS):
        d = v_scr[gp] - mu
        s2 = s2 + jnp.sum(d * d, axis=-1, keepdims=True)
    rstd = lax.rsqrt(s2 * (1.0 / E) + LN_EPS)

    _, causal = _block_masks(CHUNK_B, mix_block)
    ws_reps = CHUNK_B // ws_ref.shape[1]
    n_mix_groups = bs_ref.shape[0]
    bias_cols = jnp.concatenate(
        [bs_ref[...], jnp.zeros((CHUNK_B - n_mix_groups, CHUNK_B), F32)], axis=0).T

    def stage_gate(gp):
        return _dot(xb, w_ref[gp]), _dot(xb, w_ref[2 * N_GROUPS + gp])

    def stage_mix(gp, gate):
        u_pre, z = gate
        vn = (v_scr[gp] - mu) * rstd * vg_ref[gp] + vb_ref[gp]
        if vn_ref is not None:
            vn_ref[:, gp * LANE_GROUP:(gp + 1) * LANE_GROUP] = vn
        vnb = vn.astype(BF16)
        cols = []
        for i in range(HEADS_PER_GROUP):
            ls = slice(i * HEAD, (i + 1) * HEAD)
            g = HEADS_PER_GROUP * gp + i
            wg = jnp.concatenate([ws_ref[g]] * ws_reps, axis=0) if ws_reps > 1 else ws_ref[g]
            wc = jnp.where(causal, wg, 0.0).astype(BF16)
            bias = bias_cols[:, g:g + 1]
            side_by_side = jnp.concatenate(
                [vnb[c * CHUNK_B:(c + 1) * CHUNK_B, ls] for c in range(n_chunks)], axis=1)
            m = _dot(wc, side_by_side) + bias
            cols.append(jnp.concatenate([m[:, c * HEAD:(c + 1) * HEAD] for c in range(n_chunks)], axis=0))
        mixed = jnp.concatenate(cols, axis=1)
        og_scr[gp] = (_gelu_tanh(u_pre) * mixed * _silu(z)).astype(BF16)

    def out_proj(groups):
        y = _dot(og_scr[groups[0]], wo_ref[groups[0]])
        for gp in groups[1:]:
            y = y + _dot(og_scr[gp], wo_ref[gp])
        return y

    y_parts = []
    gates = {0: stage_gate(0)}
    for gp in range(N_GROUPS):
        if gp + 1 < N_GROUPS:
            gates[gp + 1] = stage_gate(gp + 1)
        stage_mix(gp, gates.pop(gp))
        if decode is not None:
            decode[1](gp)
        if gp == N_GROUPS - 2:
            y_parts.append(out_proj(list(range(0, N_GROUPS - 3))))
    y_parts.append(out_proj([N_GROUPS - 3, N_GROUPS - 2]))
    y_parts.append(out_proj([N_GROUPS - 1]))
    y = y_parts[0] + y_parts[1] + y_parts[2]
    h_scr[...] = ALPHA * x + y


def _gmlp(x, w_in, vg3, vb3, ws, bs_rows, w_out, lng, lnb, emit_v, mix_block=CHUNK_B, decode=None,
          decode_seq=0):
    rows = x.shape[0]
    tile = min(GMLP_TILE, rows)
    assert tile == STAGE_ROWS and N_GROUPS == STAGE_SLOTS
    n_tiles = rows // tile
    tile_in = lambda s: jnp.minimum(s, n_tiles - 1)
    tile_out = lambda s: jnp.maximum(s - 1, 0)
    hbm = pl.BlockSpec(memory_space=pl.ANY)
    in_specs = [
        pl.BlockSpec((tile, D_MODEL), lambda s: (tile_in(s), 0)),
        hbm, _resident(vg3.shape), _resident(vb3.shape), _resident(ws.shape),
        _resident(bs_rows.shape), hbm, _resident(lng.shape), _resident(lnb.shape),
    ]
    operands = [x, w_in, vg3, vb3, ws, bs_rows, w_out, lng, lnb]
    w_scr, wo_scr, _, sems = _weight_scratch(w_in, w_out)
    assert len(operands) == N_GMLP_INPUTS
    out_specs = [pl.BlockSpec((tile, D_MODEL), lambda s: (tile_out(s), 0))]
    out_shape = [jax.ShapeDtypeStruct(x.shape, F32)]
    if emit_v:
        out_specs.append(pl.BlockSpec((tile, E), lambda s: (tile_in(s), 0)))
        out_shape.append(jax.ShapeDtypeStruct((rows, E), F32))
    if decode is not None:
        assert len(decode) == N_DECODE_INPUTS
        state = decode[-1]
        n_seq = state.shape[0]
        seqs_per_step = n_seq // n_tiles
        rows_per_step = seqs_per_step * decode_seq
        assert n_seq % n_tiles == 0 and rows_per_step % SUBLANES == 0 and SUBLANES % decode_seq == 0
        tok_spec = pl.BlockSpec((N_GROUPS, rows_per_step, LANE_GROUP), lambda s: (0, tile_in(s), 0))
        st_spec = pl.BlockSpec((seqs_per_step, N_HEADS, HEAD, HEAD), lambda s: (tile_in(s), 0, 0, 0))
        in_specs += [tok_spec] * (N_DECODE_INPUTS - 1) + [st_spec]
        operands += list(decode)
        out_specs += [tok_spec, st_spec]
        out_shape += [jax.ShapeDtypeStruct(decode[0].shape, F32), jax.ShapeDtypeStruct(state.shape, F32)]
    return pl.pallas_call(
        functools.partial(_gmlp_kernel, emit_v=emit_v, decode_seq=decode_seq if decode is not None else 0,
                          mix_block=mix_block),
        grid=(n_tiles + 1,),
        in_specs=in_specs,
        out_specs=out_specs,
        out_shape=out_shape,
        scratch_shapes=[
            pltpu.VMEM((N_GROUPS, tile, LANE_GROUP), F32),
            pltpu.VMEM((N_GROUPS, tile, LANE_GROUP), BF16),
            pltpu.VMEM((tile, D_MODEL), F32),
            w_scr, wo_scr, sems,
        ],
        compiler_params=pltpu.CompilerParams(
            dimension_semantics=("arbitrary",), vmem_limit_bytes=VMEM_LIMIT_BYTES),
        name="gmlp_emit_v" if emit_v else "gmlp",
    )(*operands)


def _lane_groups_of_rows(w):
    return w.reshape(w.shape[0] // LANE_GROUP, LANE_GROUP, w.shape[1])


def _lane_groups_of_vector(v):
    return v.reshape(v.shape[0], N_GROUPS, LANE_GROUP).transpose(1, 0, 2)


def kernel(x_prompt, x_sample, state_hgrn, w_in_a, lb_logits_a, gnorm_a, w_out_a, w_in_b, lnv_g_b,
           lnv_b_b, w_s_b, b_s_b, w_out_b, ln_g, ln_b):
    bsz, seq, _ = x_prompt.shape
    n_seq, dec_seq, _ = x_sample.shape

    wa, woa = w_in_a[0], w_out_a[0]
    lbl3 = _lane_groups_of_vector(lb_logits_a)
    gn3 = _lane_groups_of_vector(gnorm_a[0:1])
    lng0, lnb0 = ln_g[0:1], ln_b[0:1]

    hp, sp = _hgrn_prompt(x_prompt, wa, lbl3, gn3, woa, lng0, lnb0, layer=0)

    xs = x_sample.reshape(n_seq * dec_seq, D_MODEL)
    decode_gates = _hgrn_sample_gates(xs, wa, lbl3, gn3, layer=0, dec_seq=dec_seq)

    w3b, wo3b = w_in_b[0], w_out_b[0]
    vg3 = _lane_groups_of_vector(lnv_g_b[0:1])
    vb3 = _lane_groups_of_vector(lnv_b_b[0:1])
    lng1, lnb1 = ln_g[1:2], ln_b[1:2]
    ws = w_s_b[0]
    bs = b_s_b[0]

    yp, og, ss = _gmlp(hp.reshape(bsz * seq, D_MODEL), w3b, vg3, vb3, ws, bs, wo3b, lng1, lnb1,
                       emit_v=False, decode=(*decode_gates, state_hgrn[0]), decode_seq=dec_seq)
    yp = yp.reshape(bsz, seq, D_MODEL)
    hs = _out_proj_ln(og, xs, _lane_groups_of_rows(woa), lng0, lnb0)

    reps = CHUNK_B // dec_seq
    ws_dec = jnp.tile(ws[:, :dec_seq, :dec_seq], (1, SUBLANES // dec_seq, reps))
    bs_dec = jnp.tile(bs[:, :dec_seq], (1, reps))
    ys, vn = _gmlp(hs, w3b, vg3, vb3, ws_dec, bs_dec, wo3b, lng1, lnb1, emit_v=True, mix_block=dec_seq)
    ys = ys.reshape(n_seq, dec_seq, D_MODEL)
    vs = vn.reshape(n_seq, dec_seq, E)

    return (yp, ys, sp[None], ss[None], vs[None])
```
